```python
import math
import numpy as np
import jax
import jax.numpy as jnp
from jax import lax

D_MODEL = 1024
BATCH = 8
SEQ = 4096
DEPTH = 2
DEC_BATCH = 128
DEC_SEQ = 4
PAST_LEN = 16384
PAGE_SIZE = 128

MLSTM_W = D_MODEL // 2
MLSTM_HEADS = 4
MLSTM_DH = MLSTM_W // MLSTM_HEADS
MLSTM_CHUNK = 64
SWA_W = D_MODEL // 4
SWA_DH = 64
SWA_HEADS = SWA_W // SWA_DH
SWA_KV_HEADS = SWA_HEADS // 2
SWA_GROUP = SWA_HEADS // SWA_KV_HEADS
WINDOW = 128
CONV_W = D_MODEL - MLSTM_W - SWA_W
CONV_WIDTH = 3
MIX_W = MLSTM_W + SWA_W + CONV_W
IN_SPLITS = (MLSTM_W, MLSTM_W, MLSTM_W, MLSTM_W, MLSTM_HEADS, MLSTM_HEADS,
             SWA_W, SWA_KV_HEADS * SWA_DH, SWA_KV_HEADS * SWA_DH,
             CONV_W, CONV_W, CONV_W)
IN_W = sum(IN_SPLITS)
PEER_HEADS = 8
PEER_NKEYS = 128
PEER_EXPERTS = PEER_NKEYS * PEER_NKEYS
PEER_DKEY = 256
PEER_TOPK = 16
PEER_BLOCK = 256
EPS = 1e-6

kernel_name = 'hybrid_mlstm_swa_conv_peer_step'


def rmsnorm(x, g):
    xf = x.astype(jnp.float32)
    y = xf * lax.rsqrt(jnp.mean(jnp.square(xf), axis=-1, keepdims=True) + EPS)
    return (y * g.astype(jnp.float32)).astype(x.dtype)


def alibi_slopes():
    h = jnp.arange(1, SWA_HEADS + 1, dtype=jnp.float32)
    return jnp.exp2(-8.0 * h / SWA_HEADS).reshape(SWA_KV_HEADS, SWA_GROUP)


def _mlstm_chunk(carry, xs):
    C, n, m = carry
    q, k, v, ig, lf = xs
    L = q.shape[2]
    b = jnp.cumsum(lf, axis=-1)
    a = b + m[..., None]
    d = b[..., :, None] - b[..., None, :] + ig[..., None, :]
    causal = jnp.tril(jnp.ones((L, L), dtype=bool))
    d = jnp.where(causal, d, -jnp.inf)
    m_t = jnp.maximum(a, jnp.max(d, axis=-1))
    w_inter = jnp.exp(a - m_t)
    w_intra = jnp.exp(d - m_t[..., None])
    qk = jnp.einsum('bhtk,bhsk->bhts', q, k) * w_intra
    num = w_inter[..., None] * jnp.einsum('bhvk,bhtk->bhtv', C, q) + jnp.einsum('bhts,bhsv->bhtv', qk, v)
    den = w_inter * jnp.einsum('bhk,bhtk->bht', n, q) + jnp.sum(qk, axis=-1)
    h = num / jnp.maximum(jnp.abs(den), jnp.exp(-m_t))[..., None]
    wl_inter = w_inter[..., -1]
    wl_intra = w_intra[..., -1, :]
    C_new = wl_inter[..., None, None] * C + jnp.einsum('bhs,bhsv,bhsk->bhvk', wl_intra, v, k)
    n_new = wl_inter[..., None] * n + jnp.einsum('bhs,bhsk->bhk', wl_intra, k)
    return (C_new, n_new, m_t[..., -1]), h


def mlstm(q, k, v, ig, lf, C0, n0, m0):
    B, T = q.shape[:2]
    L = math.gcd(T, MLSTM_CHUNK)
    nc = T // L

    def to_chunks(t):
        t = jnp.swapaxes(t, 1, 2)
        t = t.reshape(t.shape[:2] + (nc, L) + t.shape[3:])
        return jnp.moveaxis(t, 2, 0)

    carry0 = (C0.astype(jnp.float32), n0.astype(jnp.float32), m0.astype(jnp.float32))
    xs = (to_chunks(q), to_chunks(k), to_chunks(v), to_chunks(ig), to_chunks(lf))
    (C1, n1, m1), h = lax.scan(_mlstm_chunk, carry0, xs)
    h = jnp.moveaxis(h, 0, 2).reshape(B, MLSTM_HEADS, T, MLSTM_DH)
    return jnp.swapaxes(h, 1, 2), C1, n1, m1


def swa_attend(q, k, v, q_pos, k_pos, sinks):
    s = jnp.einsum('...qhgd,...khd->...hgqk', q, k).astype(jnp.float32) * (SWA_DH ** -0.5)
    dist = q_pos[..., :, None] - k_pos[..., None, :]
    visible = (dist >= 0) & (dist <= WINDOW) & (k_pos[..., None, :] >= 0)
    s = s - alibi_slopes()[:, :, None, None] * jnp.expand_dims(dist, (-4, -3)).astype(jnp.float32)
    s = jnp.where(jnp.expand_dims(visible, (-4, -3)), s, -jnp.inf)
    sink = jnp.broadcast_to(sinks.astype(jnp.float32).reshape(SWA_KV_HEADS, SWA_GROUP, 1, 1), s.shape[:-1] + (1,))
    p = jax.nn.softmax(jnp.concatenate([s, sink], axis=-1), axis=-1)[..., :-1]
    return jnp.einsum('...hgqk,...khd->...qhgd', p.astype(v.dtype), v)


def swa_banded(q, k, v, sinks):
    B, S = q.shape[:2]
    nb = S // WINDOW
    qb = q.reshape(B, nb, WINDOW, SWA_KV_HEADS, SWA_GROUP, SWA_DH)
    kb = k.reshape(B, nb, WINDOW, SWA_KV_HEADS, SWA_DH)
    vb = v.reshape(B, nb, WINDOW, SWA_KV_HEADS, SWA_DH)
    pad = ((0, 0), (1, 0), (0, 0), (0, 0), (0, 0))
    kband = jnp.concatenate([jnp.pad(kb[:, :-1], pad), kb], axis=2)
    vband = jnp.concatenate([jnp.pad(vb[:, :-1], pad), vb], axis=2)
    pos = jnp.arange(S, dtype=jnp.int32).reshape(nb, WINDOW)
    kpos = jnp.concatenate([pos - WINDOW, pos], axis=1)
    o = swa_attend(qb, kband, vband, pos, kpos, sinks)
    return o.reshape(B, S, SWA_KV_HEADS, SWA_GROUP, SWA_DH)


def short_conv(u, buf, w):
    T = u.shape[1]
    full = jnp.concatenate([buf.astype(u.dtype), u], axis=1)
    y = sum(w[j] * full[:, j:j + T] for j in range(CONV_WIDTH))
    return y, full[:, T:]


def token_mixers(h, w_in, w_out, gate_b, mh_g, sinks, conv_w, state):
    B, T, _ = h.shape
    f32 = jnp.float32
    z = h @ w_in
    offs = np.cumsum(IN_SPLITS)[:-1].tolist()
    mq, mk, mv, mo, mi, mf, sq, sk, sv, cb, cc, ch = jnp.split(z, offs, axis=-1)
    if state is None:
        kbuf = vbuf = None
        cbuf = jnp.zeros((B, CONV_WIDTH - 1, CONV_W), h.dtype)
        C0 = jnp.zeros((B, MLSTM_HEADS, MLSTM_DH, MLSTM_DH), f32)
        n0 = jnp.zeros((B, MLSTM_HEADS, MLSTM_DH), f32)
        m0 = jnp.zeros((B, MLSTM_HEADS), f32)
        win_rows = min(WINDOW, PAST_LEN)
    else:
        kbuf, vbuf, cbuf, C0, n0, m0 = state
        win_rows = kbuf.shape[1]

    shp = (B, T, MLSTM_HEADS, MLSTM_DH)
    q = mq.reshape(shp).astype(f32)
    k = mk.reshape(shp).astype(f32) * (MLSTM_DH ** -0.5)
    v = mv.reshape(shp).astype(f32)
    pre = jnp.concatenate([mi, mf], axis=-1).astype(f32) + gate_b.astype(f32)
    ig = pre[..., :MLSTM_HEADS]
    lf = jax.nn.log_sigmoid(pre[..., MLSTM_HEADS:])
    hm, C1, n1, m1 = mlstm(q, k, v, ig, lf, C0, n0, m0)
    hm = hm * lax.rsqrt(jnp.mean(jnp.square(hm), axis=-1, keepdims=True) + EPS) * mh_g.astype(f32).reshape(MLSTM_HEADS, MLSTM_DH)
    hm = (hm * jax.nn.sigmoid(mo.reshape(shp).astype(f32))).astype(h.dtype).reshape(B, T, MLSTM_W)

    qa = sq.reshape(B, T, SWA_KV_HEADS, SWA_GROUP, SWA_DH)
    ka = sk.reshape(B, T, SWA_KV_HEADS, SWA_DH)
    va = sv.reshape(B, T, SWA_KV_HEADS, SWA_DH)
    if kbuf is None:
        oa = swa_banded(qa, ka, va, sinks)
        k_all, v_all = ka, va
    else:
        k_all = jnp.concatenate([kbuf.astype(ka.dtype), ka], axis=1)
        v_all = jnp.concatenate([vbuf.astype(va.dtype), va], axis=1)
        q_pos = PAST_LEN + jnp.arange(T, dtype=jnp.int32)
        k_pos = PAST_LEN - win_rows + jnp.arange(win_rows + T, dtype=jnp.int32)
        oa = swa_attend(qa, k_all, v_all, q_pos, k_pos, sinks)
    oa = oa.reshape(B, T, SWA_W)
    k_keep = k_all[:, -win_rows:]
    v_keep = v_all[:, -win_rows:]

    yconv, cbuf_new = short_conv(cc * ch, cbuf, conv_w)
    yc = cb * yconv

    out = jnp.concatenate([hm, oa, yc], axis=-1) @ w_out
    return out, (k_keep, v_keep, cbuf_new, C1, n1, m1)


def peer_ffn(h, wq, subkeys, u_tab, v_tab):
    B, T, D = h.shape
    n = B * T
    nb = -(-n // PEER_BLOCK)
    x = jnp.pad(h.reshape(n, D), ((0, nb * PEER_BLOCK - n), (0, 0))).reshape(nb, PEER_BLOCK, D)

    def block(xb):
        q = (xb @ wq).reshape(PEER_BLOCK, PEER_HEADS, 2, PEER_DKEY // 2)
        s = jnp.einsum('thpd,pkd->thpk', q, subkeys).astype(jnp.float32)
        s1, i1 = lax.top_k(s[:, :, 0], PEER_TOPK)
        s2, i2 = lax.top_k(s[:, :, 1], PEER_TOPK)
        cand = (s1[..., :, None] + s2[..., None, :]).reshape(PEER_BLOCK, PEER_HEADS, PEER_TOPK * PEER_TOPK)
        cidx = (i1[..., :, None] * PEER_NKEYS + i2[..., None, :]).reshape(PEER_BLOCK, PEER_HEADS, PEER_TOPK * PEER_TOPK)
        top_s, sel = lax.top_k(cand, PEER_TOPK)
        idx = jnp.take_along_axis(cidx, sel, axis=-1)
        g = jax.nn.softmax(top_s, axis=-1)
        u = u_tab[idx]
        act = jax.nn.gelu(jnp.einsum('thkd,td->thk', u, xb).astype(jnp.float32))
        vv = v_tab[idx]
        return jnp.einsum('thk,thkd->td', (g * act).astype(vv.dtype), vv)

    y = lax.map(block, x)
    return y.reshape(nb * PEER_BLOCK, D)[:n].reshape(B, T, D)


def trunk_layer(x, c, ada_w, ada_b, g_mix, g_ffn, w_in, w_out, gate_b, mh_g, sinks, conv_w,
                wq, subkeys, u_tab, v_tab, state):
    mod = (jax.nn.silu(c) @ ada_w + ada_b)[:, None, :]
    sh1, sc1, gt1, sh2, sc2, gt2 = jnp.split(mod, 6, axis=-1)
    h = rmsnorm(x, g_mix) * (1 + sc1) + sh1
    mix, new_state = token_mixers(h, w_in, w_out, gate_b, mh_g, sinks, conv_w, state)
    x = x + gt1 * mix
    h = rmsnorm(x, g_ffn) * (1 + sc2) + sh2
    x = x + gt2 * peer_ffn(h, wq, subkeys, u_tab, v_tab)
    return x, new_state


def setup_inputs(seed: int = 0) -> dict:
    key = jax.random.key(seed)
    ks = jax.random.split(key, 26)
    f32 = jnp.float32
    D = D_MODEL
    win_rows = min(WINDOW, PAST_LEN)

    def nrm(k, shape, s):
        return jax.random.normal(k, shape, f32) * s

    gate_b = jnp.concatenate([
        nrm(ks[0], (DEPTH, MLSTM_HEADS), 0.1),
        jnp.linspace(3.0, 6.0, MLSTM_HEADS, dtype=f32)[None, :] + nrm(ks[1], (DEPTH, MLSTM_HEADS), 0.1)], axis=-1)
    return {
        'x_prompt': nrm(ks[2], (BATCH, SEQ, D), 1.0),
        'x_sample': nrm(ks[3], (DEC_BATCH, DEC_SEQ, D), 1.0),
        'cache_swa_k': nrm(ks[4], (DEPTH, DEC_BATCH, win_rows, SWA_KV_HEADS, SWA_DH), 1.0),
        'cache_swa_v': nrm(ks[5], (DEPTH, DEC_BATCH, win_rows, SWA_KV_HEADS, SWA_DH), 1.0),
        'state_conv': nrm(ks[6], (DEPTH, DEC_BATCH, CONV_WIDTH - 1, CONV_W), 1.0),
        'state_mlstm_C': nrm(ks[7], (DEPTH, DEC_BATCH, MLSTM_HEADS, MLSTM_DH, MLSTM_DH), 0.5),
        'state_mlstm_n': nrm(ks[8], (DEPTH, DEC_BATCH, MLSTM_HEADS, MLSTM_DH), 0.5),
        'state_mlstm_m': nrm(ks[9], (DEPTH, DEC_BATCH, MLSTM_HEADS), 1.0),
        'c_prompt': nrm(ks[10], (BATCH, D), 1.0),
        'c_sample': nrm(ks[11], (DEC_BATCH, D), 1.0),
        'ada_w': nrm(ks[12], (DEPTH, D, 6 * D), D ** -0.5),
        'ada_b': nrm(ks[13], (DEPTH, 6 * D), 0.01),
        'norm_mix_g': 1.0 + nrm(ks[14], (DEPTH, D), 0.01),
        'norm_ffn_g': 1.0 + nrm(ks[15], (DEPTH, D), 0.01),
        'w_in': nrm(ks[16], (DEPTH, D, IN_W), D ** -0.5),
        'w_out': nrm(ks[17], (DEPTH, MIX_W, D), MIX_W ** -0.5),
        'mlstm_gate_b': gate_b,
        'mlstm_norm_g': 1.0 + nrm(ks[18], (DEPTH, MLSTM_W), 0.01),
        'swa_sinks': nrm(ks[19], (DEPTH, SWA_HEADS), 0.5),
        'conv_w': nrm(ks[20], (DEPTH, CONV_WIDTH, CONV_W), CONV_WIDTH ** -0.5),
        'peer_wq': nrm(ks[21], (DEPTH, D, PEER_HEADS * PEER_DKEY), D ** -0.5),
        'peer_subkeys': nrm(ks[22], (DEPTH, 2, PEER_NKEYS, PEER_DKEY // 2), (PEER_DKEY // 2) ** -0.5),
        'peer_u': nrm(ks[23], (DEPTH, PEER_EXPERTS, D), D ** -0.5),
        'peer_v': nrm(ks[24], (DEPTH, PEER_EXPERTS, D), PEER_HEADS ** -0.5),
        'final_g': 1.0 + nrm(ks[25], (D,), 0.01),
    }


def reference(x_prompt, x_sample, cache_swa_k, cache_swa_v, state_conv, state_mlstm_C, state_mlstm_n,
              state_mlstm_m, c_prompt, c_sample, ada_w, ada_b, norm_mix_g, norm_ffn_g, w_in, w_out,
              mlstm_gate_b, mlstm_norm_g, swa_sinks, conv_w, peer_wq, peer_subkeys, peer_u, peer_v, final_g):
    xp, xs = x_prompt, x_sample
    new_p, new_s = [], []
    for l in range(DEPTH):
        w = (ada_w[l], ada_b[l], norm_mix_g[l], norm_ffn_g[l], w_in[l], w_out[l], mlstm_gate_b[l],
             mlstm_norm_g[l], swa_sinks[l], conv_w[l], peer_wq[l], peer_subkeys[l], peer_u[l], peer_v[l])
        xp, sp = trunk_layer(xp, c_prompt, *w, None)
        st = (cache_swa_k[l], cache_swa_v[l], state_conv[l], state_mlstm_C[l], state_mlstm_n[l], state_mlstm_m[l])
        xs, ss = trunk_layer(xs, c_sample, *w, st)
        new_p.append(sp)
        new_s.append(ss)
    y_prompt = rmsnorm(xp, final_g)
    y_sample = rmsnorm(xs, final_g)
    pk, pv, pc, pC, pn, pm = [jnp.stack(t) for t in zip(*new_p)]
    sk, sv, sc, sC, sn, sm = [jnp.stack(t) for t in zip(*new_s)]
    return (y_prompt, y_sample, pk, pv, pc, pC, pn, pm, sk, sv, sc, sC, sn, sm)
```

```python
import math
import numpy as np
import jax
import jax.numpy as jnp
from jax import lax
from jax.experimental import pallas as pl

D_MODEL = 1024
BATCH = 8
SEQ = 4096
DEPTH = 2
DEC_BATCH = 128
DEC_SEQ = 4
PAST_LEN = 16384

MLSTM_W = D_MODEL // 2
MLSTM_HEADS = 4
MLSTM_DH = MLSTM_W // MLSTM_HEADS
MLSTM_CHUNK = 64
SWA_W = D_MODEL // 4
SWA_DH = 64
SWA_HEADS = SWA_W // SWA_DH
SWA_KV_HEADS = SWA_HEADS // 2
SWA_GROUP = SWA_HEADS // SWA_KV_HEADS
WINDOW = 128
CONV_W = D_MODEL - MLSTM_W - SWA_W
CONV_WIDTH = 3
MIX_W = MLSTM_W + SWA_W + CONV_W
IN_SPLITS = (MLSTM_W, MLSTM_W, MLSTM_W, MLSTM_W, MLSTM_HEADS, MLSTM_HEADS,
             SWA_W, SWA_KV_HEADS * SWA_DH, SWA_KV_HEADS * SWA_DH,
             CONV_W, CONV_W, CONV_W)
IN_W = sum(IN_SPLITS)
PEER_HEADS = 8
PEER_NKEYS = 128
PEER_EXPERTS = PEER_NKEYS * PEER_NKEYS
PEER_DKEY = 256
PEER_TOPK = 16
PEER_BLOCK = 256
EPS = 1e-6


def rmsnorm(x, g):
    xf = x.astype(jnp.float32)
    y = xf * lax.rsqrt(jnp.mean(jnp.square(xf), axis=-1, keepdims=True) + EPS)
    return (y * g.astype(jnp.float32)).astype(x.dtype)


def alibi_slopes():
    h = jnp.arange(1, SWA_HEADS + 1, dtype=jnp.float32)
    return jnp.exp2(-8.0 * h / SWA_HEADS).reshape(SWA_KV_HEADS, SWA_GROUP)


def _mlstm_chunk(carry, xs):
    C, n, m = carry
    q, k, v, ig, lf = xs
    L = q.shape[2]
    b = jnp.cumsum(lf, axis=-1)
    a = b + m[..., None]
    d = b[..., :, None] - b[..., None, :] + ig[..., None, :]
    causal = jnp.tril(jnp.ones((L, L), dtype=bool))
    d = jnp.where(causal, d, -jnp.inf)
    m_t = jnp.maximum(a, jnp.max(d, axis=-1))
    w_inter = jnp.exp(a - m_t)
    w_intra = jnp.exp(d - m_t[..., None])
    qk = jnp.einsum('bhtk,bhsk->bhts', q, k) * w_intra
    num = w_inter[..., None] * jnp.einsum('bhvk,bhtk->bhtv', C, q) + jnp.einsum('bhts,bhsv->bhtv', qk, v)
    den = w_inter * jnp.einsum('bhk,bhtk->bht', n, q) + jnp.sum(qk, axis=-1)
    h = num / jnp.maximum(jnp.abs(den), jnp.exp(-m_t))[..., None]
    wl_inter = w_inter[..., -1]
    wl_intra = w_intra[..., -1, :]
    C_new = wl_inter[..., None, None] * C + jnp.einsum('bhs,bhsv,bhsk->bhvk', wl_intra, v, k)
    n_new = wl_inter[..., None] * n + jnp.einsum('bhs,bhsk->bhk', wl_intra, k)
    return (C_new, n_new, m_t[..., -1]), h


def mlstm(q, k, v, ig, lf, C0, n0, m0):
    B, T = q.shape[:2]
    L = math.gcd(T, MLSTM_CHUNK)
    nc = T // L

    def to_chunks(t):
        t = jnp.swapaxes(t, 1, 2)
        t = t.reshape(t.shape[:2] + (nc, L) + t.shape[3:])
        return jnp.moveaxis(t, 2, 0)

    carry0 = (C0.astype(jnp.float32), n0.astype(jnp.float32), m0.astype(jnp.float32))
    xs = (to_chunks(q), to_chunks(k), to_chunks(v), to_chunks(ig), to_chunks(lf))
    (C1, n1, m1), h = lax.scan(_mlstm_chunk, carry0, xs)
    h = jnp.moveaxis(h, 0, 2).reshape(B, MLSTM_HEADS, T, MLSTM_DH)
    return jnp.swapaxes(h, 1, 2), C1, n1, m1


def swa_attend(q, k, v, q_pos, k_pos, sinks):
    s = jnp.einsum('...qhgd,...khd->...hgqk', q, k).astype(jnp.float32) * (SWA_DH ** -0.5)
    dist = q_pos[..., :, None] - k_pos[..., None, :]
    visible = (dist >= 0) & (dist <= WINDOW) & (k_pos[..., None, :] >= 0)
    s = s - alibi_slopes()[:, :, None, None] * jnp.expand_dims(dist, (-4, -3)).astype(jnp.float32)
    s = jnp.where(jnp.expand_dims(visible, (-4, -3)), s, -jnp.inf)
    sink = jnp.broadcast_to(sinks.astype(jnp.float32).reshape(SWA_KV_HEADS, SWA_GROUP, 1, 1), s.shape[:-1] + (1,))
    p = jax.nn.softmax(jnp.concatenate([s, sink], axis=-1), axis=-1)[..., :-1]
    return jnp.einsum('...hgqk,...khd->...qhgd', p.astype(v.dtype), v)


def swa_banded(q, k, v, sinks):
    B, S = q.shape[:2]
    nb = S // WINDOW
    qb = q.reshape(B, nb, WINDOW, SWA_KV_HEADS, SWA_GROUP, SWA_DH)
    kb = k.reshape(B, nb, WINDOW, SWA_KV_HEADS, SWA_DH)
    vb = v.reshape(B, nb, WINDOW, SWA_KV_HEADS, SWA_DH)
    pad = ((0, 0), (1, 0), (0, 0), (0, 0), (0, 0))
    kband = jnp.concatenate([jnp.pad(kb[:, :-1], pad), kb], axis=2)
    vband = jnp.concatenate([jnp.pad(vb[:, :-1], pad), vb], axis=2)
    pos = jnp.arange(S, dtype=jnp.int32).reshape(nb, WINDOW)
    kpos = jnp.concatenate([pos - WINDOW, pos], axis=1)
    o = swa_attend(qb, kband, vband, pos, kpos, sinks)
    return o.reshape(B, S, SWA_KV_HEADS, SWA_GROUP, SWA_DH)


def short_conv(u, buf, w):
    T = u.shape[1]
    full = jnp.concatenate([buf.astype(u.dtype), u], axis=1)
    y = sum(w[j] * full[:, j:j + T] for j in range(CONV_WIDTH))
    return y, full[:, T:]


def token_mixers(h, w_in, w_out, gate_b, mh_g, sinks, conv_w, state):
    B, T, _ = h.shape
    f32 = jnp.float32
    z = h @ w_in
    offs = np.cumsum(IN_SPLITS)[:-1].tolist()
    mq, mk, mv, mo, mi, mf, sq, sk, sv, cb, cc, ch = jnp.split(z, offs, axis=-1)
    if state is None:
        kbuf = vbuf = None
        cbuf = jnp.zeros((B, CONV_WIDTH - 1, CONV_W), h.dtype)
        C0 = jnp.zeros((B, MLSTM_HEADS, MLSTM_DH, MLSTM_DH), f32)
        n0 = jnp.zeros((B, MLSTM_HEADS, MLSTM_DH), f32)
        m0 = jnp.zeros((B, MLSTM_HEADS), f32)
        win_rows = min(WINDOW, PAST_LEN)
    else:
        kbuf, vbuf, cbuf, C0, n0, m0 = state
        win_rows = kbuf.shape[1]

    shp = (B, T, MLSTM_HEADS, MLSTM_DH)
    q = mq.reshape(shp).astype(f32)
    k = mk.reshape(shp).astype(f32) * (MLSTM_DH ** -0.5)
    v = mv.reshape(shp).astype(f32)
    pre = jnp.concatenate([mi, mf], axis=-1).astype(f32) + gate_b.astype(f32)
    ig = pre[..., :MLSTM_HEADS]
    lf = jax.nn.log_sigmoid(pre[..., MLSTM_HEADS:])
    hm, C1, n1, m1 = mlstm(q, k, v, ig, lf, C0, n0, m0)
    hm = hm * lax.rsqrt(jnp.mean(jnp.square(hm), axis=-1, keepdims=True) + EPS) * mh_g.astype(f32).reshape(MLSTM_HEADS, MLSTM_DH)
    hm = (hm * jax.nn.sigmoid(mo.reshape(shp).astype(f32))).astype(h.dtype).reshape(B, T, MLSTM_W)

    qa = sq.reshape(B, T, SWA_KV_HEADS, SWA_GROUP, SWA_DH)
    ka = sk.reshape(B, T, SWA_KV_HEADS, SWA_DH)
    va = sv.reshape(B, T, SWA_KV_HEADS, SWA_DH)
    if kbuf is None:
        oa = swa_banded(qa, ka, va, sinks)
        k_all, v_all = ka, va
    else:
        k_all = jnp.concatenate([kbuf.astype(ka.dtype), ka], axis=1)
        v_all = jnp.concatenate([vbuf.astype(va.dtype), va], axis=1)
        q_pos = PAST_LEN + jnp.arange(T, dtype=jnp.int32)
        k_pos = PAST_LEN - win_rows + jnp.arange(win_rows + T, dtype=jnp.int32)
        oa = swa_attend(qa, k_all, v_all, q_pos, k_pos, sinks)
    oa = oa.reshape(B, T, SWA_W)
    k_keep = k_all[:, -win_rows:]
    v_keep = v_all[:, -win_rows:]

    yconv, cbuf_new = short_conv(cc * ch, cbuf, conv_w)
    yc = cb * yconv

    out = jnp.concatenate([hm, oa, yc], axis=-1) @ w_out
    return out, (k_keep, v_keep, cbuf_new, C1, n1, m1)


def peer_ffn(h, wq, subkeys, u_tab, v_tab):
    B, T, D = h.shape
    n = B * T
    nb = -(-n // PEER_BLOCK)
    x = jnp.pad(h.reshape(n, D), ((0, nb * PEER_BLOCK - n), (0, 0))).reshape(nb, PEER_BLOCK, D)

    def block(xb):
        q = (xb @ wq).reshape(PEER_BLOCK, PEER_HEADS, 2, PEER_DKEY // 2)
        s = jnp.einsum('thpd,pkd->thpk', q, subkeys).astype(jnp.float32)
        s1, i1 = lax.top_k(s[:, :, 0], PEER_TOPK)
        s2, i2 = lax.top_k(s[:, :, 1], PEER_TOPK)
        cand = (s1[..., :, None] + s2[..., None, :]).reshape(PEER_BLOCK, PEER_HEADS, PEER_TOPK * PEER_TOPK)
        cidx = (i1[..., :, None] * PEER_NKEYS + i2[..., None, :]).reshape(PEER_BLOCK, PEER_HEADS, PEER_TOPK * PEER_TOPK)
        top_s, sel = lax.top_k(cand, PEER_TOPK)
        idx = jnp.take_along_axis(cidx, sel, axis=-1)
        g = jax.nn.softmax(top_s, axis=-1)
        u = u_tab[idx]
        act = jax.nn.gelu(jnp.einsum('thkd,td->thk', u, xb).astype(jnp.float32))
        vv = v_tab[idx]
        return jnp.einsum('thk,thkd->td', (g * act).astype(vv.dtype), vv)

    y = lax.map(block, x)
    return y.reshape(nb * PEER_BLOCK, D)[:n].reshape(B, T, D)


def trunk_layer(x, c, ada_w, ada_b, g_mix, g_ffn, w_in, w_out, gate_b, mh_g, sinks, conv_w,
                wq, subkeys, u_tab, v_tab, state):
    mod = (jax.nn.silu(c) @ ada_w + ada_b)[:, None, :]
    sh1, sc1, gt1, sh2, sc2, gt2 = jnp.split(mod, 6, axis=-1)
    h = rmsnorm(x, g_mix) * (1 + sc1) + sh1
    mix, new_state = token_mixers(h, w_in, w_out, gate_b, mh_g, sinks, conv_w, state)
    x = x + gt1 * mix
    h = rmsnorm(x, g_ffn) * (1 + sc2) + sh2
    x = x + gt2 * peer_ffn(h, wq, subkeys, u_tab, v_tab)
    return x, new_state


def _final_norm_kernel(x_ref, g_ref, o_ref):
    x = x_ref[...]
    y = x * lax.rsqrt(jnp.mean(jnp.square(x), axis=-1, keepdims=True) + EPS)
    o_ref[...] = y * g_ref[...]


def final_norm(x, g):
    shp = x.shape
    x2 = x.reshape(-1, shp[-1])
    n, d = x2.shape
    tm = 512
    out = pl.pallas_call(
        _final_norm_kernel,
        grid=(n // tm,),
        in_specs=[pl.BlockSpec((tm, d), lambda i: (i, 0)), pl.BlockSpec((1, d), lambda i: (0, 0))],
        out_specs=pl.BlockSpec((tm, d), lambda i: (i, 0)),
        out_shape=jax.ShapeDtypeStruct((n, d), x.dtype),
    )(x2, g.reshape(1, d))
    return out.reshape(shp)


def kernel(x_prompt, x_sample, cache_swa_k, cache_swa_v, state_conv, state_mlstm_C, state_mlstm_n, state_mlstm_m, c_prompt, c_sample, ada_w, ada_b, norm_mix_g, norm_ffn_g, w_in, w_out, mlstm_gate_b, mlstm_norm_g, swa_sinks, conv_w, peer_wq, peer_subkeys, peer_u, peer_v, final_g):
    xp, xs = x_prompt, x_sample
    new_p, new_s = [], []
    for l in range(DEPTH):
        w = (ada_w[l], ada_b[l], norm_mix_g[l], norm_ffn_g[l], w_in[l], w_out[l], mlstm_gate_b[l],
             mlstm_norm_g[l], swa_sinks[l], conv_w[l], peer_wq[l], peer_subkeys[l], peer_u[l], peer_v[l])
        xp, sp = trunk_layer(xp, c_prompt, *w, None)
        st = (cache_swa_k[l], cache_swa_v[l], state_conv[l], state_mlstm_C[l], state_mlstm_n[l], state_mlstm_m[l])
        xs, ss = trunk_layer(xs, c_sample, *w, st)
        new_p.append(sp)
        new_s.append(ss)
    y_prompt = final_norm(xp, final_g)
    y_sample = final_norm(xs, final_g)
    pk, pv, pc, pC, pn, pm = [jnp.stack(t) for t in zip(*new_p)]
    sk, sv, sc, sC, sn, sm = [jnp.stack(t) for t in zip(*new_s)]
    return (y_prompt, y_sample, pk, pv, pc, pC, pn, pm, sk, sv, sc, sC, sn, sm)
```

```python
import math
import numpy as np
import jax
import jax.numpy as jnp
from jax import lax
from jax.experimental import pallas as pl
from jax.experimental.pallas import tpu as pltpu

D_MODEL = 1024
BATCH = 8
SEQ = 4096
DEPTH = 2
DEC_BATCH = 128
DEC_SEQ = 4
PAST_LEN = 16384

MLSTM_W = D_MODEL // 2
MLSTM_HEADS = 4
MLSTM_DH = MLSTM_W // MLSTM_HEADS
MLSTM_CHUNK = 64
SWA_W = D_MODEL // 4
SWA_DH = 64
SWA_HEADS = SWA_W // SWA_DH
SWA_KV_HEADS = SWA_HEADS // 2
SWA_GROUP = SWA_HEADS // SWA_KV_HEADS
WINDOW = 128
CONV_W = D_MODEL - MLSTM_W - SWA_W
CONV_WIDTH = 3
MIX_W = MLSTM_W + SWA_W + CONV_W
IN_SPLITS = (MLSTM_W, MLSTM_W, MLSTM_W, MLSTM_W, MLSTM_HEADS, MLSTM_HEADS,
             SWA_W, SWA_KV_HEADS * SWA_DH, SWA_KV_HEADS * SWA_DH,
             CONV_W, CONV_W, CONV_W)
IN_W = sum(IN_SPLITS)
PEER_HEADS = 8
PEER_NKEYS = 128
PEER_EXPERTS = PEER_NKEYS * PEER_NKEYS
PEER_DKEY = 256
PEER_TOPK = 16
PEER_BLOCK = 256
EPS = 1e-6


def rmsnorm(x, g):
    xf = x.astype(jnp.float32)
    y = xf * lax.rsqrt(jnp.mean(jnp.square(xf), axis=-1, keepdims=True) + EPS)
    return (y * g.astype(jnp.float32)).astype(x.dtype)


def alibi_slopes():
    h = jnp.arange(1, SWA_HEADS + 1, dtype=jnp.float32)
    return jnp.exp2(-8.0 * h / SWA_HEADS).reshape(SWA_KV_HEADS, SWA_GROUP)


def _mlstm_chunk(carry, xs):
    C, n, m = carry
    q, k, v, ig, lf = xs
    L = q.shape[2]
    b = jnp.cumsum(lf, axis=-1)
    a = b + m[..., None]
    d = b[..., :, None] - b[..., None, :] + ig[..., None, :]
    causal = jnp.tril(jnp.ones((L, L), dtype=bool))
    d = jnp.where(causal, d, -jnp.inf)
    m_t = jnp.maximum(a, jnp.max(d, axis=-1))
    w_inter = jnp.exp(a - m_t)
    w_intra = jnp.exp(d - m_t[..., None])
    qk = jnp.einsum('bhtk,bhsk->bhts', q, k) * w_intra
    num = w_inter[..., None] * jnp.einsum('bhvk,bhtk->bhtv', C, q) + jnp.einsum('bhts,bhsv->bhtv', qk, v)
    den = w_inter * jnp.einsum('bhk,bhtk->bht', n, q) + jnp.sum(qk, axis=-1)
    h = num / jnp.maximum(jnp.abs(den), jnp.exp(-m_t))[..., None]
    wl_inter = w_inter[..., -1]
    wl_intra = w_intra[..., -1, :]
    C_new = wl_inter[..., None, None] * C + jnp.einsum('bhs,bhsv,bhsk->bhvk', wl_intra, v, k)
    n_new = wl_inter[..., None] * n + jnp.einsum('bhs,bhsk->bhk', wl_intra, k)
    return (C_new, n_new, m_t[..., -1]), h


def mlstm(q, k, v, ig, lf, C0, n0, m0):
    B, T = q.shape[:2]
    L = math.gcd(T, MLSTM_CHUNK)
    nc = T // L

    def to_chunks(t):
        t = jnp.swapaxes(t, 1, 2)
        t = t.reshape(t.shape[:2] + (nc, L) + t.shape[3:])
        return jnp.moveaxis(t, 2, 0)

    carry0 = (C0.astype(jnp.float32), n0.astype(jnp.float32), m0.astype(jnp.float32))
    xs = (to_chunks(q), to_chunks(k), to_chunks(v), to_chunks(ig), to_chunks(lf))
    (C1, n1, m1), h = lax.scan(_mlstm_chunk, carry0, xs)
    h = jnp.moveaxis(h, 0, 2).reshape(B, MLSTM_HEADS, T, MLSTM_DH)
    return jnp.swapaxes(h, 1, 2), C1, n1, m1


def swa_attend(q, k, v, q_pos, k_pos, sinks):
    s = jnp.einsum('...qhgd,...khd->...hgqk', q, k).astype(jnp.float32) * (SWA_DH ** -0.5)
    dist = q_pos[..., :, None] - k_pos[..., None, :]
    visible = (dist >= 0) & (dist <= WINDOW) & (k_pos[..., None, :] >= 0)
    s = s - alibi_slopes()[:, :, None, None] * jnp.expand_dims(dist, (-4, -3)).astype(jnp.float32)
    s = jnp.where(jnp.expand_dims(visible, (-4, -3)), s, -jnp.inf)
    sink = jnp.broadcast_to(sinks.astype(jnp.float32).reshape(SWA_KV_HEADS, SWA_GROUP, 1, 1), s.shape[:-1] + (1,))
    p = jax.nn.softmax(jnp.concatenate([s, sink], axis=-1), axis=-1)[..., :-1]
    return jnp.einsum('...hgqk,...khd->...qhgd', p.astype(v.dtype), v)


def swa_banded(q, k, v, sinks):
    B, S = q.shape[:2]
    nb = S // WINDOW
    qb = q.reshape(B, nb, WINDOW, SWA_KV_HEADS, SWA_GROUP, SWA_DH)
    kb = k.reshape(B, nb, WINDOW, SWA_KV_HEADS, SWA_DH)
    vb = v.reshape(B, nb, WINDOW, SWA_KV_HEADS, SWA_DH)
    pad = ((0, 0), (1, 0), (0, 0), (0, 0), (0, 0))
    kband = jnp.concatenate([jnp.pad(kb[:, :-1], pad), kb], axis=2)
    vband = jnp.concatenate([jnp.pad(vb[:, :-1], pad), vb], axis=2)
    pos = jnp.arange(S, dtype=jnp.int32).reshape(nb, WINDOW)
    kpos = jnp.concatenate([pos - WINDOW, pos], axis=1)
    o = swa_attend(qb, kband, vband, pos, kpos, sinks)
    return o.reshape(B, S, SWA_KV_HEADS, SWA_GROUP, SWA_DH)


def short_conv(u, buf, w):
    T = u.shape[1]
    full = jnp.concatenate([buf.astype(u.dtype), u], axis=1)
    y = sum(w[j] * full[:, j:j + T] for j in range(CONV_WIDTH))
    return y, full[:, T:]


def token_mixers(h, w_in, w_out, gate_b, mh_g, sinks, conv_w, state):
    B, T, _ = h.shape
    f32 = jnp.float32
    z = h @ w_in
    offs = np.cumsum(IN_SPLITS)[:-1].tolist()
    mq, mk, mv, mo, mi, mf, sq, sk, sv, cb, cc, ch = jnp.split(z, offs, axis=-1)
    if state is None:
        kbuf = vbuf = None
        cbuf = jnp.zeros((B, CONV_WIDTH - 1, CONV_W), h.dtype)
        C0 = jnp.zeros((B, MLSTM_HEADS, MLSTM_DH, MLSTM_DH), f32)
        n0 = jnp.zeros((B, MLSTM_HEADS, MLSTM_DH), f32)
        m0 = jnp.zeros((B, MLSTM_HEADS), f32)
        win_rows = min(WINDOW, PAST_LEN)
    else:
        kbuf, vbuf, cbuf, C0, n0, m0 = state
        win_rows = kbuf.shape[1]

    shp = (B, T, MLSTM_HEADS, MLSTM_DH)
    q = mq.reshape(shp).astype(f32)
    k = mk.reshape(shp).astype(f32) * (MLSTM_DH ** -0.5)
    v = mv.reshape(shp).astype(f32)
    pre = jnp.concatenate([mi, mf], axis=-1).astype(f32) + gate_b.astype(f32)
    ig = pre[..., :MLSTM_HEADS]
    lf = jax.nn.log_sigmoid(pre[..., MLSTM_HEADS:])
    hm, C1, n1, m1 = mlstm(q, k, v, ig, lf, C0, n0, m0)
    hm = hm * lax.rsqrt(jnp.mean(jnp.square(hm), axis=-1, keepdims=True) + EPS) * mh_g.astype(f32).reshape(MLSTM_HEADS, MLSTM_DH)
    hm = (hm * jax.nn.sigmoid(mo.reshape(shp).astype(f32))).astype(h.dtype).reshape(B, T, MLSTM_W)

    qa = sq.reshape(B, T, SWA_KV_HEADS, SWA_GROUP, SWA_DH)
    ka = sk.reshape(B, T, SWA_KV_HEADS, SWA_DH)
    va = sv.reshape(B, T, SWA_KV_HEADS, SWA_DH)
    if kbuf is None:
        oa = swa_banded(qa, ka, va, sinks)
        k_all, v_all = ka, va
    else:
        k_all = jnp.concatenate([kbuf.astype(ka.dtype), ka], axis=1)
        v_all = jnp.concatenate([vbuf.astype(va.dtype), va], axis=1)
        q_pos = PAST_LEN + jnp.arange(T, dtype=jnp.int32)
        k_pos = PAST_LEN - win_rows + jnp.arange(win_rows + T, dtype=jnp.int32)
        oa = swa_attend(qa, k_all, v_all, q_pos, k_pos, sinks)
    oa = oa.reshape(B, T, SWA_W)
    k_keep = k_all[:, -win_rows:]
    v_keep = v_all[:, -win_rows:]

    yconv, cbuf_new = short_conv(cc * ch, cbuf, conv_w)
    yc = cb * yconv

    out = jnp.concatenate([hm, oa, yc], axis=-1) @ w_out
    return out, (k_keep, v_keep, cbuf_new, C1, n1, m1)


PEER_SEL = PEER_HEADS * PEER_TOPK
PEER_TB = 128
PEER_NSLOT = 8


def _gelu_tanh(x):
    return 0.5 * x * (1.0 + jnp.tanh(math.sqrt(2.0 / math.pi) * (x + 0.044715 * (x * x * x))))


def _peer_experts_kernel(idx_ref, x_ref, gt_ref, tab_ref, o_ref, buf, sem):
    tb, d = x_ref.shape

    def row_copy(e, slot, r):
        return pltpu.make_async_copy(tab_ref.at[pl.ds(e, 1)], buf.at[slot, pl.ds(r, 1)], sem.at[slot])

    def issue(t, slot):
        for r in range(PEER_SEL):
            row_copy(idx_ref[r, t], slot, r).start()

    def wait(slot):
        pltpu.make_async_copy(tab_ref.at[pl.ds(0, PEER_SEL)], buf.at[slot], sem.at[slot]).wait()

    lane = lax.broadcasted_iota(jnp.int32, (PEER_SEL, tb), 1)

    def compute(t, slot):
        u = buf[slot, :, :d]
        v = buf[slot, :, d:]
        x = x_ref[pl.ds(t, 1), :]
        s = jnp.sum(u * x, axis=-1, keepdims=True)
        g = jnp.sum(jnp.where(lane == t, gt_ref[...], 0.0), axis=-1, keepdims=True)
        w = g * _gelu_tanh(s)
        o_ref[pl.ds(t, 1), :] = jnp.sum(v * w, axis=0, keepdims=True)

    ahead = PEER_NSLOT - 1
    for t0 in range(ahead):
        issue(t0, t0)

    def main_body(t, carry):
        issue(t + ahead, (t + ahead) & (PEER_NSLOT - 1))
        slot = t & (PEER_NSLOT - 1)
        wait(slot)
        compute(t, slot)
        return carry

    def tail_body(t, carry):
        slot = t & (PEER_NSLOT - 1)
        wait(slot)
        compute(t, slot)
        return carry

    lax.fori_loop(0, tb - ahead, main_body, 0)
    lax.fori_loop(tb - ahead, tb, tail_body, 0)


def peer_experts(h, idx, g, tab):
    n, d = h.shape
    nblk = n // PEER_TB
    idx_t = idx.reshape(nblk, PEER_TB, PEER_SEL).transpose(0, 2, 1)
    g_t = g.reshape(nblk, PEER_TB, PEER_SEL).transpose(0, 2, 1)
    return pl.pallas_call(
        _peer_experts_kernel,
        grid=(nblk,),
        in_specs=[
            pl.BlockSpec((None, PEER_SEL, PEER_TB), lambda i: (i, 0, 0), memory_space=pltpu.SMEM),
            pl.BlockSpec((PEER_TB, d), lambda i: (i, 0)),
            pl.BlockSpec((None, PEER_SEL, PEER_TB), lambda i: (i, 0, 0)),
            pl.BlockSpec(memory_space=pl.ANY),
        ],
        out_specs=pl.BlockSpec((PEER_TB, d), lambda i: (i, 0)),
        out_shape=jax.ShapeDtypeStruct((n, d), jnp.float32),
        scratch_shapes=[pltpu.VMEM((PEER_NSLOT, PEER_SEL, 2 * d), jnp.float32),
                        pltpu.SemaphoreType.DMA((PEER_NSLOT,))],
        compiler_params=pltpu.CompilerParams(dimension_semantics=("arbitrary",)),
        name="peer_experts",
    )(idx_t, h, g_t, tab)


def peer_ffn(h, wq, subkeys, tab):
    B, T, D = h.shape
    n = B * T
    x = h.reshape(n, D)
    q = (x @ wq).reshape(n, PEER_HEADS, 2, PEER_DKEY // 2)
    s = jnp.einsum('thpd,pkd->thpk', q, subkeys).astype(jnp.float32)
    s1, i1 = lax.top_k(s[:, :, 0], PEER_TOPK)
    s2, i2 = lax.top_k(s[:, :, 1], PEER_TOPK)
    cand = (s1[..., :, None] + s2[..., None, :]).reshape(n, PEER_HEADS, PEER_TOPK * PEER_TOPK)
    cidx = (i1[..., :, None] * PEER_NKEYS + i2[..., None, :]).reshape(n, PEER_HEADS, PEER_TOPK * PEER_TOPK)
    top_s, sel = lax.top_k(cand, PEER_TOPK)
    idx = jnp.take_along_axis(cidx, sel, axis=-1)
    g = jax.nn.softmax(top_s, axis=-1)
    y = peer_experts(x, idx.reshape(n, PEER_SEL).astype(jnp.int32), g.reshape(n, PEER_SEL), tab)
    return y.reshape(B, T, D)


def trunk_layer(x, c, ada_w, ada_b, g_mix, g_ffn, w_in, w_out, gate_b, mh_g, sinks, conv_w,
                wq, subkeys, tab, state):
    mod = (jax.nn.silu(c) @ ada_w + ada_b)[:, None, :]
    sh1, sc1, gt1, sh2, sc2, gt2 = jnp.split(mod, 6, axis=-1)
    h = rmsnorm(x, g_mix) * (1 + sc1) + sh1
    mix, new_state = token_mixers(h, w_in, w_out, gate_b, mh_g, sinks, conv_w, state)
    x = x + gt1 * mix
    h = rmsnorm(x, g_ffn) * (1 + sc2) + sh2
    x = x + gt2 * peer_ffn(h, wq, subkeys, tab)
    return x, new_state


def _final_norm_kernel(x_ref, g_ref, o_ref):
    x = x_ref[...]
    y = x * lax.rsqrt(jnp.mean(jnp.square(x), axis=-1, keepdims=True) + EPS)
    o_ref[...] = y * g_ref[...]


def final_norm(x, g):
    shp = x.shape
    x2 = x.reshape(-1, shp[-1])
    n, d = x2.shape
    tm = 512
    out = pl.pallas_call(
        _final_norm_kernel,
        grid=(n // tm,),
        in_specs=[pl.BlockSpec((tm, d), lambda i: (i, 0)), pl.BlockSpec((1, d), lambda i: (0, 0))],
        out_specs=pl.BlockSpec((tm, d), lambda i: (i, 0)),
        out_shape=jax.ShapeDtypeStruct((n, d), x.dtype),
    )(x2, g.reshape(1, d))
    return out.reshape(shp)


def kernel(x_prompt, x_sample, cache_swa_k, cache_swa_v, state_conv, state_mlstm_C, state_mlstm_n, state_mlstm_m, c_prompt, c_sample, ada_w, ada_b, norm_mix_g, norm_ffn_g, w_in, w_out, mlstm_gate_b, mlstm_norm_g, swa_sinks, conv_w, peer_wq, peer_subkeys, peer_u, peer_v, final_g):
    xp, xs = x_prompt, x_sample
    new_p, new_s = [], []
    for l in range(DEPTH):
        w = (ada_w[l], ada_b[l], norm_mix_g[l], norm_ffn_g[l], w_in[l], w_out[l], mlstm_gate_b[l],
             mlstm_norm_g[l], swa_sinks[l], conv_w[l], peer_wq[l], peer_subkeys[l],
             jnp.concatenate([peer_u[l], peer_v[l]], axis=-1))
        xp, sp = trunk_layer(xp, c_prompt, *w, None)
        st = (cache_swa_k[l], cache_swa_v[l], state_conv[l], state_mlstm_C[l], state_mlstm_n[l], state_mlstm_m[l])
        xs, ss = trunk_layer(xs, c_sample, *w, st)
        new_p.append(sp)
        new_s.append(ss)
    y_prompt = final_norm(xp, final_g)
    y_sample = final_norm(xs, final_g)
    pk, pv, pc, pC, pn, pm = [jnp.stack(t) for t in zip(*new_p)]
    sk, sv, sc, sC, sn, sm = [jnp.stack(t) for t in zip(*new_s)]
    return (y_prompt, y_sample, pk, pv, pc, pC, pn, pm, sk, sv, sc, sC, sn, sm)
```

```python
import math
import numpy as np
import jax
import jax.numpy as jnp
from jax import lax
from jax.experimental import pallas as pl
from jax.experimental.pallas import tpu as pltpu

D_MODEL = 1024
BATCH = 8
SEQ = 4096
DEPTH = 2
DEC_BATCH = 128
DEC_SEQ = 4
PAST_LEN = 16384

MLSTM_W = D_MODEL // 2
MLSTM_HEADS = 4
MLSTM_DH = MLSTM_W // MLSTM_HEADS
MLSTM_CHUNK = 64
SWA_W = D_MODEL // 4
SWA_DH = 64
SWA_HEADS = SWA_W // SWA_DH
SWA_KV_HEADS = SWA_HEADS // 2
SWA_GROUP = SWA_HEADS // SWA_KV_HEADS
WINDOW = 128
CONV_W = D_MODEL - MLSTM_W - SWA_W
CONV_WIDTH = 3
MIX_W = MLSTM_W + SWA_W + CONV_W
IN_SPLITS = (MLSTM_W, MLSTM_W, MLSTM_W, MLSTM_W, MLSTM_HEADS, MLSTM_HEADS,
             SWA_W, SWA_KV_HEADS * SWA_DH, SWA_KV_HEADS * SWA_DH,
             CONV_W, CONV_W, CONV_W)
IN_W = sum(IN_SPLITS)
PEER_HEADS = 8
PEER_NKEYS = 128
PEER_EXPERTS = PEER_NKEYS * PEER_NKEYS
PEER_DKEY = 256
PEER_TOPK = 16
PEER_BLOCK = 256
EPS = 1e-6


def rmsnorm(x, g):
    xf = x.astype(jnp.float32)
    y = xf * lax.rsqrt(jnp.mean(jnp.square(xf), axis=-1, keepdims=True) + EPS)
    return (y * g.astype(jnp.float32)).astype(x.dtype)


def alibi_slopes():
    h = jnp.arange(1, SWA_HEADS + 1, dtype=jnp.float32)
    return jnp.exp2(-8.0 * h / SWA_HEADS).reshape(SWA_KV_HEADS, SWA_GROUP)


def _mlstm_chunk(carry, xs):
    C, n, m = carry
    q, k, v, ig, lf = xs
    L = q.shape[2]
    b = jnp.cumsum(lf, axis=-1)
    a = b + m[..., None]
    d = b[..., :, None] - b[..., None, :] + ig[..., None, :]
    causal = jnp.tril(jnp.ones((L, L), dtype=bool))
    d = jnp.where(causal, d, -jnp.inf)
    m_t = jnp.maximum(a, jnp.max(d, axis=-1))
    w_inter = jnp.exp(a - m_t)
    w_intra = jnp.exp(d - m_t[..., None])
    qk = jnp.einsum('bhtk,bhsk->bhts', q, k) * w_intra
    num = w_inter[..., None] * jnp.einsum('bhvk,bhtk->bhtv', C, q) + jnp.einsum('bhts,bhsv->bhtv', qk, v)
    den = w_inter * jnp.einsum('bhk,bhtk->bht', n, q) + jnp.sum(qk, axis=-1)
    h = num / jnp.maximum(jnp.abs(den), jnp.exp(-m_t))[..., None]
    wl_inter = w_inter[..., -1]
    wl_intra = w_intra[..., -1, :]
    C_new = wl_inter[..., None, None] * C + jnp.einsum('bhs,bhsv,bhsk->bhvk', wl_intra, v, k)
    n_new = wl_inter[..., None] * n + jnp.einsum('bhs,bhsk->bhk', wl_intra, k)
    return (C_new, n_new, m_t[..., -1]), h


def mlstm(q, k, v, ig, lf, C0, n0, m0):
    B, T = q.shape[:2]
    L = math.gcd(T, MLSTM_CHUNK)
    nc = T // L

    def to_chunks(t):
        t = jnp.swapaxes(t, 1, 2)
        t = t.reshape(t.shape[:2] + (nc, L) + t.shape[3:])
        return jnp.moveaxis(t, 2, 0)

    carry0 = (C0.astype(jnp.float32), n0.astype(jnp.float32), m0.astype(jnp.float32))
    xs = (to_chunks(q), to_chunks(k), to_chunks(v), to_chunks(ig), to_chunks(lf))
    (C1, n1, m1), h = lax.scan(_mlstm_chunk, carry0, xs)
    h = jnp.moveaxis(h, 0, 2).reshape(B, MLSTM_HEADS, T, MLSTM_DH)
    return jnp.swapaxes(h, 1, 2), C1, n1, m1


def swa_attend(q, k, v, q_pos, k_pos, sinks):
    s = jnp.einsum('...qhgd,...khd->...hgqk', q, k).astype(jnp.float32) * (SWA_DH ** -0.5)
    dist = q_pos[..., :, None] - k_pos[..., None, :]
    visible = (dist >= 0) & (dist <= WINDOW) & (k_pos[..., None, :] >= 0)
    s = s - alibi_slopes()[:, :, None, None] * jnp.expand_dims(dist, (-4, -3)).astype(jnp.float32)
    s = jnp.where(jnp.expand_dims(visible, (-4, -3)), s, -jnp.inf)
    sink = jnp.broadcast_to(sinks.astype(jnp.float32).reshape(SWA_KV_HEADS, SWA_GROUP, 1, 1), s.shape[:-1] + (1,))
    p = jax.nn.softmax(jnp.concatenate([s, sink], axis=-1), axis=-1)[..., :-1]
    return jnp.einsum('...hgqk,...khd->...qhgd', p.astype(v.dtype), v)


def swa_banded(q, k, v, sinks):
    B, S = q.shape[:2]
    nb = S // WINDOW
    qb = q.reshape(B, nb, WINDOW, SWA_KV_HEADS, SWA_GROUP, SWA_DH)
    kb = k.reshape(B, nb, WINDOW, SWA_KV_HEADS, SWA_DH)
    vb = v.reshape(B, nb, WINDOW, SWA_KV_HEADS, SWA_DH)
    pad = ((0, 0), (1, 0), (0, 0), (0, 0), (0, 0))
    kband = jnp.concatenate([jnp.pad(kb[:, :-1], pad), kb], axis=2)
    vband = jnp.concatenate([jnp.pad(vb[:, :-1], pad), vb], axis=2)
    pos = jnp.arange(S, dtype=jnp.int32).reshape(nb, WINDOW)
    kpos = jnp.concatenate([pos - WINDOW, pos], axis=1)
    o = swa_attend(qb, kband, vband, pos, kpos, sinks)
    return o.reshape(B, S, SWA_KV_HEADS, SWA_GROUP, SWA_DH)


def short_conv(u, buf, w):
    T = u.shape[1]
    full = jnp.concatenate([buf.astype(u.dtype), u], axis=1)
    y = sum(w[j] * full[:, j:j + T] for j in range(CONV_WIDTH))
    return y, full[:, T:]


def token_mixers(h, w_in, w_out, gate_b, mh_g, sinks, conv_w, state):
    B, T, _ = h.shape
    f32 = jnp.float32
    z = h @ w_in
    offs = np.cumsum(IN_SPLITS)[:-1].tolist()
    mq, mk, mv, mo, mi, mf, sq, sk, sv, cb, cc, ch = jnp.split(z, offs, axis=-1)
    if state is None:
        kbuf = vbuf = None
        cbuf = jnp.zeros((B, CONV_WIDTH - 1, CONV_W), h.dtype)
        C0 = jnp.zeros((B, MLSTM_HEADS, MLSTM_DH, MLSTM_DH), f32)
        n0 = jnp.zeros((B, MLSTM_HEADS, MLSTM_DH), f32)
        m0 = jnp.zeros((B, MLSTM_HEADS), f32)
        win_rows = min(WINDOW, PAST_LEN)
    else:
        kbuf, vbuf, cbuf, C0, n0, m0 = state
        win_rows = kbuf.shape[1]

    shp = (B, T, MLSTM_HEADS, MLSTM_DH)
    q = mq.reshape(shp).astype(f32)
    k = mk.reshape(shp).astype(f32) * (MLSTM_DH ** -0.5)
    v = mv.reshape(shp).astype(f32)
    pre = jnp.concatenate([mi, mf], axis=-1).astype(f32) + gate_b.astype(f32)
    ig = pre[..., :MLSTM_HEADS]
    lf = jax.nn.log_sigmoid(pre[..., MLSTM_HEADS:])
    hm, C1, n1, m1 = mlstm(q, k, v, ig, lf, C0, n0, m0)
    hm = hm * lax.rsqrt(jnp.mean(jnp.square(hm), axis=-1, keepdims=True) + EPS) * mh_g.astype(f32).reshape(MLSTM_HEADS, MLSTM_DH)
    hm = (hm * jax.nn.sigmoid(mo.reshape(shp).astype(f32))).astype(h.dtype).reshape(B, T, MLSTM_W)

    qa = sq.reshape(B, T, SWA_KV_HEADS, SWA_GROUP, SWA_DH)
    ka = sk.reshape(B, T, SWA_KV_HEADS, SWA_DH)
    va = sv.reshape(B, T, SWA_KV_HEADS, SWA_DH)
    if kbuf is None:
        oa = swa_banded(qa, ka, va, sinks)
        k_all, v_all = ka, va
    else:
        k_all = jnp.concatenate([kbuf.astype(ka.dtype), ka], axis=1)
        v_all = jnp.concatenate([vbuf.astype(va.dtype), va], axis=1)
        q_pos = PAST_LEN + jnp.arange(T, dtype=jnp.int32)
        k_pos = PAST_LEN - win_rows + jnp.arange(win_rows + T, dtype=jnp.int32)
        oa = swa_attend(qa, k_all, v_all, q_pos, k_pos, sinks)
    oa = oa.reshape(B, T, SWA_W)
    k_keep = k_all[:, -win_rows:]
    v_keep = v_all[:, -win_rows:]

    yconv, cbuf_new = short_conv(cc * ch, cbuf, conv_w)
    yc = cb * yconv

    out = jnp.concatenate([hm, oa, yc], axis=-1) @ w_out
    return out, (k_keep, v_keep, cbuf_new, C1, n1, m1)


PEER_SEL = PEER_HEADS * PEER_TOPK
PEER_TB = 128
PEER_NSLOT = 8


def _gelu_tanh(x):
    return 0.5 * x * (1.0 + jnp.tanh(math.sqrt(2.0 / math.pi) * (x + 0.044715 * (x * x * x))))


def _peer_experts_kernel(idx_ref, h_ref, gt_ref, xres_ref, gate_ref, tab_ref, o_ref, buf, sem):
    tb, d = h_ref.shape
    per_token_gate = gate_ref.shape[0] == tb

    def row_copy(e, slot, r):
        return pltpu.make_async_copy(tab_ref.at[pl.ds(e, 1)], buf.at[slot, pl.ds(r, 1)], sem.at[slot])

    def issue(t, slot):
        for r in range(PEER_SEL):
            row_copy(idx_ref[r, t], slot, r).start()

    def wait(slot):
        pltpu.make_async_copy(tab_ref.at[pl.ds(0, PEER_SEL)], buf.at[slot], sem.at[slot]).wait()

    lane = lax.broadcasted_iota(jnp.int32, (PEER_SEL, tb), 1)

    def compute(t, slot):
        u = buf[slot, :, :d]
        v = buf[slot, :, d:]
        s = jnp.sum(u * h_ref[pl.ds(t, 1), :], axis=-1, keepdims=True)
        g = jnp.sum(jnp.where(lane == t, gt_ref[...], 0.0), axis=-1, keepdims=True)
        y = jnp.sum(v * (g * _gelu_tanh(s)), axis=0, keepdims=True)
        gate = gate_ref[pl.ds(t, 1), :] if per_token_gate else gate_ref[...]
        o_ref[pl.ds(t, 1), :] = xres_ref[pl.ds(t, 1), :] + gate * y

    ahead = PEER_NSLOT - 1
    for t0 in range(ahead):
        issue(t0, t0)

    def main_body(t, carry):
        issue(t + ahead, (t + ahead) & (PEER_NSLOT - 1))
        slot = t & (PEER_NSLOT - 1)
        wait(slot)
        compute(t, slot)
        return carry

    def tail_body(t, carry):
        slot = t & (PEER_NSLOT - 1)
        wait(slot)
        compute(t, slot)
        return carry

    lax.fori_loop(0, tb - ahead, main_body, 0)
    lax.fori_loop(tb - ahead, tb, tail_body, 0)


def peer_experts(h, idx_t, g_t, tab, xres, gate):
    n, d = h.shape
    tb = PEER_TB
    rows_per_gate = n // gate.shape[0]
    if rows_per_gate == 1:
        gate_spec = pl.BlockSpec((tb, d), lambda i: (i, 0))
    else:
        gate_spec = pl.BlockSpec((None, 1, d), lambda i: (i * tb // rows_per_gate, 0, 0))
        gate = gate[:, None, :]
    return pl.pallas_call(
        _peer_experts_kernel,
        grid=(n // tb,),
        in_specs=[
            pl.BlockSpec((None, PEER_SEL, tb), lambda i: (i, 0, 0), memory_space=pltpu.SMEM),
            pl.BlockSpec((tb, d), lambda i: (i, 0)),
            pl.BlockSpec((None, PEER_SEL, tb), lambda i: (i, 0, 0)),
            pl.BlockSpec((tb, d), lambda i: (i, 0)),
            gate_spec,
            pl.BlockSpec(memory_space=pl.ANY),
        ],
        out_specs=pl.BlockSpec((tb, d), lambda i: (i, 0)),
        out_shape=jax.ShapeDtypeStruct((n, d), jnp.float32),
        scratch_shapes=[pltpu.VMEM((PEER_NSLOT, PEER_SEL, 2 * d), jnp.float32),
                        pltpu.SemaphoreType.DMA((PEER_NSLOT,))],
        compiler_params=pltpu.CompilerParams(dimension_semantics=("arbitrary",)),
        name="peer_experts",
    )(idx_t, h, g_t, xres, gate, tab)


LANES = 128
ROUTE_TB = 256
HALF_KEY = PEER_DKEY // 2


def _top_rows(s, k, payload=None):
    rows = lax.broadcasted_iota(jnp.int32, s.shape, 0)
    big = jnp.int32(s.shape[0])
    vals, ids = [], []
    for _ in range(k):
        m = jnp.max(s, axis=0, keepdims=True)
        r = jnp.min(jnp.where(s == m, rows, big), axis=0, keepdims=True)
        hit = rows == r
        vals.append(m)
        ids.append(r if payload is None else jnp.max(jnp.where(hit, payload, -1), axis=0, keepdims=True))
        s = jnp.where(hit, -jnp.inf, s)
    return jnp.concatenate(vals, axis=0), jnp.concatenate(ids, axis=0)


def _peer_route_kernel(x_ref, gain_ref, sc_ref, sh_ref, wq_ref, sk_ref, h_ref, idx_ref, gt_ref, q_scr):
    tb = x_ref.shape[0]
    x = x_ref[...]
    y = x * lax.rsqrt(jnp.mean(x * x, axis=-1, keepdims=True) + EPS) * gain_ref[...]
    h = y * (1.0 + sc_ref[...]) + sh_ref[...]
    h_ref[...] = h
    q = jnp.dot(h.astype(jnp.bfloat16), wq_ref[...], preferred_element_type=jnp.float32)
    for sub in range(tb // LANES):
        for head in range(PEER_HEADS):
            q_scr[sub * PEER_HEADS + head] = q[sub * LANES:(sub + 1) * LANES, head * PEER_DKEY:(head + 1) * PEER_DKEY]

    def per_head(j, carry):
        sub = j // PEER_HEADS
        head = j % PEER_HEADS
        qh = q_scr[j].astype(jnp.bfloat16)
        tops = []
        for p in range(2):
            st = lax.dot_general(sk_ref[p], qh[:, p * HALF_KEY:(p + 1) * HALF_KEY],
                                 (((1,), (1,)), ((), ())), preferred_element_type=jnp.float32)
            tops.append(_top_rows(st, PEER_TOPK))
        (s1, i1), (s2, i2) = tops
        cand = jnp.concatenate([s1[a:a + 1] + s2 for a in range(PEER_TOPK)], axis=0)
        cidx = jnp.concatenate([i1[a:a + 1] * PEER_NKEYS + i2 for a in range(PEER_TOPK)], axis=0)
        top_s, top_i = _top_rows(cand, PEER_TOPK, payload=cidx)
        e = jnp.exp(top_s - jnp.max(top_s, axis=0, keepdims=True))
        off = pl.multiple_of(head * PEER_TOPK, PEER_TOPK)
        idx_ref[sub, pl.ds(off, PEER_TOPK), :] = top_i
        gt_ref[sub, pl.ds(off, PEER_TOPK), :] = e / jnp.sum(e, axis=0, keepdims=True)
        return carry

    lax.fori_loop(0, (tb // LANES) * PEER_HEADS, per_head, 0)


def peer_route(x, gain, sc, sh, wq_bf, sk_bf):
    n, d = x.shape
    tb = ROUTE_TB
    nsub = n // LANES
    per_token = sc.shape[0] == n
    rows_per_mod = n // sc.shape[0]
    if per_token:
        mod_spec = pl.BlockSpec((tb, d), lambda i: (i, 0))
    else:
        mod_spec = pl.BlockSpec((None, 1, d), lambda i: (i * tb // rows_per_mod, 0, 0))
        sc, sh = sc[:, None, :], sh[:, None, :]
    return pl.pallas_call(
        _peer_route_kernel,
        grid=(n // tb,),
        in_specs=[
            pl.BlockSpec((tb, d), lambda i: (i, 0)),
            pl.BlockSpec((1, d), lambda i: (0, 0)),
            mod_spec, mod_spec,
            pl.BlockSpec(wq_bf.shape, lambda i: (0, 0)),
            pl.BlockSpec(sk_bf.shape, lambda i: (0, 0, 0)),
        ],
        out_specs=[
            pl.BlockSpec((tb, d), lambda i: (i, 0)),
            pl.BlockSpec((tb // LANES, PEER_SEL, LANES), lambda i: (i, 0, 0)),
            pl.BlockSpec((tb // LANES, PEER_SEL, LANES), lambda i: (i, 0, 0)),
        ],
        out_shape=[jax.ShapeDtypeStruct((n, d), jnp.float32),
                   jax.ShapeDtypeStruct((nsub, PEER_SEL, LANES), jnp.int32),
                   jax.ShapeDtypeStruct((nsub, PEER_SEL, LANES), jnp.float32)],
        scratch_shapes=[pltpu.VMEM((tb // LANES * PEER_HEADS, LANES, PEER_DKEY), jnp.float32)],
        compiler_params=pltpu.CompilerParams(dimension_semantics=("arbitrary",)),
        name="peer_route",
    )(x, gain, sc, sh, wq_bf, sk_bf)


def peer_layer(x, gain, sc, sh, gt, wq_bf, sk_bf, tab):
    h, idx_t, g_t = peer_route(x, gain, sc, sh, wq_bf, sk_bf)
    return peer_experts(h, idx_t, g_t, tab, x, gt)


def trunk_layer(x, c, ada_w, ada_b, g_mix, g_ffn, w_in, w_out, gate_b, mh_g, sinks, conv_w,
                wq, subkeys, tab, state):
    mod = (jax.nn.silu(c) @ ada_w + ada_b)[:, None, :]
    sh1, sc1, gt1, sh2, sc2, gt2 = jnp.split(mod, 6, axis=-1)
    h = rmsnorm(x, g_mix) * (1 + sc1) + sh1
    mix, new_state = token_mixers(h, w_in, w_out, gate_b, mh_g, sinks, conv_w, state)
    x = x + gt1 * mix
    B, T, D = x.shape
    if T % ROUTE_TB:
        sc2, sh2, gt2 = (jnp.broadcast_to(m, (B, T, D)).reshape(B * T, D) for m in (sc2, sh2, gt2))
    else:
        sc2, sh2, gt2 = (m.reshape(B, D) for m in (sc2, sh2, gt2))
    x = peer_layer(x.reshape(B * T, D), g_ffn.reshape(1, D), sc2, sh2, gt2,
                   wq.astype(jnp.bfloat16), subkeys.astype(jnp.bfloat16), tab).reshape(B, T, D)
    return x, new_state


def _final_norm_kernel(x_ref, g_ref, o_ref):
    x = x_ref[...]
    y = x * lax.rsqrt(jnp.mean(jnp.square(x), axis=-1, keepdims=True) + EPS)
    o_ref[...] = y * g_ref[...]


def final_norm(x, g):
    shp = x.shape
    x2 = x.reshape(-1, shp[-1])
    n, d = x2.shape
    tm = 512
    out = pl.pallas_call(
        _final_norm_kernel,
        grid=(n // tm,),
        in_specs=[pl.BlockSpec((tm, d), lambda i: (i, 0)), pl.BlockSpec((1, d), lambda i: (0, 0))],
        out_specs=pl.BlockSpec((tm, d), lambda i: (i, 0)),
        out_shape=jax.ShapeDtypeStruct((n, d), x.dtype),
    )(x2, g.reshape(1, d))
    return out.reshape(shp)


def kernel(x_prompt, x_sample, cache_swa_k, cache_swa_v, state_conv, state_mlstm_C, state_mlstm_n, state_mlstm_m, c_prompt, c_sample, ada_w, ada_b, norm_mix_g, norm_ffn_g, w_in, w_out, mlstm_gate_b, mlstm_norm_g, swa_sinks, conv_w, peer_wq, peer_subkeys, peer_u, peer_v, final_g):
    xp, xs = x_prompt, x_sample
    new_p, new_s = [], []
    for l in range(DEPTH):
        w = (ada_w[l], ada_b[l], norm_mix_g[l], norm_ffn_g[l], w_in[l], w_out[l], mlstm_gate_b[l],
             mlstm_norm_g[l], swa_sinks[l], conv_w[l], peer_wq[l], peer_subkeys[l],
             jnp.concatenate([peer_u[l], peer_v[l]], axis=-1))
        xp, sp = trunk_layer(xp, c_prompt, *w, None)
        st = (cache_swa_k[l], cache_swa_v[l], state_conv[l], state_mlstm_C[l], state_mlstm_n[l], state_mlstm_m[l])
        xs, ss = trunk_layer(xs, c_sample, *w, st)
        new_p.append(sp)
        new_s.append(ss)
    y_prompt = final_norm(xp, final_g)
    y_sample = final_norm(xs, final_g)
    pk, pv, pc, pC, pn, pm = [jnp.stack(t) for t in zip(*new_p)]
    sk, sv, sc, sC, sn, sm = [jnp.stack(t) for t in zip(*new_s)]
    return (y_prompt, y_sample, pk, pv, pc, pC, pn, pm, sk, sv, sc, sC, sn, sm)
```

```python
import math
import numpy as np
import jax
import jax.numpy as jnp
from jax import lax
from jax.experimental import pallas as pl
from jax.experimental.pallas import tpu as pltpu

D_MODEL = 1024
BATCH = 8
SEQ = 4096
DEPTH = 2
DEC_BATCH = 128
DEC_SEQ = 4
PAST_LEN = 16384

MLSTM_W = D_MODEL // 2
MLSTM_HEADS = 4
MLSTM_DH = MLSTM_W // MLSTM_HEADS
MLSTM_CHUNK = 64
SWA_W = D_MODEL // 4
SWA_DH = 64
SWA_HEADS = SWA_W // SWA_DH
SWA_KV_HEADS = SWA_HEADS // 2
SWA_GROUP = SWA_HEADS // SWA_KV_HEADS
WINDOW = 128
CONV_W = D_MODEL - MLSTM_W - SWA_W
CONV_WIDTH = 3
MIX_W = MLSTM_W + SWA_W + CONV_W
IN_SPLITS = (MLSTM_W, MLSTM_W, MLSTM_W, MLSTM_W, MLSTM_HEADS, MLSTM_HEADS,
             SWA_W, SWA_KV_HEADS * SWA_DH, SWA_KV_HEADS * SWA_DH,
             CONV_W, CONV_W, CONV_W)
IN_W = sum(IN_SPLITS)
PEER_HEADS = 8
PEER_NKEYS = 128
PEER_EXPERTS = PEER_NKEYS * PEER_NKEYS
PEER_DKEY = 256
PEER_TOPK = 16
PEER_BLOCK = 256
EPS = 1e-6


def rmsnorm(x, g):
    xf = x.astype(jnp.float32)
    y = xf * lax.rsqrt(jnp.mean(jnp.square(xf), axis=-1, keepdims=True) + EPS)
    return (y * g.astype(jnp.float32)).astype(x.dtype)


def alibi_slopes():
    h = jnp.arange(1, SWA_HEADS + 1, dtype=jnp.float32)
    return jnp.exp2(-8.0 * h / SWA_HEADS).reshape(SWA_KV_HEADS, SWA_GROUP)


def _mlstm_chunk(carry, xs):
    C, n, m = carry
    q, k, v, ig, lf = xs
    L = q.shape[2]
    b = jnp.cumsum(lf, axis=-1)
    a = b + m[..., None]
    d = b[..., :, None] - b[..., None, :] + ig[..., None, :]
    causal = jnp.tril(jnp.ones((L, L), dtype=bool))
    d = jnp.where(causal, d, -jnp.inf)
    m_t = jnp.maximum(a, jnp.max(d, axis=-1))
    w_inter = jnp.exp(a - m_t)
    w_intra = jnp.exp(d - m_t[..., None])
    qk = jnp.einsum('bhtk,bhsk->bhts', q, k) * w_intra
    num = w_inter[..., None] * jnp.einsum('bhvk,bhtk->bhtv', C, q) + jnp.einsum('bhts,bhsv->bhtv', qk, v)
    den = w_inter * jnp.einsum('bhk,bhtk->bht', n, q) + jnp.sum(qk, axis=-1)
    h = num / jnp.maximum(jnp.abs(den), jnp.exp(-m_t))[..., None]
    wl_inter = w_inter[..., -1]
    wl_intra = w_intra[..., -1, :]
    C_new = wl_inter[..., None, None] * C + jnp.einsum('bhs,bhsv,bhsk->bhvk', wl_intra, v, k)
    n_new = wl_inter[..., None] * n + jnp.einsum('bhs,bhsk->bhk', wl_intra, k)
    return (C_new, n_new, m_t[..., -1]), h


def mlstm(q, k, v, ig, lf, C0, n0, m0):
    B, T = q.shape[:2]
    L = math.gcd(T, MLSTM_CHUNK)
    nc = T // L

    def to_chunks(t):
        t = jnp.swapaxes(t, 1, 2)
        t = t.reshape(t.shape[:2] + (nc, L) + t.shape[3:])
        return jnp.moveaxis(t, 2, 0)

    carry0 = (C0.astype(jnp.float32), n0.astype(jnp.float32), m0.astype(jnp.float32))
    xs = (to_chunks(q), to_chunks(k), to_chunks(v), to_chunks(ig), to_chunks(lf))
    (C1, n1, m1), h = lax.scan(_mlstm_chunk, carry0, xs)
    h = jnp.moveaxis(h, 0, 2).reshape(B, MLSTM_HEADS, T, MLSTM_DH)
    return jnp.swapaxes(h, 1, 2), C1, n1, m1


def swa_attend(q, k, v, q_pos, k_pos, sinks):
    s = jnp.einsum('...qhgd,...khd->...hgqk', q, k).astype(jnp.float32) * (SWA_DH ** -0.5)
    dist = q_pos[..., :, None] - k_pos[..., None, :]
    visible = (dist >= 0) & (dist <= WINDOW) & (k_pos[..., None, :] >= 0)
    s = s - alibi_slopes()[:, :, None, None] * jnp.expand_dims(dist, (-4, -3)).astype(jnp.float32)
    s = jnp.where(jnp.expand_dims(visible, (-4, -3)), s, -jnp.inf)
    sink = jnp.broadcast_to(sinks.astype(jnp.float32).reshape(SWA_KV_HEADS, SWA_GROUP, 1, 1), s.shape[:-1] + (1,))
    p = jax.nn.softmax(jnp.concatenate([s, sink], axis=-1), axis=-1)[..., :-1]
    return jnp.einsum('...hgqk,...khd->...qhgd', p.astype(v.dtype), v)


def swa_banded(q, k, v, sinks):
    B, S = q.shape[:2]
    nb = S // WINDOW
    qb = q.reshape(B, nb, WINDOW, SWA_KV_HEADS, SWA_GROUP, SWA_DH)
    kb = k.reshape(B, nb, WINDOW, SWA_KV_HEADS, SWA_DH)
    vb = v.reshape(B, nb, WINDOW, SWA_KV_HEADS, SWA_DH)
    pad = ((0, 0), (1, 0), (0, 0), (0, 0), (0, 0))
    kband = jnp.concatenate([jnp.pad(kb[:, :-1], pad), kb], axis=2)
    vband = jnp.concatenate([jnp.pad(vb[:, :-1], pad), vb], axis=2)
    pos = jnp.arange(S, dtype=jnp.int32).reshape(nb, WINDOW)
    kpos = jnp.concatenate([pos - WINDOW, pos], axis=1)
    o = swa_attend(qb, kband, vband, pos, kpos, sinks)
    return o.reshape(B, S, SWA_KV_HEADS, SWA_GROUP, SWA_DH)


def short_conv(u, buf, w):
    T = u.shape[1]
    full = jnp.concatenate([buf.astype(u.dtype), u], axis=1)
    y = sum(w[j] * full[:, j:j + T] for j in range(CONV_WIDTH))
    return y, full[:, T:]


def token_mixers(h, w_in, w_out, gate_b, mh_g, sinks, conv_w, state):
    B, T, _ = h.shape
    f32 = jnp.float32
    z = h @ w_in
    offs = np.cumsum(IN_SPLITS)[:-1].tolist()
    mq, mk, mv, mo, mi, mf, sq, sk, sv, cb, cc, ch = jnp.split(z, offs, axis=-1)
    if state is None:
        kbuf = vbuf = None
        cbuf = jnp.zeros((B, CONV_WIDTH - 1, CONV_W), h.dtype)
        C0 = jnp.zeros((B, MLSTM_HEADS, MLSTM_DH, MLSTM_DH), f32)
        n0 = jnp.zeros((B, MLSTM_HEADS, MLSTM_DH), f32)
        m0 = jnp.zeros((B, MLSTM_HEADS), f32)
        win_rows = min(WINDOW, PAST_LEN)
    else:
        kbuf, vbuf, cbuf, C0, n0, m0 = state
        win_rows = kbuf.shape[1]

    shp = (B, T, MLSTM_HEADS, MLSTM_DH)
    q = mq.reshape(shp).astype(f32)
    k = mk.reshape(shp).astype(f32) * (MLSTM_DH ** -0.5)
    v = mv.reshape(shp).astype(f32)
    pre = jnp.concatenate([mi, mf], axis=-1).astype(f32) + gate_b.astype(f32)
    ig = pre[..., :MLSTM_HEADS]
    lf = jax.nn.log_sigmoid(pre[..., MLSTM_HEADS:])
    hm, C1, n1, m1 = mlstm(q, k, v, ig, lf, C0, n0, m0)
    hm = hm * lax.rsqrt(jnp.mean(jnp.square(hm), axis=-1, keepdims=True) + EPS) * mh_g.astype(f32).reshape(MLSTM_HEADS, MLSTM_DH)
    hm = (hm * jax.nn.sigmoid(mo.reshape(shp).astype(f32))).astype(h.dtype).reshape(B, T, MLSTM_W)

    qa = sq.reshape(B, T, SWA_KV_HEADS, SWA_GROUP, SWA_DH)
    ka = sk.reshape(B, T, SWA_KV_HEADS, SWA_DH)
    va = sv.reshape(B, T, SWA_KV_HEADS, SWA_DH)
    if kbuf is None:
        oa = swa_banded(qa, ka, va, sinks)
        k_all, v_all = ka, va
    else:
        k_all = jnp.concatenate([kbuf.astype(ka.dtype), ka], axis=1)
        v_all = jnp.concatenate([vbuf.astype(va.dtype), va], axis=1)
        q_pos = PAST_LEN + jnp.arange(T, dtype=jnp.int32)
        k_pos = PAST_LEN - win_rows + jnp.arange(win_rows + T, dtype=jnp.int32)
        oa = swa_attend(qa, k_all, v_all, q_pos, k_pos, sinks)
    oa = oa.reshape(B, T, SWA_W)
    k_keep = k_all[:, -win_rows:]
    v_keep = v_all[:, -win_rows:]

    yconv, cbuf_new = short_conv(cc * ch, cbuf, conv_w)
    yc = cb * yconv

    out = jnp.concatenate([hm, oa, yc], axis=-1) @ w_out
    return out, (k_keep, v_keep, cbuf_new, C1, n1, m1)


PEER_SEL = PEER_HEADS * PEER_TOPK
PEER_TB = 128
PEER_NSLOT = 8


def _gelu_tanh(x):
    return 0.5 * x * (1.0 + jnp.tanh(math.sqrt(2.0 / math.pi) * (x + 0.044715 * (x * x * x))))


def _peer_experts_kernel(idx_ref, h_ref, gt_ref, xres_ref, gate_ref, tab_ref, o_ref, buf, sem):
    tb, d = h_ref.shape
    nch = d // LANES
    per_token_gate = gate_ref.shape[0] == tb

    def row_copy(e, slot, r):
        src = tab_ref.at[pl.ds(pl.multiple_of(e * (2 * nch), 2 * nch), 2 * nch)]
        return pltpu.make_async_copy(src, buf.at[slot, :, r, :], sem.at[slot])

    def issue(t, slot):
        for r in range(PEER_SEL):
            row_copy(idx_ref[t, r], slot, r).start()

    def wait(slot):
        pltpu.make_async_copy(tab_ref.at[pl.ds(0, PEER_SEL * 2 * nch)], buf.at[slot], sem.at[slot]).wait()

    lane = lax.broadcasted_iota(jnp.int32, (PEER_SEL, tb), 1)

    def ffn(t, slot):
        hrow = h_ref[pl.ds(t, 1), :]
        part = buf[slot, 0] * hrow[:, :LANES]
        for c in range(1, nch):
            part = part + buf[slot, c] * hrow[:, c * LANES:(c + 1) * LANES]
        s = jnp.sum(part, axis=-1, keepdims=True)
        g = jnp.sum(jnp.where(lane == t, gt_ref[...], 0.0), axis=-1, keepdims=True)
        w = g * _gelu_tanh(s)
        y = jnp.concatenate([jnp.sum(buf[slot, nch + c] * w, axis=0, keepdims=True) for c in range(nch)], axis=1)
        gate = gate_ref[pl.ds(t, 1), :] if per_token_gate else gate_ref[...]
        return xres_ref[pl.ds(t, 1), :] + gate * y

    ahead = PEER_NSLOT - 1
    for t0 in range(ahead):
        issue(t0, t0)

    def group(t0, last):
        for j in range(PEER_NSLOT):
            t = t0 + j
            wait(j)
            out = ffn(t, j)
            if not last or j == 0:
                issue(t + ahead, (j + ahead) % PEER_NSLOT)
            o_ref[pl.ds(t, 1), :] = out

    def main_body(gi, carry):
        group(pl.multiple_of(gi * PEER_NSLOT, PEER_NSLOT), False)
        return carry

    lax.fori_loop(0, tb // PEER_NSLOT - 1, main_body, 0)
    group(tb - PEER_NSLOT, True)


def peer_experts(h, idx_t, g_t, tab, xres, gate):
    n, d = h.shape
    tb = PEER_TB
    rows_per_gate = n // gate.shape[0]
    if rows_per_gate == 1:
        gate_spec = pl.BlockSpec((tb, d), lambda i: (i, 0))
    else:
        gate_spec = pl.BlockSpec((None, 1, d), lambda i: (i * tb // rows_per_gate, 0, 0))
        gate = gate[:, None, :]
    nrow = 2 * d // LANES
    return pl.pallas_call(
        _peer_experts_kernel,
        grid=(n // tb,),
        in_specs=[
            pl.BlockSpec((None, tb, PEER_SEL), lambda i: (i, 0, 0), memory_space=pltpu.SMEM),
            pl.BlockSpec((tb, d), lambda i: (i, 0)),
            pl.BlockSpec((None, PEER_SEL, tb), lambda i: (i, 0, 0)),
            pl.BlockSpec((tb, d), lambda i: (i, 0)),
            gate_spec,
            pl.BlockSpec(memory_space=pl.ANY),
        ],
        out_specs=pl.BlockSpec((tb, d), lambda i: (i, 0)),
        out_shape=jax.ShapeDtypeStruct((n, d), jnp.float32),
        scratch_shapes=[pltpu.VMEM((PEER_NSLOT, nrow, PEER_SEL, LANES), jnp.float32),
                        pltpu.SemaphoreType.DMA((PEER_NSLOT,))],
        compiler_params=pltpu.CompilerParams(dimension_semantics=("arbitrary",)),
        name="peer_experts",
    )(idx_t.transpose(0, 2, 1), h, g_t, xres, gate, tab.reshape(-1, LANES))


LANES = 128
ROUTE_TB = 256
HALF_KEY = PEER_DKEY // 2


def _top_rows(s, k, payload=None):
    rows = lax.broadcasted_iota(jnp.int32, s.shape, 0)
    big = jnp.int32(s.shape[0])
    vals, ids = [], []
    for _ in range(k):
        m = jnp.max(s, axis=0, keepdims=True)
        r = jnp.min(jnp.where(s == m, rows, big), axis=0, keepdims=True)
        hit = rows == r
        vals.append(m)
        ids.append(r if payload is None else jnp.max(jnp.where(hit, payload, -1), axis=0, keepdims=True))
        s = jnp.where(hit, -jnp.inf, s)
    return jnp.concatenate(vals, axis=0), jnp.concatenate(ids, axis=0)


def _peer_route_kernel(x_ref, gain_ref, sc_ref, sh_ref, wq_ref, sk_ref, h_ref, idx_ref, gt_ref, q_scr):
    tb = x_ref.shape[0]
    x = x_ref[...]
    y = x * lax.rsqrt(jnp.mean(x * x, axis=-1, keepdims=True) + EPS) * gain_ref[...]
    h = y * (1.0 + sc_ref[...]) + sh_ref[...]
    h_ref[...] = h
    q = jnp.dot(h.astype(jnp.bfloat16), wq_ref[...], preferred_element_type=jnp.float32)
    for sub in range(tb // LANES):
        for head in range(PEER_HEADS):
            q_scr[sub * PEER_HEADS + head] = q[sub * LANES:(sub + 1) * LANES, head * PEER_DKEY:(head + 1) * PEER_DKEY]

    def per_head(j, carry):
        sub = j // PEER_HEADS
        head = j % PEER_HEADS
        qh = q_scr[j].astype(jnp.bfloat16)
        tops = []
        for p in range(2):
            st = lax.dot_general(sk_ref[p], qh[:, p * HALF_KEY:(p + 1) * HALF_KEY],
                                 (((1,), (1,)), ((), ())), preferred_element_type=jnp.float32)
            tops.append(_top_rows(st, PEER_TOPK))
        (s1, i1), (s2, i2) = tops
        cand = jnp.concatenate([s1[a:a + 1] + s2 for a in range(PEER_TOPK)], axis=0)
        cidx = jnp.concatenate([i1[a:a + 1] * PEER_NKEYS + i2 for a in range(PEER_TOPK)], axis=0)
        top_s, top_i = _top_rows(cand, PEER_TOPK, payload=cidx)
        e = jnp.exp(top_s - jnp.max(top_s, axis=0, keepdims=True))
        off = pl.multiple_of(head * PEER_TOPK, PEER_TOPK)
        idx_ref[sub, pl.ds(off, PEER_TOPK), :] = top_i
        gt_ref[sub, pl.ds(off, PEER_TOPK), :] = e / jnp.sum(e, axis=0, keepdims=True)
        return carry

    lax.fori_loop(0, (tb // LANES) * PEER_HEADS, per_head, 0)


def peer_route(x, gain, sc, sh, wq_bf, sk_bf):
    n, d = x.shape
    tb = ROUTE_TB
    nsub = n // LANES
    per_token = sc.shape[0] == n
    rows_per_mod = n // sc.shape[0]
    if per_token:
        mod_spec = pl.BlockSpec((tb, d), lambda i: (i, 0))
    else:
        mod_spec = pl.BlockSpec((None, 1, d), lambda i: (i * tb // rows_per_mod, 0, 0))
        sc, sh = sc[:, None, :], sh[:, None, :]
    return pl.pallas_call(
        _peer_route_kernel,
        grid=(n // tb,),
        in_specs=[
            pl.BlockSpec((tb, d), lambda i: (i, 0)),
            pl.BlockSpec((1, d), lambda i: (0, 0)),
            mod_spec, mod_spec,
            pl.BlockSpec(wq_bf.shape, lambda i: (0, 0)),
            pl.BlockSpec(sk_bf.shape, lambda i: (0, 0, 0)),
        ],
        out_specs=[
            pl.BlockSpec((tb, d), lambda i: (i, 0)),
            pl.BlockSpec((tb // LANES, PEER_SEL, LANES), lambda i: (i, 0, 0)),
            pl.BlockSpec((tb // LANES, PEER_SEL, LANES), lambda i: (i, 0, 0)),
        ],
        out_shape=[jax.ShapeDtypeStruct((n, d), jnp.float32),
                   jax.ShapeDtypeStruct((nsub, PEER_SEL, LANES), jnp.int32),
                   jax.ShapeDtypeStruct((nsub, PEER_SEL, LANES), jnp.float32)],
        scratch_shapes=[pltpu.VMEM((tb // LANES * PEER_HEADS, LANES, PEER_DKEY), jnp.float32)],
        compiler_params=pltpu.CompilerParams(dimension_semantics=("arbitrary",)),
        name="peer_route",
    )(x, gain, sc, sh, wq_bf, sk_bf)


def peer_layer(x, gain, sc, sh, gt, wq_bf, sk_bf, tab):
    h, idx_t, g_t = peer_route(x, gain, sc, sh, wq_bf, sk_bf)
    return peer_experts(h, idx_t, g_t, tab, x, gt)


def trunk_layer(x, c, ada_w, ada_b, g_mix, g_ffn, w_in, w_out, gate_b, mh_g, sinks, conv_w,
                wq, subkeys, tab, state):
    mod = (jax.nn.silu(c) @ ada_w + ada_b)[:, None, :]
    sh1, sc1, gt1, sh2, sc2, gt2 = jnp.split(mod, 6, axis=-1)
    h = rmsnorm(x, g_mix) * (1 + sc1) + sh1
    mix, new_state = token_mixers(h, w_in, w_out, gate_b, mh_g, sinks, conv_w, state)
    x = x + gt1 * mix
    B, T, D = x.shape
    if T % ROUTE_TB:
        sc2, sh2, gt2 = (jnp.broadcast_to(m, (B, T, D)).reshape(B * T, D) for m in (sc2, sh2, gt2))
    else:
        sc2, sh2, gt2 = (m.reshape(B, D) for m in (sc2, sh2, gt2))
    x = peer_layer(x.reshape(B * T, D), g_ffn.reshape(1, D), sc2, sh2, gt2,
                   wq.astype(jnp.bfloat16), subkeys.astype(jnp.bfloat16), tab).reshape(B, T, D)
    return x, new_state


def _final_norm_kernel(x_ref, g_ref, o_ref):
    x = x_ref[...]
    y = x * lax.rsqrt(jnp.mean(jnp.square(x), axis=-1, keepdims=True) + EPS)
    o_ref[...] = y * g_ref[...]


def final_norm(x, g):
    shp = x.shape
    x2 = x.reshape(-1, shp[-1])
    n, d = x2.shape
    tm = 512
    out = pl.pallas_call(
        _final_norm_kernel,
        grid=(n // tm,),
        in_specs=[pl.BlockSpec((tm, d), lambda i: (i, 0)), pl.BlockSpec((1, d), lambda i: (0, 0))],
        out_specs=pl.BlockSpec((tm, d), lambda i: (i, 0)),
        out_shape=jax.ShapeDtypeStruct((n, d), x.dtype),
    )(x2, g.reshape(1, d))
    return out.reshape(shp)


def kernel(x_prompt, x_sample, cache_swa_k, cache_swa_v, state_conv, state_mlstm_C, state_mlstm_n, state_mlstm_m, c_prompt, c_sample, ada_w, ada_b, norm_mix_g, norm_ffn_g, w_in, w_out, mlstm_gate_b, mlstm_norm_g, swa_sinks, conv_w, peer_wq, peer_subkeys, peer_u, peer_v, final_g):
    xp, xs = x_prompt, x_sample
    new_p, new_s = [], []
    for l in range(DEPTH):
        w = (ada_w[l], ada_b[l], norm_mix_g[l], norm_ffn_g[l], w_in[l], w_out[l], mlstm_gate_b[l],
             mlstm_norm_g[l], swa_sinks[l], conv_w[l], peer_wq[l], peer_subkeys[l],
             jnp.concatenate([peer_u[l], peer_v[l]], axis=-1))
        xp, sp = trunk_layer(xp, c_prompt, *w, None)
        st = (cache_swa_k[l], cache_swa_v[l], state_conv[l], state_mlstm_C[l], state_mlstm_n[l], state_mlstm_m[l])
        xs, ss = trunk_layer(xs, c_sample, *w, st)
        new_p.append(sp)
        new_s.append(ss)
    y_prompt = final_norm(xp, final_g)
    y_sample = final_norm(xs, final_g)
    pk, pv, pc, pC, pn, pm = [jnp.stack(t) for t in zip(*new_p)]
    sk, sv, sc, sC, sn, sm = [jnp.stack(t) for t in zip(*new_s)]
    return (y_prompt, y_sample, pk, pv, pc, pC, pn, pm, sk, sv, sc, sC, sn, sm)
```

```python
import math
import numpy as np
import jax
import jax.numpy as jnp
from jax import lax
from jax.experimental import pallas as pl
from jax.experimental.pallas import tpu as pltpu

D_MODEL = 1024
BATCH = 8
SEQ = 4096
DEPTH = 2
DEC_BATCH = 128
DEC_SEQ = 4
PAST_LEN = 16384

MLSTM_W = D_MODEL // 2
MLSTM_HEADS = 4
MLSTM_DH = MLSTM_W // MLSTM_HEADS
MLSTM_CHUNK = 64
SWA_W = D_MODEL // 4
SWA_DH = 64
SWA_HEADS = SWA_W // SWA_DH
SWA_KV_HEADS = SWA_HEADS // 2
SWA_GROUP = SWA_HEADS // SWA_KV_HEADS
WINDOW = 128
CONV_W = D_MODEL - MLSTM_W - SWA_W
CONV_WIDTH = 3
MIX_W = MLSTM_W + SWA_W + CONV_W
IN_SPLITS = (MLSTM_W, MLSTM_W, MLSTM_W, MLSTM_W, MLSTM_HEADS, MLSTM_HEADS,
             SWA_W, SWA_KV_HEADS * SWA_DH, SWA_KV_HEADS * SWA_DH,
             CONV_W, CONV_W, CONV_W)
IN_W = sum(IN_SPLITS)
PEER_HEADS = 8
PEER_NKEYS = 128
PEER_EXPERTS = PEER_NKEYS * PEER_NKEYS
PEER_DKEY = 256
PEER_TOPK = 16
PEER_BLOCK = 256
EPS = 1e-6


def rmsnorm(x, g):
    xf = x.astype(jnp.float32)
    y = xf * lax.rsqrt(jnp.mean(jnp.square(xf), axis=-1, keepdims=True) + EPS)
    return (y * g.astype(jnp.float32)).astype(x.dtype)


def alibi_slopes():
    h = jnp.arange(1, SWA_HEADS + 1, dtype=jnp.float32)
    return jnp.exp2(-8.0 * h / SWA_HEADS).reshape(SWA_KV_HEADS, SWA_GROUP)


def _mlstm_chunk(carry, xs):
    C, n, m = carry
    q, k, v, ig, lf = xs
    L = q.shape[2]
    b = jnp.cumsum(lf, axis=-1)
    a = b + m[..., None]
    d = b[..., :, None] - b[..., None, :] + ig[..., None, :]
    causal = jnp.tril(jnp.ones((L, L), dtype=bool))
    d = jnp.where(causal, d, -jnp.inf)
    m_t = jnp.maximum(a, jnp.max(d, axis=-1))
    w_inter = jnp.exp(a - m_t)
    w_intra = jnp.exp(d - m_t[..., None])
    qk = jnp.einsum('bhtk,bhsk->bhts', q, k) * w_intra
    num = w_inter[..., None] * jnp.einsum('bhvk,bhtk->bhtv', C, q) + jnp.einsum('bhts,bhsv->bhtv', qk, v)
    den = w_inter * jnp.einsum('bhk,bhtk->bht', n, q) + jnp.sum(qk, axis=-1)
    h = num / jnp.maximum(jnp.abs(den), jnp.exp(-m_t))[..., None]
    wl_inter = w_inter[..., -1]
    wl_intra = w_intra[..., -1, :]
    C_new = wl_inter[..., None, None] * C + jnp.einsum('bhs,bhsv,bhsk->bhvk', wl_intra, v, k)
    n_new = wl_inter[..., None] * n + jnp.einsum('bhs,bhsk->bhk', wl_intra, k)
    return (C_new, n_new, m_t[..., -1]), h


def mlstm(q, k, v, ig, lf, C0, n0, m0):
    B, T = q.shape[:2]
    L = math.gcd(T, MLSTM_CHUNK)
    nc = T // L

    def to_chunks(t):
        t = jnp.swapaxes(t, 1, 2)
        t = t.reshape(t.shape[:2] + (nc, L) + t.shape[3:])
        return jnp.moveaxis(t, 2, 0)

    carry0 = (C0.astype(jnp.float32), n0.astype(jnp.float32), m0.astype(jnp.float32))
    xs = (to_chunks(q), to_chunks(k), to_chunks(v), to_chunks(ig), to_chunks(lf))
    (C1, n1, m1), h = lax.scan(_mlstm_chunk, carry0, xs)
    h = jnp.moveaxis(h, 0, 2).reshape(B, MLSTM_HEADS, T, MLSTM_DH)
    return jnp.swapaxes(h, 1, 2), C1, n1, m1


def swa_attend(q, k, v, q_pos, k_pos, sinks):
    s = jnp.einsum('...qhgd,...khd->...hgqk', q, k).astype(jnp.float32) * (SWA_DH ** -0.5)
    dist = q_pos[..., :, None] - k_pos[..., None, :]
    visible = (dist >= 0) & (dist <= WINDOW) & (k_pos[..., None, :] >= 0)
    s = s - alibi_slopes()[:, :, None, None] * jnp.expand_dims(dist, (-4, -3)).astype(jnp.float32)
    s = jnp.where(jnp.expand_dims(visible, (-4, -3)), s, -jnp.inf)
    sink = jnp.broadcast_to(sinks.astype(jnp.float32).reshape(SWA_KV_HEADS, SWA_GROUP, 1, 1), s.shape[:-1] + (1,))
    p = jax.nn.softmax(jnp.concatenate([s, sink], axis=-1), axis=-1)[..., :-1]
    return jnp.einsum('...hgqk,...khd->...qhgd', p.astype(v.dtype), v)


def swa_banded(q, k, v, sinks):
    B, S = q.shape[:2]
    nb = S // WINDOW
    qb = q.reshape(B, nb, WINDOW, SWA_KV_HEADS, SWA_GROUP, SWA_DH)
    kb = k.reshape(B, nb, WINDOW, SWA_KV_HEADS, SWA_DH)
    vb = v.reshape(B, nb, WINDOW, SWA_KV_HEADS, SWA_DH)
    pad = ((0, 0), (1, 0), (0, 0), (0, 0), (0, 0))
    kband = jnp.concatenate([jnp.pad(kb[:, :-1], pad), kb], axis=2)
    vband = jnp.concatenate([jnp.pad(vb[:, :-1], pad), vb], axis=2)
    pos = jnp.arange(S, dtype=jnp.int32).reshape(nb, WINDOW)
    kpos = jnp.concatenate([pos - WINDOW, pos], axis=1)
    o = swa_attend(qb, kband, vband, pos, kpos, sinks)
    return o.reshape(B, S, SWA_KV_HEADS, SWA_GROUP, SWA_DH)


def short_conv(u, buf, w):
    T = u.shape[1]
    full = jnp.concatenate([buf.astype(u.dtype), u], axis=1)
    y = sum(w[j] * full[:, j:j + T] for j in range(CONV_WIDTH))
    return y, full[:, T:]


def token_mixers(h, w_in, w_out, gate_b, mh_g, sinks, conv_w, state):
    B, T, _ = h.shape
    f32 = jnp.float32
    z = h @ w_in
    offs = np.cumsum(IN_SPLITS)[:-1].tolist()
    mq, mk, mv, mo, mi, mf, sq, sk, sv, cb, cc, ch = jnp.split(z, offs, axis=-1)
    if state is None:
        kbuf = vbuf = None
        cbuf = jnp.zeros((B, CONV_WIDTH - 1, CONV_W), h.dtype)
        C0 = jnp.zeros((B, MLSTM_HEADS, MLSTM_DH, MLSTM_DH), f32)
        n0 = jnp.zeros((B, MLSTM_HEADS, MLSTM_DH), f32)
        m0 = jnp.zeros((B, MLSTM_HEADS), f32)
        win_rows = min(WINDOW, PAST_LEN)
    else:
        kbuf, vbuf, cbuf, C0, n0, m0 = state
        win_rows = kbuf.shape[1]

    shp = (B, T, MLSTM_HEADS, MLSTM_DH)
    q = mq.reshape(shp).astype(f32)
    k = mk.reshape(shp).astype(f32) * (MLSTM_DH ** -0.5)
    v = mv.reshape(shp).astype(f32)
    pre = jnp.concatenate([mi, mf], axis=-1).astype(f32) + gate_b.astype(f32)
    ig = pre[..., :MLSTM_HEADS]
    lf = jax.nn.log_sigmoid(pre[..., MLSTM_HEADS:])
    hm, C1, n1, m1 = mlstm(q, k, v, ig, lf, C0, n0, m0)
    hm = hm * lax.rsqrt(jnp.mean(jnp.square(hm), axis=-1, keepdims=True) + EPS) * mh_g.astype(f32).reshape(MLSTM_HEADS, MLSTM_DH)
    hm = (hm * jax.nn.sigmoid(mo.reshape(shp).astype(f32))).astype(h.dtype).reshape(B, T, MLSTM_W)

    qa = sq.reshape(B, T, SWA_KV_HEADS, SWA_GROUP, SWA_DH)
    ka = sk.reshape(B, T, SWA_KV_HEADS, SWA_DH)
    va = sv.reshape(B, T, SWA_KV_HEADS, SWA_DH)
    if kbuf is None:
        oa = swa_banded(qa, ka, va, sinks)
        k_all, v_all = ka, va
    else:
        k_all = jnp.concatenate([kbuf.astype(ka.dtype), ka], axis=1)
        v_all = jnp.concatenate([vbuf.astype(va.dtype), va], axis=1)
        q_pos = PAST_LEN + jnp.arange(T, dtype=jnp.int32)
        k_pos = PAST_LEN - win_rows + jnp.arange(win_rows + T, dtype=jnp.int32)
        oa = swa_attend(qa, k_all, v_all, q_pos, k_pos, sinks)
    oa = oa.reshape(B, T, SWA_W)
    k_keep = k_all[:, -win_rows:]
    v_keep = v_all[:, -win_rows:]

    yconv, cbuf_new = short_conv(cc * ch, cbuf, conv_w)
    yc = cb * yconv

    out = jnp.concatenate([hm, oa, yc], axis=-1) @ w_out
    return out, (k_keep, v_keep, cbuf_new, C1, n1, m1)


PEER_SEL = PEER_HEADS * PEER_TOPK
PEER_TB = 128
PEER_NSLOT = 8


def _gelu_tanh(x):
    return 0.5 * x * (1.0 + jnp.tanh(math.sqrt(2.0 / math.pi) * (x + 0.044715 * (x * x * x))))


def _peer_experts_kernel(idx_ref, h_ref, gt_ref, xres_ref, gate_ref, tab_ref, o_ref, buf, sem):
    tb, d = h_ref.shape
    nch = d // LANES
    per_token_gate = gate_ref.shape[0] == tb

    def row_copy(e, slot, r):
        src = tab_ref.at[pl.ds(pl.multiple_of(e * (2 * nch), 2 * nch), 2 * nch)]
        return pltpu.make_async_copy(src, buf.at[slot, :, r, :], sem.at[slot])

    def issue(t, slot):
        for r in range(PEER_SEL):
            row_copy(idx_ref[t, r], slot, r).start(priority=r % 2)

    def wait(slot):
        pltpu.make_async_copy(tab_ref.at[pl.ds(0, PEER_SEL * 2 * nch)], buf.at[slot], sem.at[slot]).wait()

    lane = lax.broadcasted_iota(jnp.int32, (PEER_SEL, tb), 1)

    def ffn(t, slot):
        hrow = h_ref[pl.ds(t, 1), :]
        part = buf[slot, 0] * hrow[:, :LANES]
        for c in range(1, nch):
            part = part + buf[slot, c] * hrow[:, c * LANES:(c + 1) * LANES]
        s = jnp.sum(part, axis=-1, keepdims=True)
        g = jnp.sum(jnp.where(lane == t, gt_ref[...], 0.0), axis=-1, keepdims=True)
        w = g * _gelu_tanh(s)
        y = jnp.concatenate([jnp.sum(buf[slot, nch + c] * w, axis=0, keepdims=True) for c in range(nch)], axis=1)
        gate = gate_ref[pl.ds(t, 1), :] if per_token_gate else gate_ref[...]
        return xres_ref[pl.ds(t, 1), :] + gate * y

    ahead = PEER_NSLOT - 1
    for t0 in range(ahead):
        issue(t0, t0)

    def group(t0, last):
        for j in range(PEER_NSLOT):
            t = t0 + j
            wait(j)
            out = ffn(t, j)
            if not last or j == 0:
                issue(t + ahead, (j + ahead) % PEER_NSLOT)
            o_ref[pl.ds(t, 1), :] = out

    def main_body(gi, carry):
        group(pl.multiple_of(gi * PEER_NSLOT, PEER_NSLOT), False)
        return carry

    lax.fori_loop(0, tb // PEER_NSLOT - 1, main_body, 0)
    group(tb - PEER_NSLOT, True)


def peer_experts(h, idx_t, g_t, tab, xres, gate):
    n, d = h.shape
    tb = PEER_TB
    rows_per_gate = n // gate.shape[0]
    if rows_per_gate == 1:
        gate_spec = pl.BlockSpec((tb, d), lambda i: (i, 0))
    else:
        gate_spec = pl.BlockSpec((None, 1, d), lambda i: (i * tb // rows_per_gate, 0, 0))
        gate = gate[:, None, :]
    nrow = 2 * d // LANES
    return pl.pallas_call(
        _peer_experts_kernel,
        grid=(n // tb,),
        in_specs=[
            pl.BlockSpec((None, tb, PEER_SEL), lambda i: (i, 0, 0), memory_space=pltpu.SMEM),
            pl.BlockSpec((tb, d), lambda i: (i, 0)),
            pl.BlockSpec((None, PEER_SEL, tb), lambda i: (i, 0, 0)),
            pl.BlockSpec((tb, d), lambda i: (i, 0)),
            gate_spec,
            pl.BlockSpec(memory_space=pl.ANY),
        ],
        out_specs=pl.BlockSpec((tb, d), lambda i: (i, 0)),
        out_shape=jax.ShapeDtypeStruct((n, d), jnp.float32),
        scratch_shapes=[pltpu.VMEM((PEER_NSLOT, nrow, PEER_SEL, LANES), jnp.float32),
                        pltpu.SemaphoreType.DMA((PEER_NSLOT,))],
        compiler_params=pltpu.CompilerParams(dimension_semantics=("arbitrary",)),
        name="peer_experts",
    )(idx_t.transpose(0, 2, 1), h, g_t, xres, gate, tab.reshape(-1, LANES))


LANES = 128
ROUTE_TB = 256
HALF_KEY = PEER_DKEY // 2


def _top_rows(s, k, payload=None):
    rows = lax.broadcasted_iota(jnp.int32, s.shape, 0)
    big = jnp.int32(s.shape[0])
    vals, ids = [], []
    for _ in range(k):
        m = jnp.max(s, axis=0, keepdims=True)
        r = jnp.min(jnp.where(s == m, rows, big), axis=0, keepdims=True)
        hit = rows == r
        vals.append(m)
        ids.append(r if payload is None else jnp.max(jnp.where(hit, payload, -1), axis=0, keepdims=True))
        s = jnp.where(hit, -jnp.inf, s)
    return jnp.concatenate(vals, axis=0), jnp.concatenate(ids, axis=0)


def _peer_route_kernel(x_ref, gain_ref, sc_ref, sh_ref, wq_ref, sk_ref, h_ref, idx_ref, gt_ref, q_scr):
    tb = x_ref.shape[0]
    x = x_ref[...]
    y = x * lax.rsqrt(jnp.mean(x * x, axis=-1, keepdims=True) + EPS) * gain_ref[...]
    h = y * (1.0 + sc_ref[...]) + sh_ref[...]
    h_ref[...] = h
    q = jnp.dot(h.astype(jnp.bfloat16), wq_ref[...], preferred_element_type=jnp.float32)
    for sub in range(tb // LANES):
        for head in range(PEER_HEADS):
            q_scr[sub * PEER_HEADS + head] = q[sub * LANES:(sub + 1) * LANES, head * PEER_DKEY:(head + 1) * PEER_DKEY]

    def per_head(j, carry):
        sub = j // PEER_HEADS
        head = j % PEER_HEADS
        qh = q_scr[j].astype(jnp.bfloat16)
        tops = []
        for p in range(2):
            st = lax.dot_general(sk_ref[p], qh[:, p * HALF_KEY:(p + 1) * HALF_KEY],
                                 (((1,), (1,)), ((), ())), preferred_element_type=jnp.float32)
            tops.append(_top_rows(st, PEER_TOPK))
        (s1, i1), (s2, i2) = tops
        cand = jnp.concatenate([s1[a:a + 1] + s2 for a in range(PEER_TOPK)], axis=0)
        cidx = jnp.concatenate([i1[a:a + 1] * PEER_NKEYS + i2 for a in range(PEER_TOPK)], axis=0)
        top_s, top_i = _top_rows(cand, PEER_TOPK, payload=cidx)
        e = jnp.exp(top_s - jnp.max(top_s, axis=0, keepdims=True))
        off = pl.multiple_of(head * PEER_TOPK, PEER_TOPK)
        idx_ref[sub, pl.ds(off, PEER_TOPK), :] = top_i
        gt_ref[sub, pl.ds(off, PEER_TOPK), :] = e / jnp.sum(e, axis=0, keepdims=True)
        return carry

    lax.fori_loop(0, (tb // LANES) * PEER_HEADS, per_head, 0)


def peer_route(x, gain, sc, sh, wq_bf, sk_bf):
    n, d = x.shape
    tb = ROUTE_TB
    nsub = n // LANES
    per_token = sc.shape[0] == n
    rows_per_mod = n // sc.shape[0]
    if per_token:
        mod_spec = pl.BlockSpec((tb, d), lambda i: (i, 0))
    else:
        mod_spec = pl.BlockSpec((None, 1, d), lambda i: (i * tb // rows_per_mod, 0, 0))
        sc, sh = sc[:, None, :], sh[:, None, :]
    return pl.pallas_call(
        _peer_route_kernel,
        grid=(n // tb,),
        in_specs=[
            pl.BlockSpec((tb, d), lambda i: (i, 0)),
            pl.BlockSpec((1, d), lambda i: (0, 0)),
            mod_spec, mod_spec,
            pl.BlockSpec(wq_bf.shape, lambda i: (0, 0)),
            pl.BlockSpec(sk_bf.shape, lambda i: (0, 0, 0)),
        ],
        out_specs=[
            pl.BlockSpec((tb, d), lambda i: (i, 0)),
            pl.BlockSpec((tb // LANES, PEER_SEL, LANES), lambda i: (i, 0, 0)),
            pl.BlockSpec((tb // LANES, PEER_SEL, LANES), lambda i: (i, 0, 0)),
        ],
        out_shape=[jax.ShapeDtypeStruct((n, d), jnp.float32),
                   jax.ShapeDtypeStruct((nsub, PEER_SEL, LANES), jnp.int32),
                   jax.ShapeDtypeStruct((nsub, PEER_SEL, LANES), jnp.float32)],
        scratch_shapes=[pltpu.VMEM((tb // LANES * PEER_HEADS, LANES, PEER_DKEY), jnp.float32)],
        compiler_params=pltpu.CompilerParams(dimension_semantics=("arbitrary",)),
        name="peer_route",
    )(x, gain, sc, sh, wq_bf, sk_bf)


def peer_layer(x, gain, sc, sh, gt, wq_bf, sk_bf, tab):
    h, idx_t, g_t = peer_route(x, gain, sc, sh, wq_bf, sk_bf)
    return peer_experts(h, idx_t, g_t, tab, x, gt)


def trunk_layer(x, c, ada_w, ada_b, g_mix, g_ffn, w_in, w_out, gate_b, mh_g, sinks, conv_w,
                wq, subkeys, tab, state):
    mod = (jax.nn.silu(c) @ ada_w + ada_b)[:, None, :]
    sh1, sc1, gt1, sh2, sc2, gt2 = jnp.split(mod, 6, axis=-1)
    h = rmsnorm(x, g_mix) * (1 + sc1) + sh1
    mix, new_state = token_mixers(h, w_in, w_out, gate_b, mh_g, sinks, conv_w, state)
    x = x + gt1 * mix
    B, T, D = x.shape
    if T % ROUTE_TB:
        sc2, sh2, gt2 = (jnp.broadcast_to(m, (B, T, D)).reshape(B * T, D) for m in (sc2, sh2, gt2))
    else:
        sc2, sh2, gt2 = (m.reshape(B, D) for m in (sc2, sh2, gt2))
    x = peer_layer(x.reshape(B * T, D), g_ffn.reshape(1, D), sc2, sh2, gt2,
                   wq.astype(jnp.bfloat16), subkeys.astype(jnp.bfloat16), tab).reshape(B, T, D)
    return x, new_state


def _final_norm_kernel(x_ref, g_ref, o_ref):
    x = x_ref[...]
    y = x * lax.rsqrt(jnp.mean(jnp.square(x), axis=-1, keepdims=True) + EPS)
    o_ref[...] = y * g_ref[...]


def final_norm(x, g):
    shp = x.shape
    x2 = x.reshape(-1, shp[-1])
    n, d = x2.shape
    tm = 512
    out = pl.pallas_call(
        _final_norm_kernel,
        grid=(n // tm,),
        in_specs=[pl.BlockSpec((tm, d), lambda i: (i, 0)), pl.BlockSpec((1, d), lambda i: (0, 0))],
        out_specs=pl.BlockSpec((tm, d), lambda i: (i, 0)),
        out_shape=jax.ShapeDtypeStruct((n, d), x.dtype),
    )(x2, g.reshape(1, d))
    return out.reshape(shp)


def kernel(x_prompt, x_sample, cache_swa_k, cache_swa_v, state_conv, state_mlstm_C, state_mlstm_n, state_mlstm_m, c_prompt, c_sample, ada_w, ada_b, norm_mix_g, norm_ffn_g, w_in, w_out, mlstm_gate_b, mlstm_norm_g, swa_sinks, conv_w, peer_wq, peer_subkeys, peer_u, peer_v, final_g):
    xp, xs = x_prompt, x_sample
    new_p, new_s = [], []
    for l in range(DEPTH):
        w = (ada_w[l], ada_b[l], norm_mix_g[l], norm_ffn_g[l], w_in[l], w_out[l], mlstm_gate_b[l],
             mlstm_norm_g[l], swa_sinks[l], conv_w[l], peer_wq[l], peer_subkeys[l],
             jnp.concatenate([peer_u[l], peer_v[l]], axis=-1))
        xp, sp = trunk_layer(xp, c_prompt, *w, None)
        st = (cache_swa_k[l], cache_swa_v[l], state_conv[l], state_mlstm_C[l], state_mlstm_n[l], state_mlstm_m[l])
        xs, ss = trunk_layer(xs, c_sample, *w, st)
        new_p.append(sp)
        new_s.append(ss)
    y_prompt = final_norm(xp, final_g)
    y_sample = final_norm(xs, final_g)
    pk, pv, pc, pC, pn, pm = [jnp.stack(t) for t in zip(*new_p)]
    sk, sv, sc, sC, sn, sm = [jnp.stack(t) for t in zip(*new_s)]
    return (y_prompt, y_sample, pk, pv, pc, pC, pn, pm, sk, sv, sc, sC, sn, sm)
```

```python
import functools
import math

import jax
import jax.numpy as jnp
from jax import lax
from jax.experimental import pallas as pl
from jax.experimental.pallas import tpu as pltpu

D_MODEL = 1024
DEPTH = 2
PAST_LEN = 16384

MLSTM_W = D_MODEL // 2
MLSTM_HEADS = 4
MLSTM_DH = MLSTM_W // MLSTM_HEADS
MLSTM_CHUNK = 64
SWA_W = D_MODEL // 4
SWA_DH = 64
SWA_HEADS = SWA_W // SWA_DH
SWA_KV_HEADS = SWA_HEADS // 2
SWA_GROUP = SWA_HEADS // SWA_KV_HEADS
SWA_KVW = SWA_KV_HEADS * SWA_DH
WINDOW = 128
CONV_W = D_MODEL - MLSTM_W - SWA_W
CONV_WIDTH = 3
PEER_HEADS = 8
PEER_NKEYS = 128
PEER_DKEY = 256
HALF_KEY = PEER_DKEY // 2
PEER_TOPK = 16
PEER_SEL = PEER_HEADS * PEER_TOPK
EPS = 1e-6

LANES = 128
SUBLANES = 8
ROW_TILE = 512
ROUTE_TB = 256
PEER_TB = 128
PEER_NSLOT = 8

Z_Q, Z_K, Z_V, Z_O = 0, MLSTM_W, 2 * MLSTM_W, 3 * MLSTM_W
Z_SQ = 4 * MLSTM_W
Z_SK = Z_SQ + SWA_W
Z_SV = Z_SK + SWA_KVW
Z_CB = Z_SV + SWA_KVW
Z_CC = Z_CB + CONV_W
Z_CH = Z_CC + CONV_W
Z_GATE = Z_CH + CONV_W
Z_W = Z_GATE + LANES
ALIBI = tuple(2.0 ** (-8.0 * h / SWA_HEADS) for h in range(1, SWA_HEADS + 1))


def _bdot(a, b, dims):
    return lax.dot_general(a.astype(jnp.bfloat16), b.astype(jnp.bfloat16), (dims, ((), ())),
                           preferred_element_type=jnp.float32)


def _row_spec(arr, n, tm, d):
    g = arr.shape[0]
    if g == n:
        return arr, pl.BlockSpec((tm, d), lambda i, *_: (i, 0))
    per = n // g
    return arr[:, None, :], pl.BlockSpec((None, 1, d), lambda i, *_: (i * tm // per, 0, 0))


def _norm_mod(x, gain, sc, sh):
    y = x * lax.rsqrt(jnp.mean(x * x, axis=-1, keepdims=True) + EPS) * gain
    return y * (1.0 + sc) + sh


def _adaln_kernel(c_ref, w_ref, b_ref, o_ref):
    c = c_ref[...]
    o_ref[...] = _bdot(c * jax.nn.sigmoid(c), w_ref[...], ((1,), (0,))) + b_ref[...]


def adaln_mod(c, w, b):
    bsz, d = c.shape
    m = w.shape[1]
    tn = 512
    return pl.pallas_call(
        _adaln_kernel,
        grid=(m // tn,),
        in_specs=[pl.BlockSpec((bsz, d), lambda j: (0, 0)),
                  pl.BlockSpec((d, tn), lambda j: (0, j)),
                  pl.BlockSpec((1, tn), lambda j: (0, j))],
        out_specs=pl.BlockSpec((bsz, tn), lambda j: (0, j)),
        out_shape=jax.ShapeDtypeStruct((bsz, m), jnp.float32),
        name="adaln_mod",
    )(c, w, b.reshape(1, m))


def _mix_in_kernel(x_ref, gain_ref, sc_ref, sh_ref, w_ref, z_ref):
    h = _norm_mod(x_ref[...], gain_ref[...], sc_ref[...], sh_ref[...])
    z_ref[...] = jnp.dot(h.astype(jnp.bfloat16), w_ref[...], preferred_element_type=jnp.float32)


def mix_in(x, gain, sc, sh, w_bf):
    n, d = x.shape
    zw = w_bf.shape[1]
    tm, tn = min(ROW_TILE, n), zw // 3
    sc, sc_spec = _row_spec(sc, n, tm, d)
    sh, sh_spec = _row_spec(sh, n, tm, d)
    return pl.pallas_call(
        _mix_in_kernel,
        grid=(n // tm, zw // tn),
        in_specs=[pl.BlockSpec((tm, d), lambda i, j: (i, 0)),
                  pl.BlockSpec((1, d), lambda i, j: (0, 0)),
                  sc_spec, sh_spec,
                  pl.BlockSpec((d, tn), lambda i, j: (0, j))],
        out_specs=pl.BlockSpec((tm, tn), lambda i, j: (i, j)),
        out_shape=jax.ShapeDtypeStruct((n, zw), jnp.float32),
        name="mix_in",
    )(x, gain, sc, sh, w_bf)


def _log_sigmoid(x):
    return jnp.minimum(x, 0.0) - jnp.log(1.0 + jnp.exp(-jnp.abs(x)))


def _mlstm_kernel(chunk, valid, q_ref, k_ref, v_ref, o_ref, gate_ref, gb_ref, g_ref, c0_ref, n0_ref, m0_ref,
                  hm_ref, c_ref, n_ref, m_ref):
    tc = q_ref.shape[0]
    nh, dh, L = MLSTM_HEADS, MLSTM_DH, chunk

    @pl.when(pl.program_id(1) == 0)
    def _():
        c_ref[...] = c0_ref[...]
        n_ref[...] = n0_ref[...]
        m_ref[...] = m0_ref[...]

    row = lax.broadcasted_iota(jnp.int32, (L, L), 0)
    col = lax.broadcasted_iota(jnp.int32, (L, L), 1)
    eye = row == col
    visible = (col <= row) & (col < valid)
    rcol = lax.broadcasted_iota(jnp.int32, (L, 1), 0)
    gb = gb_ref[...]
    gain = g_ref[...]

    def to_row(x_col):
        return jnp.sum(jnp.where(eye, x_col, 0.0), axis=0, keepdims=True)

    def one_chunk(ci, carry):
        r0 = pl.multiple_of(ci * L, L)
        pre = gate_ref[pl.ds(r0, L), :] + gb
        for hd in range(nh):
            sl = slice(hd * dh, (hd + 1) * dh)
            q = q_ref[pl.ds(r0, L), sl]
            k = k_ref[pl.ds(r0, L), sl] * (dh ** -0.5)
            v = v_ref[pl.ds(r0, L), sl]
            ig = pre[:, hd:hd + 1]
            lf = jnp.where(rcol < valid, _log_sigmoid(pre[:, nh + hd:nh + hd + 1]), 0.0)
            b = lf
            s = 1
            while s < L:
                b = b + jnp.where(rcol >= s, pltpu.roll(b, s, 0), 0.0)
                s *= 2
            cmat, nrow, m_prev = c_ref[hd], n_ref[hd], m_ref[hd][:, 0:1]
            a = b + m_prev
            d = jnp.where(visible, b - to_row(b) + to_row(ig), -jnp.inf)
            m_t = jnp.maximum(a, jnp.max(d, axis=-1, keepdims=True))
            w_inter = jnp.exp(a - m_t)
            w_intra = jnp.exp(d - m_t)
            qk = _bdot(q, k, ((1,), (1,))) * w_intra
            num = w_inter * _bdot(q, cmat, ((1,), (1,))) + _bdot(qk, v, ((1,), (0,)))
            den = w_inter * jnp.sum(q * nrow, axis=-1, keepdims=True) + jnp.sum(qk, axis=-1, keepdims=True)
            h = num / jnp.maximum(jnp.abs(den), jnp.exp(-m_t))
            hn = h * lax.rsqrt(jnp.mean(h * h, axis=-1, keepdims=True) + EPS) * gain[:, sl]
            hm_ref[pl.ds(r0, L), sl] = hn * jax.nn.sigmoid(o_ref[pl.ds(r0, L), sl])
            m_last = m_t[L - 1:L]
            wl_inter = w_inter[L - 1:L]
            wl = jnp.where(rcol < valid, jnp.exp(b[L - 1:L] - b + ig - m_last), 0.0)
            c_ref[hd] = wl_inter * cmat + _bdot(wl * v, k, ((0,), (0,)))
            n_ref[hd] = wl_inter * nrow + jnp.sum(wl * k, axis=0, keepdims=True)
            m_ref[hd] = jnp.broadcast_to(m_last, (1, dh))
        return carry

    lax.fori_loop(0, tc // L, one_chunk, 0)


def mlstm_layer(z, gate_b, mh_g, c0, n0, m0, seq_len):
    bsz = z.shape[0] // seq_len
    nh, dh = MLSTM_HEADS, MLSTM_DH
    chunk = math.gcd(seq_len, MLSTM_CHUNK)
    valid, padded = chunk, seq_len
    if chunk % SUBLANES:
        assert seq_len < SUBLANES
        chunk = padded = SUBLANES
        z = jnp.pad(z.reshape(bsz, seq_len, -1), ((0, 0), (0, padded - seq_len), (0, 0))).reshape(bsz * padded, -1)
    n = bsz * padded
    tc = min(padded, ROW_TILE)
    steps = padded // tc
    w = nh * dh
    gb = jnp.zeros((1, LANES), jnp.float32).at[0, :2 * nh].set(gate_b)

    def zcol(off, width):
        return pl.BlockSpec((tc, width), lambda b, c: (b * steps + c, off // width))

    def state(shape):
        return pl.BlockSpec((None,) + shape, lambda b, c: (b,) + (0,) * len(shape))

    hm, c1, n1, m1 = pl.pallas_call(
        functools.partial(_mlstm_kernel, chunk, valid),
        grid=(bsz, steps),
        in_specs=[zcol(Z_Q, w), zcol(Z_K, w), zcol(Z_V, w), zcol(Z_O, w), zcol(Z_GATE, LANES),
                  pl.BlockSpec((1, LANES), lambda b, c: (0, 0)),
                  pl.BlockSpec((1, w), lambda b, c: (0, 0)),
                  state((nh, dh, dh)), state((nh, 1, dh)), state((nh, 1, dh))],
        out_specs=[pl.BlockSpec((tc, w), lambda b, c: (b * steps + c, 0)),
                   state((nh, dh, dh)), state((nh, 1, dh)), state((nh, 1, dh))],
        out_shape=[jax.ShapeDtypeStruct((n, w), jnp.float32),
                   jax.ShapeDtypeStruct((bsz, nh, dh, dh), jnp.float32),
                   jax.ShapeDtypeStruct((bsz, nh, 1, dh), jnp.float32),
                   jax.ShapeDtypeStruct((bsz, nh, 1, dh), jnp.float32)],
        compiler_params=pltpu.CompilerParams(dimension_semantics=("arbitrary", "arbitrary")),
        name="mlstm",
    )(z, z, z, z, z, gb, mh_g.reshape(1, w), c0, n0[:, :, None, :],
      jnp.broadcast_to(m0[:, :, None, None], (bsz, nh, 1, dh)))
    if padded != seq_len:
        hm = hm.reshape(bsz, padded, w)[:, :seq_len].reshape(bsz * seq_len, w)
    return hm, c1, n1[:, :, 0, :], m1[:, :, 0, 0]


def _attend(q, kb, vb, dist, visible, slope, sink):
    s = _bdot(q, kb, ((1,), (1,))) * (SWA_DH ** -0.5) - slope * dist.astype(jnp.float32)
    s = jnp.where(visible, s, -jnp.inf)
    mx = jnp.maximum(jnp.max(s, axis=-1, keepdims=True), sink)
    e = jnp.exp(s - mx)
    p = e / (jnp.sum(e, axis=-1, keepdims=True) + jnp.exp(sink - mx))
    return _bdot(p, vb, ((1,), (0,)))


def _swa_banded_kernel(q_ref, kp_ref, kc_ref, vp_ref, vc_ref, sink_ref, o_ref):
    j = pl.program_id(1)
    w = q_ref.shape[0]
    t = lax.broadcasted_iota(jnp.int32, (w, 2 * w), 0)
    i = lax.broadcasted_iota(jnp.int32, (w, 2 * w), 1)
    dist = t + w - i
    visible = (dist >= 0) & (dist <= WINDOW) & ((j > 0) | (i >= w))
    sinks = sink_ref[...]
    for kv in range(SWA_KV_HEADS):
        ks = slice(kv * SWA_DH, (kv + 1) * SWA_DH)
        kb = jnp.concatenate([kp_ref[:, ks], kc_ref[:, ks]], axis=0)
        vb = jnp.concatenate([vp_ref[:, ks], vc_ref[:, ks]], axis=0)
        for g in range(SWA_GROUP):
            h = kv * SWA_GROUP + g
            hs = slice(h * SWA_DH, (h + 1) * SWA_DH)
            o_ref[:, hs] = _attend(q_ref[:, hs], kb, vb, dist, visible, ALIBI[h], sinks[:, h:h + 1])


def swa_banded(z, sinks, seq_len):
    n = z.shape[0]
    w = WINDOW
    nb = seq_len // w
    sink_row = jnp.zeros((1, LANES), jnp.float32).at[0, :SWA_HEADS].set(sinks)

    def cur(off, width):
        return pl.BlockSpec((w, width), lambda b, j: (b * nb + j, off // width))

    def prev(off, width):
        return pl.BlockSpec((w, width), lambda b, j: (b * nb + jnp.maximum(j - 1, 0), off // width))

    return pl.pallas_call(
        _swa_banded_kernel,
        grid=(n // seq_len, nb),
        in_specs=[cur(Z_SQ, SWA_W), prev(Z_SK, SWA_KVW), cur(Z_SK, SWA_KVW), prev(Z_SV, SWA_KVW),
                  cur(Z_SV, SWA_KVW), pl.BlockSpec((1, LANES), lambda b, j: (0, 0))],
        out_specs=pl.BlockSpec((w, SWA_W), lambda b, j: (b * nb + j, 0)),
        out_shape=jax.ShapeDtypeStruct((n, SWA_W), jnp.float32),
        name="swa_banded",
    )(z, z, z, z, z, sink_row)


def _swa_cached_kernel(t_new, n_keys, q_ref, k_ref, v_ref, sink_ref, o_ref):
    m = q_ref.shape[1]
    nkp = k_ref.shape[0]
    r = lax.broadcasted_iota(jnp.int32, (m, nkp), 0)
    i = lax.broadcasted_iota(jnp.int32, (m, nkp), 1)
    first_key_pos = PAST_LEN - (n_keys - t_new)
    dist = (n_keys - t_new) + r % t_new - i
    visible = (dist >= 0) & (dist <= WINDOW) & (i < n_keys) & (first_key_pos + i >= 0)
    rg = lax.broadcasted_iota(jnp.int32, (m, 1), 0) // t_new
    sinks = sink_ref[...]
    for kv in range(SWA_KV_HEADS):
        ks = slice(kv * SWA_DH, (kv + 1) * SWA_DH)
        h0 = kv * SWA_GROUP
        slope = jnp.where(rg == 0, ALIBI[h0], ALIBI[h0 + 1])
        sink = jnp.where(rg == 0, sinks[:, h0:h0 + 1], sinks[:, h0 + 1:h0 + 2])
        o_ref[kv] = _attend(q_ref[kv], k_ref[:, ks], v_ref[:, ks], dist, visible, slope, sink)


def swa_cached(q, k_all, v_all, sinks):
    assert SWA_GROUP == 2
    bsz, t_new, _ = q.shape
    n_keys = k_all.shape[1]
    nkp = -(-n_keys // SUBLANES) * SUBLANES
    pad = ((0, 0), (0, nkp - n_keys), (0, 0))
    k_all, v_all = jnp.pad(k_all, pad), jnp.pad(v_all, pad)
    m = SWA_GROUP * t_new
    qs = q.reshape(bsz, t_new, SWA_KV_HEADS, SWA_GROUP, SWA_DH).transpose(0, 2, 3, 1, 4)
    qs = qs.reshape(bsz, SWA_KV_HEADS, m, SWA_DH)
    sink_row = jnp.zeros((1, LANES), jnp.float32).at[0, :SWA_HEADS].set(sinks)
    o = pl.pallas_call(
        functools.partial(_swa_cached_kernel, t_new, n_keys),
        grid=(bsz,),
        in_specs=[pl.BlockSpec((None, SWA_KV_HEADS, m, SWA_DH), lambda b: (b, 0, 0, 0)),
                  pl.BlockSpec((None, nkp, SWA_KVW), lambda b: (b, 0, 0)),
                  pl.BlockSpec((None, nkp, SWA_KVW), lambda b: (b, 0, 0)),
                  pl.BlockSpec((1, LANES), lambda b: (0, 0))],
        out_specs=pl.BlockSpec((None, SWA_KV_HEADS, m, SWA_DH), lambda b: (b, 0, 0, 0)),
        out_shape=jax.ShapeDtypeStruct((bsz, SWA_KV_HEADS, m, SWA_DH), jnp.float32),
        name="swa_cached",
    )(qs, k_all, v_all, sink_row)
    o = o.reshape(bsz, SWA_KV_HEADS, SWA_GROUP, t_new, SWA_DH).transpose(0, 3, 1, 2, 4)
    return o.reshape(bsz, t_new, SWA_W)


def _mix_out_kernel(seq_len, x_ref, gate_ref, hm_ref, oa_ref, cb_ref, cc_ref, ch_ref, hcc_ref, hch_ref,
                    pa_ref, pb_ref, cw_ref, w_ref, o_ref, u_ref):
    i = pl.program_id(0)
    tm = x_ref.shape[0]
    u = cc_ref[...] * ch_ref[...]
    u_ref[...] = u
    hu = hcc_ref[...] * hch_ref[...]
    r = lax.broadcasted_iota(jnp.int32, u.shape, 0)
    p = (i * tm + r) % seq_len
    pa, pb = pa_ref[...], pb_ref[...]
    last, last2 = hu[SUBLANES - 1:SUBLANES], hu[SUBLANES - 2:SUBLANES - 1]
    u1 = jnp.where(r >= 1, pltpu.roll(u, 1, 0), last)
    u2 = jnp.where(r >= 2, pltpu.roll(u, 2, 0), jnp.where(r == 1, last, last2))
    s1 = jnp.where(p >= 1, u1, pa)
    s2 = jnp.where(p >= 2, u2, jnp.where(p == 1, pa, pb))
    cw = cw_ref[...]
    yc = cb_ref[...] * (cw[0:1] * s2 + cw[1:2] * s1 + cw[2:3] * u)
    cat = jnp.concatenate([hm_ref[...], oa_ref[...], yc], axis=-1)
    mix = jnp.dot(cat.astype(jnp.bfloat16), w_ref[...], preferred_element_type=jnp.float32)
    o_ref[...] = x_ref[...] + gate_ref[...] * mix


def mix_out(x, gate, hm, oa, z, conv_prev, conv_w, w_out_bf, seq_len):
    n, d = x.shape
    c = CONV_W
    tm = min(ROW_TILE, n)
    gate, gate_spec = _row_spec(gate, n, tm, d)
    if seq_len % tm:
        pa = jnp.repeat(conv_prev[:, 1], seq_len, axis=0)
        pb = jnp.repeat(conv_prev[:, 0], seq_len, axis=0)
    else:
        pa, pb = conv_prev[:, 1], conv_prev[:, 0]
    pa, pa_spec = _row_spec(pa, n, tm, c)
    pb, pb_spec = _row_spec(pb, n, tm, c)
    cw = jnp.zeros((SUBLANES, c), jnp.float32).at[:CONV_WIDTH].set(conv_w)

    def zcol(off, width):
        return pl.BlockSpec((tm, width), lambda i: (i, off // width))

    def zhalo(off):
        return pl.BlockSpec((SUBLANES, c), lambda i: (jnp.maximum(i * (tm // SUBLANES) - 1, 0), off // c))

    return pl.pallas_call(
        functools.partial(_mix_out_kernel, seq_len),
        grid=(n // tm,),
        in_specs=[pl.BlockSpec((tm, d), lambda i: (i, 0)), gate_spec,
                  pl.BlockSpec((tm, MLSTM_W), lambda i: (i, 0)),
                  pl.BlockSpec((tm, SWA_W), lambda i: (i, 0)),
                  zcol(Z_CB, c), zcol(Z_CC, c), zcol(Z_CH, c), zhalo(Z_CC), zhalo(Z_CH),
                  pa_spec, pb_spec,
                  pl.BlockSpec((SUBLANES, c), lambda i: (0, 0)),
                  pl.BlockSpec((d, d), lambda i: (0, 0))],
        out_specs=[pl.BlockSpec((tm, d), lambda i: (i, 0)), pl.BlockSpec((tm, c), lambda i: (i, 0))],
        out_shape=[jax.ShapeDtypeStruct((n, d), jnp.float32), jax.ShapeDtypeStruct((n, c), jnp.float32)],
        name="mix_out",
    )(x, gate, hm, oa, z, z, z, z, z, pa, pb, cw, w_out_bf)


def _top_rows(s, k, payload=None):
    rows = lax.broadcasted_iota(jnp.int32, s.shape, 0)
    big = jnp.int32(s.shape[0])
    vals, ids = [], []
    for _ in range(k):
        m = jnp.max(s, axis=0, keepdims=True)
        r = jnp.min(jnp.where(s == m, rows, big), axis=0, keepdims=True)
        hit = rows == r
        vals.append(m)
        ids.append(r if payload is None else jnp.max(jnp.where(hit, payload, -1), axis=0, keepdims=True))
        s = jnp.where(hit, -jnp.inf, s)
    return jnp.concatenate(vals, axis=0), jnp.concatenate(ids, axis=0)


def _peer_route_kernel(x_ref, gain_ref, sc_ref, sh_ref, wq_ref, sk_ref, h_ref, idx_ref, gt_ref, q_scr):
    tb = x_ref.shape[0]
    h = _norm_mod(x_ref[...], gain_ref[...], sc_ref[...], sh_ref[...])
    h_ref[...] = h
    q = jnp.dot(h.astype(jnp.bfloat16), wq_ref[...], preferred_element_type=jnp.float32)
    for sub in range(tb // LANES):
        for head in range(PEER_HEADS):
            q_scr[sub * PEER_HEADS + head] = q[sub * LANES:(sub + 1) * LANES, head * PEER_DKEY:(head + 1) * PEER_DKEY]

    def per_head(j, carry):
        sub = j // PEER_HEADS
        head = j % PEER_HEADS
        qh = q_scr[j].astype(jnp.bfloat16)
        tops = []
        for p in range(2):
            st = lax.dot_general(sk_ref[p], qh[:, p * HALF_KEY:(p + 1) * HALF_KEY],
                                 (((1,), (1,)), ((), ())), preferred_element_type=jnp.float32)
            tops.append(_top_rows(st, PEER_TOPK))
        (s1, i1), (s2, i2) = tops
        cand = jnp.concatenate([s1[a:a + 1] + s2 for a in range(PEER_TOPK)], axis=0)
        cidx = jnp.concatenate([i1[a:a + 1] * PEER_NKEYS + i2 for a in range(PEER_TOPK)], axis=0)
        top_s, top_i = _top_rows(cand, PEER_TOPK, payload=cidx)
        e = jnp.exp(top_s - jnp.max(top_s, axis=0, keepdims=True))
        off = pl.multiple_of(head * PEER_TOPK, PEER_TOPK)
        idx_ref[sub, pl.ds(off, PEER_TOPK), :] = top_i
        gt_ref[sub, pl.ds(off, PEER_TOPK), :] = e / jnp.sum(e, axis=0, keepdims=True)
        return carry

    lax.fori_loop(0, (tb // LANES) * PEER_HEADS, per_head, 0)


def peer_route(x, gain, sc, sh, wq_bf, sk_bf):
    n, d = x.shape
    tb = ROUTE_TB
    nsub = n // LANES
    sc, sc_spec = _row_spec(sc, n, tb, d)
    sh, sh_spec = _row_spec(sh, n, tb, d)
    sel_spec = pl.BlockSpec((tb // LANES, PEER_SEL, LANES), lambda i: (i, 0, 0))
    return pl.pallas_call(
        _peer_route_kernel,
        grid=(n // tb,),
        in_specs=[pl.BlockSpec((tb, d), lambda i: (i, 0)),
                  pl.BlockSpec((1, d), lambda i: (0, 0)),
                  sc_spec, sh_spec,
                  pl.BlockSpec(wq_bf.shape, lambda i: (0, 0)),
                  pl.BlockSpec(sk_bf.shape, lambda i: (0, 0, 0))],
        out_specs=[pl.BlockSpec((tb, d), lambda i: (i, 0)), sel_spec, sel_spec],
        out_shape=[jax.ShapeDtypeStruct((n, d), jnp.float32),
                   jax.ShapeDtypeStruct((nsub, PEER_SEL, LANES), jnp.int32),
                   jax.ShapeDtypeStruct((nsub, PEER_SEL, LANES), jnp.float32)],
        scratch_shapes=[pltpu.VMEM((tb // LANES * PEER_HEADS, LANES, PEER_DKEY), jnp.float32)],
        compiler_params=pltpu.CompilerParams(dimension_semantics=("arbitrary",)),
        name="peer_route",
    )(x, gain, sc, sh, wq_bf, sk_bf)


def _gelu_tanh(x):
    return 0.5 * x * (1.0 + jnp.tanh(math.sqrt(2.0 / math.pi) * (x + 0.044715 * (x * x * x))))


def _peer_experts_kernel(idx_ref, h_ref, gt_ref, xres_ref, gate_ref, tab_ref, o_ref, buf, sem):
    tb, d = h_ref.shape
    nch = d // LANES
    per_token_gate = gate_ref.shape[0] == tb

    def row_copy(e, slot, r):
        src = tab_ref.at[pl.ds(pl.multiple_of(e * (2 * nch), 2 * nch), 2 * nch)]
        return pltpu.make_async_copy(src, buf.at[slot, :, r, :], sem.at[slot])

    def issue(t, slot):
        for r in range(PEER_SEL):
            row_copy(idx_ref[t, r], slot, r).start(priority=r % 2)

    def wait(slot):
        pltpu.make_async_copy(tab_ref.at[pl.ds(0, PEER_SEL * 2 * nch)], buf.at[slot], sem.at[slot]).wait()

    lane = lax.broadcasted_iota(jnp.int32, (PEER_SEL, tb), 1)

    def ffn(t, slot):
        hrow = h_ref[pl.ds(t, 1), :]
        part = buf[slot, 0] * hrow[:, :LANES]
        for c in range(1, nch):
            part = part + buf[slot, c] * hrow[:, c * LANES:(c + 1) * LANES]
        s = jnp.sum(part, axis=-1, keepdims=True)
        g = jnp.sum(jnp.where(lane == t, gt_ref[...], 0.0), axis=-1, keepdims=True)
        w = g * _gelu_tanh(s)
        y = jnp.concatenate([jnp.sum(buf[slot, nch + c] * w, axis=0, keepdims=True) for c in range(nch)], axis=1)
        gate = gate_ref[pl.ds(t, 1), :] if per_token_gate else gate_ref[...]
        return xres_ref[pl.ds(t, 1), :] + gate * y

    ahead = PEER_NSLOT - 1
    for t0 in range(ahead):
        issue(t0, t0)

    def group(t0, last):
        for j in range(PEER_NSLOT):
            t = t0 + j
            wait(j)
            out = ffn(t, j)
            if not last or j == 0:
                issue(t + ahead, (j + ahead) % PEER_NSLOT)
            o_ref[pl.ds(t, 1), :] = out

    def main_body(gi, carry):
        group(pl.multiple_of(gi * PEER_NSLOT, PEER_NSLOT), False)
        return carry

    lax.fori_loop(0, tb // PEER_NSLOT - 1, main_body, 0)
    group(tb - PEER_NSLOT, True)


def peer_experts(h, idx_t, g_t, tab, xres, gate):
    n, d = h.shape
    tb = PEER_TB
    gate, gate_spec = _row_spec(gate, n, tb, d)
    nrow = 2 * d // LANES
    return pl.pallas_call(
        _peer_experts_kernel,
        grid=(n // tb,),
        in_specs=[pl.BlockSpec((None, tb, PEER_SEL), lambda i: (i, 0, 0), memory_space=pltpu.SMEM),
                  pl.BlockSpec((tb, d), lambda i: (i, 0)),
                  pl.BlockSpec((None, PEER_SEL, tb), lambda i: (i, 0, 0)),
                  pl.BlockSpec((tb, d), lambda i: (i, 0)),
                  gate_spec,
                  pl.BlockSpec(memory_space=pl.ANY)],
        out_specs=pl.BlockSpec((tb, d), lambda i: (i, 0)),
        out_shape=jax.ShapeDtypeStruct((n, d), jnp.float32),
        scratch_shapes=[pltpu.VMEM((PEER_NSLOT, nrow, PEER_SEL, LANES), jnp.float32),
                        pltpu.SemaphoreType.DMA((PEER_NSLOT,))],
        compiler_params=pltpu.CompilerParams(dimension_semantics=("arbitrary",)),
        name="peer_experts",
    )(idx_t.transpose(0, 2, 1), h, g_t, xres, gate, tab.reshape(-1, LANES))


def _final_norm_kernel(x_ref, g_ref, o_ref):
    x = x_ref[...]
    o_ref[...] = x * lax.rsqrt(jnp.mean(x * x, axis=-1, keepdims=True) + EPS) * g_ref[...]


def final_norm(x, g):
    n, d = x.shape
    tm = min(ROW_TILE, n)
    return pl.pallas_call(
        _final_norm_kernel,
        grid=(n // tm,),
        in_specs=[pl.BlockSpec((tm, d), lambda i: (i, 0)), pl.BlockSpec((1, d), lambda i: (0, 0))],
        out_specs=pl.BlockSpec((tm, d), lambda i: (i, 0)),
        out_shape=jax.ShapeDtypeStruct((n, d), x.dtype),
        name="final_norm",
    )(x, g.reshape(1, d))


def _per_tile_rows(m, seq_len, tile):
    return m if seq_len % tile == 0 else jnp.repeat(m, seq_len, axis=0)


def trunk_layer(x, seq_len, mod, p, state):
    n, d = x.shape
    bsz = n // seq_len
    sh1, sc1, gt1, sh2, sc2, gt2 = (_per_tile_rows(m, seq_len, ROW_TILE) for m in jnp.split(mod, 6, axis=-1))
    z = mix_in(x, p['g_mix'], sc1, sh1, p['w_in'])
    k_new = z[:, Z_SK:Z_SK + SWA_KVW].reshape(bsz, seq_len, SWA_KVW)
    v_new = z[:, Z_SV:Z_SV + SWA_KVW].reshape(bsz, seq_len, SWA_KVW)
    if state is None:
        win_rows = min(WINDOW, PAST_LEN)
        c0 = jnp.zeros((bsz, MLSTM_HEADS, MLSTM_DH, MLSTM_DH), jnp.float32)
        n0 = jnp.zeros((bsz, MLSTM_HEADS, MLSTM_DH), jnp.float32)
        m0 = jnp.zeros((bsz, MLSTM_HEADS), jnp.float32)
        cbuf = jnp.zeros((bsz, CONV_WIDTH - 1, CONV_W), jnp.float32)
        oa = swa_banded(z, p['sinks'], seq_len)
        k_all, v_all = k_new, v_new
    else:
        kbuf, vbuf, cbuf, c0, n0, m0 = state
        win_rows = kbuf.shape[1]
        k_all = jnp.concatenate([kbuf.reshape(bsz, win_rows, SWA_KVW), k_new], axis=1)
        v_all = jnp.concatenate([vbuf.reshape(bsz, win_rows, SWA_KVW), v_new], axis=1)
        q = z[:, Z_SQ:Z_SQ + SWA_W].reshape(bsz, seq_len, SWA_W)
        oa = swa_cached(q, k_all, v_all, p['sinks']).reshape(n, SWA_W)
    hm, c1, n1, m1 = mlstm_layer(z, p['gate_b'], p['mh_g'], c0, n0, m0, seq_len)
    x, u = mix_out(x, gt1, hm, oa, z, cbuf, p['conv_w'], p['w_out'], seq_len)
    k_keep = k_all[:, -win_rows:].reshape(bsz, win_rows, SWA_KV_HEADS, SWA_DH)
    v_keep = v_all[:, -win_rows:].reshape(bsz, win_rows, SWA_KV_HEADS, SWA_DH)
    cbuf_new = jnp.concatenate([cbuf, u.reshape(bsz, seq_len, CONV_W)], axis=1)[:, -(CONV_WIDTH - 1):]
    h, idx_t, g_t = peer_route(x, p['g_ffn'], sc2, sh2, p['wq'], p['subkeys'])
    x = peer_experts(h, idx_t, g_t, p['tab'], x, gt2)
    return x, (k_keep, v_keep, cbuf_new, c1, n1, m1)


def kernel(x_prompt, x_sample, cache_swa_k, cache_swa_v, state_conv, state_mlstm_C, state_mlstm_n, state_mlstm_m, c_prompt, c_sample, ada_w, ada_b, norm_mix_g, norm_ffn_g, w_in, w_out, mlstm_gate_b, mlstm_norm_g, swa_sinks, conv_w, peer_wq, peer_subkeys, peer_u, peer_v, final_g):
    bp, tp, d = x_prompt.shape
    bs, ts, _ = x_sample.shape
    xp, xs = x_prompt.reshape(bp * tp, d), x_sample.reshape(bs * ts, d)
    c_all = jnp.concatenate([c_prompt, c_sample], axis=0)
    bf = jnp.bfloat16
    n_gate = 2 * MLSTM_HEADS
    new_p, new_s = [], []
    for l in range(DEPTH):
        wl = w_in[l]
        w_perm = jnp.concatenate([wl[:, :Z_SQ], wl[:, Z_SQ + n_gate:], wl[:, Z_SQ:Z_SQ + n_gate],
                                  jnp.zeros((d, LANES - n_gate), wl.dtype)], axis=1)
        p = dict(g_mix=norm_mix_g[l].reshape(1, d), g_ffn=norm_ffn_g[l].reshape(1, d),
                 w_in=w_perm.astype(bf), w_out=w_out[l].astype(bf), gate_b=mlstm_gate_b[l],
                 mh_g=mlstm_norm_g[l], sinks=swa_sinks[l], conv_w=conv_w[l], wq=peer_wq[l].astype(bf),
                 subkeys=peer_subkeys[l].astype(bf), tab=jnp.concatenate([peer_u[l], peer_v[l]], axis=-1))
        mod = adaln_mod(c_all, ada_w[l], ada_b[l])
        xp, sp = trunk_layer(xp, tp, mod[:bp], p, None)
        st = (cache_swa_k[l], cache_swa_v[l], state_conv[l], state_mlstm_C[l], state_mlstm_n[l], state_mlstm_m[l])
        xs, ss = trunk_layer(xs, ts, mod[bp:], p, st)
        new_p.append(sp)
        new_s.append(ss)
    y_prompt = final_norm(xp, final_g).reshape(bp, tp, d)
    y_sample = final_norm(xs, final_g).reshape(bs, ts, d)
    pk, pv, pc, pC, pn, pm = [jnp.stack(t) for t in zip(*new_p)]
    sk, sv, sc, sC, sn, sm = [jnp.stack(t) for t in zip(*new_s)]
    return (y_prompt, y_sample, pk, pv, pc, pC, pn, pm, sk, sv, sc, sC, sn, sm)
```

```python
import functools
import math

import jax
import jax.numpy as jnp
from jax import lax
from jax.experimental import pallas as pl
from jax.experimental.pallas import tpu as pltpu

D_MODEL = 1024
DEPTH = 2
PAST_LEN = 16384

MLSTM_W = D_MODEL // 2
MLSTM_HEADS = 4
MLSTM_DH = MLSTM_W // MLSTM_HEADS
MLSTM_CHUNK = 64
SWA_W = D_MODEL // 4
SWA_DH = 64
SWA_HEADS = SWA_W // SWA_DH
SWA_KV_HEADS = SWA_HEADS // 2
SWA_GROUP = SWA_HEADS // SWA_KV_HEADS
SWA_KVW = SWA_KV_HEADS * SWA_DH
WINDOW = 128
CONV_W = D_MODEL - MLSTM_W - SWA_W
CONV_WIDTH = 3
PEER_HEADS = 8
PEER_NKEYS = 128
PEER_DKEY = 256
HALF_KEY = PEER_DKEY // 2
PEER_TOPK = 16
PEER_SEL = PEER_HEADS * PEER_TOPK
EPS = 1e-6

LANES = 128
SUBLANES = 8
ROW_TILE = 512
ROUTE_TB = 256
PEER_TB = 128
PEER_NSLOT = 8

Z_Q, Z_K, Z_V, Z_O = 0, MLSTM_W, 2 * MLSTM_W, 3 * MLSTM_W
Z_SQ = 4 * MLSTM_W
Z_SK = Z_SQ + SWA_W
Z_SV = Z_SK + SWA_KVW
Z_CB = Z_SV + SWA_KVW
Z_CC = Z_CB + CONV_W
Z_CH = Z_CC + CONV_W
Z_GATE = Z_CH + CONV_W
Z_W = Z_GATE + LANES
ALIBI = tuple(2.0 ** (-8.0 * h / SWA_HEADS) for h in range(1, SWA_HEADS + 1))


def _bdot(a, b, dims):
    return lax.dot_general(a.astype(jnp.bfloat16), b.astype(jnp.bfloat16), (dims, ((), ())),
                           preferred_element_type=jnp.float32)


def _row_spec(arr, n, tm, d):
    g = arr.shape[0]
    if g == n:
        return arr, pl.BlockSpec((tm, d), lambda i, *_: (i, 0))
    per = n // g
    return arr[:, None, :], pl.BlockSpec((None, 1, d), lambda i, *_: (i * tm // per, 0, 0))


def _norm_mod(x, gain, sc, sh):
    y = x * lax.rsqrt(jnp.mean(x * x, axis=-1, keepdims=True) + EPS) * gain
    return y * (1.0 + sc) + sh


def _adaln_kernel(c_ref, w_ref, b_ref, o_ref):
    c = c_ref[...]
    o_ref[...] = _bdot(c * jax.nn.sigmoid(c), w_ref[...], ((1,), (0,))) + b_ref[...]


def adaln_mod(c, w, b):
    bsz, d = c.shape
    m = w.shape[1]
    tn = 512
    return pl.pallas_call(
        _adaln_kernel,
        grid=(m // tn,),
        in_specs=[pl.BlockSpec((bsz, d), lambda j: (0, 0)),
                  pl.BlockSpec((d, tn), lambda j: (0, j)),
                  pl.BlockSpec((1, tn), lambda j: (0, j))],
        out_specs=pl.BlockSpec((bsz, tn), lambda j: (0, j)),
        out_shape=jax.ShapeDtypeStruct((bsz, m), jnp.float32),
        name="adaln_mod",
    )(c, w, b.reshape(1, m))


def _mix_in_kernel(x_ref, gain_ref, sc_ref, sh_ref, w_ref, z_ref):
    h = _norm_mod(x_ref[...], gain_ref[...], sc_ref[...], sh_ref[...])
    z_ref[...] = jnp.dot(h.astype(jnp.bfloat16), w_ref[...], preferred_element_type=jnp.float32)


def mix_in(x, gain, sc, sh, w_bf):
    n, d = x.shape
    zw = w_bf.shape[1]
    tm, tn = min(ROW_TILE, n), zw // 3
    sc, sc_spec = _row_spec(sc, n, tm, d)
    sh, sh_spec = _row_spec(sh, n, tm, d)
    return pl.pallas_call(
        _mix_in_kernel,
        grid=(n // tm, zw // tn),
        in_specs=[pl.BlockSpec((tm, d), lambda i, j: (i, 0)),
                  pl.BlockSpec((1, d), lambda i, j: (0, 0)),
                  sc_spec, sh_spec,
                  pl.BlockSpec((d, tn), lambda i, j: (0, j))],
        out_specs=pl.BlockSpec((tm, tn), lambda i, j: (i, j)),
        out_shape=jax.ShapeDtypeStruct((n, zw), jnp.float32),
        name="mix_in",
    )(x, gain, sc, sh, w_bf)


def _log_sigmoid(x):
    return jnp.minimum(x, 0.0) - jnp.log(1.0 + jnp.exp(-jnp.abs(x)))


def _mlstm_kernel(chunk, valid, q_ref, k_ref, v_ref, o_ref, gate_ref, gb_ref, g_ref, c0_ref, n0_ref, m0_ref,
                  hm_ref, c_ref, n_ref, m_ref):
    tc = q_ref.shape[0]
    nh, dh, L = MLSTM_HEADS, MLSTM_DH, chunk

    @pl.when(pl.program_id(1) == 0)
    def _():
        c_ref[...] = c0_ref[...]
        n_ref[...] = n0_ref[...]
        m_ref[...] = m0_ref[...]

    row = lax.broadcasted_iota(jnp.int32, (L, L), 0)
    col = lax.broadcasted_iota(jnp.int32, (L, L), 1)
    eye = row == col
    visible = (col <= row) & (col < valid)
    rcol = lax.broadcasted_iota(jnp.int32, (L, 1), 0)
    gb = gb_ref[...]
    gain = g_ref[...]

    def to_row(x_col):
        return jnp.sum(jnp.where(eye, x_col, 0.0), axis=0, keepdims=True)

    def one_chunk(ci, carry):
        r0 = pl.multiple_of(ci * L, L)
        pre = gate_ref[pl.ds(r0, L), :] + gb
        for hd in range(nh):
            sl = slice(hd * dh, (hd + 1) * dh)
            q = q_ref[pl.ds(r0, L), sl]
            k = k_ref[pl.ds(r0, L), sl] * (dh ** -0.5)
            v = v_ref[pl.ds(r0, L), sl]
            ig = pre[:, hd:hd + 1]
            lf = jnp.where(rcol < valid, _log_sigmoid(pre[:, nh + hd:nh + hd + 1]), 0.0)
            b = lf
            s = 1
            while s < L:
                b = b + jnp.where(rcol >= s, pltpu.roll(b, s, 0), 0.0)
                s *= 2
            cmat, nrow, m_prev = c_ref[hd], n_ref[hd], m_ref[hd][:, 0:1]
            a = b + m_prev
            d = jnp.where(visible, b - to_row(b) + to_row(ig), -jnp.inf)
            m_t = jnp.maximum(a, jnp.max(d, axis=-1, keepdims=True))
            w_inter = jnp.exp(a - m_t)
            w_intra = jnp.exp(d - m_t)
            qk = _bdot(q, k, ((1,), (1,))) * w_intra
            num = w_inter * _bdot(q, cmat, ((1,), (1,))) + _bdot(qk, v, ((1,), (0,)))
            den = w_inter * jnp.sum(q * nrow, axis=-1, keepdims=True) + jnp.sum(qk, axis=-1, keepdims=True)
            h = num / jnp.maximum(jnp.abs(den), jnp.exp(-m_t))
            hn = h * lax.rsqrt(jnp.mean(h * h, axis=-1, keepdims=True) + EPS) * gain[:, sl]
            hm_ref[pl.ds(r0, L), sl] = hn * jax.nn.sigmoid(o_ref[pl.ds(r0, L), sl])
            m_last = m_t[L - 1:L]
            wl_inter = w_inter[L - 1:L]
            wl = jnp.where(rcol < valid, jnp.exp(b[L - 1:L] - b + ig - m_last), 0.0)
            c_ref[hd] = wl_inter * cmat + _bdot(wl * v, k, ((0,), (0,)))
            n_ref[hd] = wl_inter * nrow + jnp.sum(wl * k, axis=0, keepdims=True)
            m_ref[hd] = jnp.broadcast_to(m_last, (1, dh))
        return carry

    lax.fori_loop(0, tc // L, one_chunk, 0)


def mlstm_layer(z, gate_b, mh_g, c0, n0, m0, seq_len):
    bsz = z.shape[0] // seq_len
    nh, dh = MLSTM_HEADS, MLSTM_DH
    chunk = math.gcd(seq_len, MLSTM_CHUNK)
    valid, padded = chunk, seq_len
    if chunk % SUBLANES:
        assert seq_len < SUBLANES
        chunk = padded = SUBLANES
        z = jnp.pad(z.reshape(bsz, seq_len, -1), ((0, 0), (0, padded - seq_len), (0, 0))).reshape(bsz * padded, -1)
    n = bsz * padded
    tc = min(padded, ROW_TILE)
    steps = padded // tc
    w = nh * dh
    gb = jnp.zeros((1, LANES), jnp.float32).at[0, :2 * nh].set(gate_b)

    def zcol(off, width):
        return pl.BlockSpec((tc, width), lambda b, c: (b * steps + c, off // width))

    def state(shape):
        return pl.BlockSpec((None,) + shape, lambda b, c: (b,) + (0,) * len(shape))

    hm, c1, n1, m1 = pl.pallas_call(
        functools.partial(_mlstm_kernel, chunk, valid),
        grid=(bsz, steps),
        in_specs=[zcol(Z_Q, w), zcol(Z_K, w), zcol(Z_V, w), zcol(Z_O, w), zcol(Z_GATE, LANES),
                  pl.BlockSpec((1, LANES), lambda b, c: (0, 0)),
                  pl.BlockSpec((1, w), lambda b, c: (0, 0)),
                  state((nh, dh, dh)), state((nh, 1, dh)), state((nh, 1, dh))],
        out_specs=[pl.BlockSpec((tc, w), lambda b, c: (b * steps + c, 0)),
                   state((nh, dh, dh)), state((nh, 1, dh)), state((nh, 1, dh))],
        out_shape=[jax.ShapeDtypeStruct((n, w), jnp.float32),
                   jax.ShapeDtypeStruct((bsz, nh, dh, dh), jnp.float32),
                   jax.ShapeDtypeStruct((bsz, nh, 1, dh), jnp.float32),
                   jax.ShapeDtypeStruct((bsz, nh, 1, dh), jnp.float32)],
        compiler_params=pltpu.CompilerParams(dimension_semantics=("arbitrary", "arbitrary")),
        name="mlstm",
    )(z, z, z, z, z, gb, mh_g.reshape(1, w), c0, n0[:, :, None, :],
      jnp.broadcast_to(m0[:, :, None, None], (bsz, nh, 1, dh)))
    if padded != seq_len:
        hm = hm.reshape(bsz, padded, w)[:, :seq_len].reshape(bsz * seq_len, w)
    return hm, c1, n1[:, :, 0, :], m1[:, :, 0, 0]


def _attend(q, kb, vb, dist, visible, slope, sink):
    s = _bdot(q, kb, ((1,), (1,))) * (SWA_DH ** -0.5) - slope * dist.astype(jnp.float32)
    s = jnp.where(visible, s, -jnp.inf)
    mx = jnp.maximum(jnp.max(s, axis=-1, keepdims=True), sink)
    e = jnp.exp(s - mx)
    p = e / (jnp.sum(e, axis=-1, keepdims=True) + jnp.exp(sink - mx))
    return _bdot(p, vb, ((1,), (0,)))


def _swa_banded_kernel(q_ref, kp_ref, kc_ref, vp_ref, vc_ref, sink_ref, o_ref):
    j = pl.program_id(1)
    w = q_ref.shape[0]
    t = lax.broadcasted_iota(jnp.int32, (w, 2 * w), 0)
    i = lax.broadcasted_iota(jnp.int32, (w, 2 * w), 1)
    dist = t + w - i
    visible = (dist >= 0) & (dist <= WINDOW) & ((j > 0) | (i >= w))
    sinks = sink_ref[...]
    for kv in range(SWA_KV_HEADS):
        ks = slice(kv * SWA_DH, (kv + 1) * SWA_DH)
        kb = jnp.concatenate([kp_ref[:, ks], kc_ref[:, ks]], axis=0)
        vb = jnp.concatenate([vp_ref[:, ks], vc_ref[:, ks]], axis=0)
        for g in range(SWA_GROUP):
            h = kv * SWA_GROUP + g
            hs = slice(h * SWA_DH, (h + 1) * SWA_DH)
            o_ref[:, hs] = _attend(q_ref[:, hs], kb, vb, dist, visible, ALIBI[h], sinks[:, h:h + 1])


def swa_banded(z, sinks, seq_len):
    n = z.shape[0]
    w = WINDOW
    nb = seq_len // w
    sink_row = jnp.zeros((1, LANES), jnp.float32).at[0, :SWA_HEADS].set(sinks)

    def cur(off, width):
        return pl.BlockSpec((w, width), lambda b, j: (b * nb + j, off // width))

    def prev(off, width):
        return pl.BlockSpec((w, width), lambda b, j: (b * nb + jnp.maximum(j - 1, 0), off // width))

    return pl.pallas_call(
        _swa_banded_kernel,
        grid=(n // seq_len, nb),
        in_specs=[cur(Z_SQ, SWA_W), prev(Z_SK, SWA_KVW), cur(Z_SK, SWA_KVW), prev(Z_SV, SWA_KVW),
                  cur(Z_SV, SWA_KVW), pl.BlockSpec((1, LANES), lambda b, j: (0, 0))],
        out_specs=pl.BlockSpec((w, SWA_W), lambda b, j: (b * nb + j, 0)),
        out_shape=jax.ShapeDtypeStruct((n, SWA_W), jnp.float32),
        name="swa_banded",
    )(z, z, z, z, z, sink_row)


def _swa_cached_kernel(t_new, n_keys, q_ref, k_ref, v_ref, sink_ref, o_ref):
    m = q_ref.shape[1]
    nkp = k_ref.shape[0]
    r = lax.broadcasted_iota(jnp.int32, (m, nkp), 0)
    i = lax.broadcasted_iota(jnp.int32, (m, nkp), 1)
    first_key_pos = PAST_LEN - (n_keys - t_new)
    dist = (n_keys - t_new) + r % t_new - i
    visible = (dist >= 0) & (dist <= WINDOW) & (i < n_keys) & (first_key_pos + i >= 0)
    rg = lax.broadcasted_iota(jnp.int32, (m, 1), 0) // t_new
    sinks = sink_ref[...]
    for kv in range(SWA_KV_HEADS):
        ks = slice(kv * SWA_DH, (kv + 1) * SWA_DH)
        h0 = kv * SWA_GROUP
        slope = jnp.where(rg == 0, ALIBI[h0], ALIBI[h0 + 1])
        sink = jnp.where(rg == 0, sinks[:, h0:h0 + 1], sinks[:, h0 + 1:h0 + 2])
        o_ref[kv] = _attend(q_ref[kv], k_ref[:, ks], v_ref[:, ks], dist, visible, slope, sink)


def swa_cached(q, k_all, v_all, sinks):
    assert SWA_GROUP == 2
    bsz, t_new, _ = q.shape
    n_keys = k_all.shape[1]
    nkp = -(-n_keys // SUBLANES) * SUBLANES
    pad = ((0, 0), (0, nkp - n_keys), (0, 0))
    k_all, v_all = jnp.pad(k_all, pad), jnp.pad(v_all, pad)
    m = SWA_GROUP * t_new
    qs = q.reshape(bsz, t_new, SWA_KV_HEADS, SWA_GROUP, SWA_DH).transpose(0, 2, 3, 1, 4)
    qs = qs.reshape(bsz, SWA_KV_HEADS, m, SWA_DH)
    sink_row = jnp.zeros((1, LANES), jnp.float32).at[0, :SWA_HEADS].set(sinks)
    o = pl.pallas_call(
        functools.partial(_swa_cached_kernel, t_new, n_keys),
        grid=(bsz,),
        in_specs=[pl.BlockSpec((None, SWA_KV_HEADS, m, SWA_DH), lambda b: (b, 0, 0, 0)),
                  pl.BlockSpec((None, nkp, SWA_KVW), lambda b: (b, 0, 0)),
                  pl.BlockSpec((None, nkp, SWA_KVW), lambda b: (b, 0, 0)),
                  pl.BlockSpec((1, LANES), lambda b: (0, 0))],
        out_specs=pl.BlockSpec((None, SWA_KV_HEADS, m, SWA_DH), lambda b: (b, 0, 0, 0)),
        out_shape=jax.ShapeDtypeStruct((bsz, SWA_KV_HEADS, m, SWA_DH), jnp.float32),
        name="swa_cached",
    )(qs, k_all, v_all, sink_row)
    o = o.reshape(bsz, SWA_KV_HEADS, SWA_GROUP, t_new, SWA_DH).transpose(0, 3, 1, 2, 4)
    return o.reshape(bsz, t_new, SWA_W)


def _mix_out_kernel(seq_len, x_ref, gate_ref, hm_ref, oa_ref, cb_ref, cc_ref, ch_ref, hcc_ref, hch_ref,
                    pa_ref, pb_ref, cw_ref, w_ref, o_ref, u_ref):
    i = pl.program_id(0)
    tm = x_ref.shape[0]
    u = cc_ref[...] * ch_ref[...]
    u_ref[...] = u
    hu = hcc_ref[...] * hch_ref[...]
    r = lax.broadcasted_iota(jnp.int32, u.shape, 0)
    p = (i * tm + r) % seq_len
    pa, pb = pa_ref[...], pb_ref[...]
    last, last2 = hu[SUBLANES - 1:SUBLANES], hu[SUBLANES - 2:SUBLANES - 1]
    u1 = jnp.where(r >= 1, pltpu.roll(u, 1, 0), last)
    u2 = jnp.where(r >= 2, pltpu.roll(u, 2, 0), jnp.where(r == 1, last, last2))
    s1 = jnp.where(p >= 1, u1, pa)
    s2 = jnp.where(p >= 2, u2, jnp.where(p == 1, pa, pb))
    cw = cw_ref[...]
    yc = cb_ref[...] * (cw[0:1] * s2 + cw[1:2] * s1 + cw[2:3] * u)
    cat = jnp.concatenate([hm_ref[...], oa_ref[...], yc], axis=-1)
    mix = jnp.dot(cat.astype(jnp.bfloat16), w_ref[...], preferred_element_type=jnp.float32)
    o_ref[...] = x_ref[...] + gate_ref[...] * mix


def mix_out(x, gate, hm, oa, z, conv_prev, conv_w, w_out_bf, seq_len):
    n, d = x.shape
    c = CONV_W
    tm = min(ROW_TILE, n)
    gate, gate_spec = _row_spec(gate, n, tm, d)
    if seq_len % tm:
        pa = jnp.repeat(conv_prev[:, 1], seq_len, axis=0)
        pb = jnp.repeat(conv_prev[:, 0], seq_len, axis=0)
    else:
        pa, pb = conv_prev[:, 1], conv_prev[:, 0]
    pa, pa_spec = _row_spec(pa, n, tm, c)
    pb, pb_spec = _row_spec(pb, n, tm, c)
    cw = jnp.zeros((SUBLANES, c), jnp.float32).at[:CONV_WIDTH].set(conv_w)

    def zcol(off, width):
        return pl.BlockSpec((tm, width), lambda i: (i, off // width))

    def zhalo(off):
        return pl.BlockSpec((SUBLANES, c), lambda i: (jnp.maximum(i * (tm // SUBLANES) - 1, 0), off // c))

    return pl.pallas_call(
        functools.partial(_mix_out_kernel, seq_len),
        grid=(n // tm,),
        in_specs=[pl.BlockSpec((tm, d), lambda i: (i, 0)), gate_spec,
                  pl.BlockSpec((tm, MLSTM_W), lambda i: (i, 0)),
                  pl.BlockSpec((tm, SWA_W), lambda i: (i, 0)),
                  zcol(Z_CB, c), zcol(Z_CC, c), zcol(Z_CH, c), zhalo(Z_CC), zhalo(Z_CH),
                  pa_spec, pb_spec,
                  pl.BlockSpec((SUBLANES, c), lambda i: (0, 0)),
                  pl.BlockSpec((d, d), lambda i: (0, 0))],
        out_specs=[pl.BlockSpec((tm, d), lambda i: (i, 0)), pl.BlockSpec((tm, c), lambda i: (i, 0))],
        out_shape=[jax.ShapeDtypeStruct((n, d), jnp.float32), jax.ShapeDtypeStruct((n, c), jnp.float32)],
        name="mix_out",
    )(x, gate, hm, oa, z, z, z, z, z, pa, pb, cw, w_out_bf)


def _top_rows(s, k, payload=None):
    rows = lax.broadcasted_iota(jnp.int32, s.shape, 0)
    big = jnp.int32(s.shape[0])
    vals, ids = [], []
    for _ in range(k):
        m = jnp.max(s, axis=0, keepdims=True)
        r = jnp.min(jnp.where(s == m, rows, big), axis=0, keepdims=True)
        hit = rows == r
        vals.append(m)
        ids.append(r if payload is None else jnp.max(jnp.where(hit, payload, -1), axis=0, keepdims=True))
        s = jnp.where(hit, -jnp.inf, s)
    return jnp.concatenate(vals, axis=0), jnp.concatenate(ids, axis=0)


def _peer_route_kernel(x_ref, gain_ref, sc_ref, sh_ref, wq_ref, sk_ref, h_ref, idx_ref, gt_ref, q_scr):
    tb = x_ref.shape[0]
    h = _norm_mod(x_ref[...], gain_ref[...], sc_ref[...], sh_ref[...])
    h_ref[...] = h
    q = jnp.dot(h.astype(jnp.bfloat16), wq_ref[...], preferred_element_type=jnp.float32)
    for sub in range(tb // LANES):
        for head in range(PEER_HEADS):
            q_scr[sub * PEER_HEADS + head] = q[sub * LANES:(sub + 1) * LANES, head * PEER_DKEY:(head + 1) * PEER_DKEY]

    def per_head(j, carry):
        sub = j // PEER_HEADS
        head = j % PEER_HEADS
        qh = q_scr[j].astype(jnp.bfloat16)
        tops = []
        for p in range(2):
            st = lax.dot_general(sk_ref[p], qh[:, p * HALF_KEY:(p + 1) * HALF_KEY],
                                 (((1,), (1,)), ((), ())), preferred_element_type=jnp.float32)
            tops.append(_top_rows(st, PEER_TOPK))
        (s1, i1), (s2, i2) = tops
        cand = jnp.concatenate([s1[a:a + 1] + s2 for a in range(PEER_TOPK)], axis=0)
        cidx = jnp.concatenate([i1[a:a + 1] * PEER_NKEYS + i2 for a in range(PEER_TOPK)], axis=0)
        top_s, top_i = _top_rows(cand, PEER_TOPK, payload=cidx)
        e = jnp.exp(top_s - jnp.max(top_s, axis=0, keepdims=True))
        off = pl.multiple_of(head * PEER_TOPK, PEER_TOPK)
        idx_ref[sub, pl.ds(off, PEER_TOPK), :] = top_i
        gt_ref[sub, pl.ds(off, PEER_TOPK), :] = e / jnp.sum(e, axis=0, keepdims=True)
        return carry

    lax.fori_loop(0, (tb // LANES) * PEER_HEADS, per_head, 0)


def peer_route(x, gain, sc, sh, wq_bf, sk_bf):
    n, d = x.shape
    tb = ROUTE_TB
    nsub = n // LANES
    sc, sc_spec = _row_spec(sc, n, tb, d)
    sh, sh_spec = _row_spec(sh, n, tb, d)
    sel_spec = pl.BlockSpec((tb // LANES, PEER_SEL, LANES), lambda i: (i, 0, 0))
    return pl.pallas_call(
        _peer_route_kernel,
        grid=(n // tb,),
        in_specs=[pl.BlockSpec((tb, d), lambda i: (i, 0)),
                  pl.BlockSpec((1, d), lambda i: (0, 0)),
                  sc_spec, sh_spec,
                  pl.BlockSpec(wq_bf.shape, lambda i: (0, 0)),
                  pl.BlockSpec(sk_bf.shape, lambda i: (0, 0, 0))],
        out_specs=[pl.BlockSpec((tb, d), lambda i: (i, 0)), sel_spec, sel_spec],
        out_shape=[jax.ShapeDtypeStruct((n, d), jnp.float32),
                   jax.ShapeDtypeStruct((nsub, PEER_SEL, LANES), jnp.int32),
                   jax.ShapeDtypeStruct((nsub, PEER_SEL, LANES), jnp.float32)],
        scratch_shapes=[pltpu.VMEM((tb // LANES * PEER_HEADS, LANES, PEER_DKEY), jnp.float32)],
        compiler_params=pltpu.CompilerParams(dimension_semantics=("arbitrary",)),
        name="peer_route",
    )(x, gain, sc, sh, wq_bf, sk_bf)


def _gelu_tanh(x):
    return 0.5 * x * (1.0 + jnp.tanh(math.sqrt(2.0 / math.pi) * (x + 0.044715 * (x * x * x))))


def _peer_experts_kernel(idx_ref, h_ref, gt_ref, xres_ref, gate_ref, tab_ref, o_ref, buf, sem):
    tb, d = h_ref.shape
    nch = d // LANES
    per_token_gate = gate_ref.shape[0] == tb

    def row_copy(e, slot, r):
        src = tab_ref.at[pl.ds(pl.multiple_of(e * nch, nch), nch)]
        return pltpu.make_async_copy(src, buf.at[slot, :, r, :], sem.at[slot])

    def issue(t, slot):
        for r in range(PEER_SEL):
            row_copy(idx_ref[t, r], slot, r).start(priority=r % 2)

    def wait(slot):
        pltpu.make_async_copy(tab_ref.at[pl.ds(0, PEER_SEL * nch)], buf.at[slot], sem.at[slot]).wait()

    lane = lax.broadcasted_iota(jnp.int32, (PEER_SEL, tb), 1)

    def ffn(t, slot):
        hrow = h_ref[pl.ds(t, 1), :]
        part = None
        for c in range(nch):
            u = lax.bitcast_convert_type(buf[slot, c] & jnp.uint32(0xFFFF0000), jnp.float32)
            term = u * hrow[:, c * LANES:(c + 1) * LANES]
            part = term if part is None else part + term
        s = jnp.sum(part, axis=-1, keepdims=True)
        g = jnp.sum(jnp.where(lane == t, gt_ref[...], 0.0), axis=-1, keepdims=True)
        w = g * _gelu_tanh(s)
        y = jnp.concatenate(
            [jnp.sum(lax.bitcast_convert_type(buf[slot, c] << 16, jnp.float32) * w, axis=0, keepdims=True)
             for c in range(nch)], axis=1)
        gate = gate_ref[pl.ds(t, 1), :] if per_token_gate else gate_ref[...]
        return xres_ref[pl.ds(t, 1), :] + gate * y

    ahead = PEER_NSLOT - 1
    for t0 in range(ahead):
        issue(t0, t0)

    def group(t0, last):
        for j in range(PEER_NSLOT):
            t = t0 + j
            wait(j)
            out = ffn(t, j)
            if not last or j == 0:
                issue(t + ahead, (j + ahead) % PEER_NSLOT)
            o_ref[pl.ds(t, 1), :] = out

    def main_body(gi, carry):
        group(pl.multiple_of(gi * PEER_NSLOT, PEER_NSLOT), False)
        return carry

    lax.fori_loop(0, tb // PEER_NSLOT - 1, main_body, 0)
    group(tb - PEER_NSLOT, True)


def pack_expert_table(u, v):
    ub = lax.bitcast_convert_type(u.astype(jnp.bfloat16), jnp.uint16).astype(jnp.uint32)
    vb = lax.bitcast_convert_type(v.astype(jnp.bfloat16), jnp.uint16).astype(jnp.uint32)
    return (ub << 16) | vb


def peer_experts(h, idx_t, g_t, tab, xres, gate):
    n, d = h.shape
    tb = PEER_TB
    gate, gate_spec = _row_spec(gate, n, tb, d)
    nrow = d // LANES
    return pl.pallas_call(
        _peer_experts_kernel,
        grid=(n // tb,),
        in_specs=[pl.BlockSpec((None, tb, PEER_SEL), lambda i: (i, 0, 0), memory_space=pltpu.SMEM),
                  pl.BlockSpec((tb, d), lambda i: (i, 0)),
                  pl.BlockSpec((None, PEER_SEL, tb), lambda i: (i, 0, 0)),
                  pl.BlockSpec((tb, d), lambda i: (i, 0)),
                  gate_spec,
                  pl.BlockSpec(memory_space=pl.ANY)],
        out_specs=pl.BlockSpec((tb, d), lambda i: (i, 0)),
        out_shape=jax.ShapeDtypeStruct((n, d), jnp.float32),
        scratch_shapes=[pltpu.VMEM((PEER_NSLOT, nrow, PEER_SEL, LANES), jnp.uint32),
                        pltpu.SemaphoreType.DMA((PEER_NSLOT,))],
        compiler_params=pltpu.CompilerParams(dimension_semantics=("arbitrary",)),
        name="peer_experts",
    )(idx_t.transpose(0, 2, 1), h, g_t, xres, gate, tab.reshape(-1, LANES))


def _final_norm_kernel(x_ref, g_ref, o_ref):
    x = x_ref[...]
    o_ref[...] = x * lax.rsqrt(jnp.mean(x * x, axis=-1, keepdims=True) + EPS) * g_ref[...]


def final_norm(x, g):
    n, d = x.shape
    tm = min(ROW_TILE, n)
    return pl.pallas_call(
        _final_norm_kernel,
        grid=(n // tm,),
        in_specs=[pl.BlockSpec((tm, d), lambda i: (i, 0)), pl.BlockSpec((1, d), lambda i: (0, 0))],
        out_specs=pl.BlockSpec((tm, d), lambda i: (i, 0)),
        out_shape=jax.ShapeDtypeStruct((n, d), x.dtype),
        name="final_norm",
    )(x, g.reshape(1, d))


def _per_tile_rows(m, seq_len, tile):
    return m if seq_len % tile == 0 else jnp.repeat(m, seq_len, axis=0)


def trunk_layer(x, seq_len, mod, p, state):
    n, d = x.shape
    bsz = n // seq_len
    sh1, sc1, gt1, sh2, sc2, gt2 = (_per_tile_rows(m, seq_len, ROW_TILE) for m in jnp.split(mod, 6, axis=-1))
    z = mix_in(x, p['g_mix'], sc1, sh1, p['w_in'])
    k_new = z[:, Z_SK:Z_SK + SWA_KVW].reshape(bsz, seq_len, SWA_KVW)
    v_new = z[:, Z_SV:Z_SV + SWA_KVW].reshape(bsz, seq_len, SWA_KVW)
    if state is None:
        win_rows = min(WINDOW, PAST_LEN)
        c0 = jnp.zeros((bsz, MLSTM_HEADS, MLSTM_DH, MLSTM_DH), jnp.float32)
        n0 = jnp.zeros((bsz, MLSTM_HEADS, MLSTM_DH), jnp.float32)
        m0 = jnp.zeros((bsz, MLSTM_HEADS), jnp.float32)
        cbuf = jnp.zeros((bsz, CONV_WIDTH - 1, CONV_W), jnp.float32)
        oa = swa_banded(z, p['sinks'], seq_len)
        k_all, v_all = k_new, v_new
    else:
        kbuf, vbuf, cbuf, c0, n0, m0 = state
        win_rows = kbuf.shape[1]
        k_all = jnp.concatenate([kbuf.reshape(bsz, win_rows, SWA_KVW), k_new], axis=1)
        v_all = jnp.concatenate([vbuf.reshape(bsz, win_rows, SWA_KVW), v_new], axis=1)
        q = z[:, Z_SQ:Z_SQ + SWA_W].reshape(bsz, seq_len, SWA_W)
        oa = swa_cached(q, k_all, v_all, p['sinks']).reshape(n, SWA_W)
    hm, c1, n1, m1 = mlstm_layer(z, p['gate_b'], p['mh_g'], c0, n0, m0, seq_len)
    x, u = mix_out(x, gt1, hm, oa, z, cbuf, p['conv_w'], p['w_out'], seq_len)
    k_keep = k_all[:, -win_rows:].reshape(bsz, win_rows, SWA_KV_HEADS, SWA_DH)
    v_keep = v_all[:, -win_rows:].reshape(bsz, win_rows, SWA_KV_HEADS, SWA_DH)
    cbuf_new = jnp.concatenate([cbuf, u.reshape(bsz, seq_len, CONV_W)], axis=1)[:, -(CONV_WIDTH - 1):]
    h, idx_t, g_t = peer_route(x, p['g_ffn'], sc2, sh2, p['wq'], p['subkeys'])
    x = peer_experts(h, idx_t, g_t, p['tab'], x, gt2)
    return x, (k_keep, v_keep, cbuf_new, c1, n1, m1)


def kernel(x_prompt, x_sample, cache_swa_k, cache_swa_v, state_conv, state_mlstm_C, state_mlstm_n, state_mlstm_m, c_prompt, c_sample, ada_w, ada_b, norm_mix_g, norm_ffn_g, w_in, w_out, mlstm_gate_b, mlstm_norm_g, swa_sinks, conv_w, peer_wq, peer_subkeys, peer_u, peer_v, final_g):
    bp, tp, d = x_prompt.shape
    bs, ts, _ = x_sample.shape
    xp, xs = x_prompt.reshape(bp * tp, d), x_sample.reshape(bs * ts, d)
    c_all = jnp.concatenate([c_prompt, c_sample], axis=0)
    bf = jnp.bfloat16
    n_gate = 2 * MLSTM_HEADS
    new_p, new_s = [], []
    for l in range(DEPTH):
        wl = w_in[l]
        w_perm = jnp.concatenate([wl[:, :Z_SQ], wl[:, Z_SQ + n_gate:], wl[:, Z_SQ:Z_SQ + n_gate],
                                  jnp.zeros((d, LANES - n_gate), wl.dtype)], axis=1)
        p = dict(g_mix=norm_mix_g[l].reshape(1, d), g_ffn=norm_ffn_g[l].reshape(1, d),
                 w_in=w_perm.astype(bf), w_out=w_out[l].astype(bf), gate_b=mlstm_gate_b[l],
                 mh_g=mlstm_norm_g[l], sinks=swa_sinks[l], conv_w=conv_w[l], wq=peer_wq[l].astype(bf),
                 subkeys=peer_subkeys[l].astype(bf), tab=pack_expert_table(peer_u[l], peer_v[l]))
        mod = adaln_mod(c_all, ada_w[l], ada_b[l])
        xp, sp = trunk_layer(xp, tp, mod[:bp], p, None)
        st = (cache_swa_k[l], cache_swa_v[l], state_conv[l], state_mlstm_C[l], state_mlstm_n[l], state_mlstm_m[l])
        xs, ss = trunk_layer(xs, ts, mod[bp:], p, st)
        new_p.append(sp)
        new_s.append(ss)
    y_prompt = final_norm(xp, final_g).reshape(bp, tp, d)
    y_sample = final_norm(xs, final_g).reshape(bs, ts, d)
    pk, pv, pc, pC, pn, pm = [jnp.stack(t) for t in zip(*new_p)]
    sk, sv, sc, sC, sn, sm = [jnp.stack(t) for t in zip(*new_s)]
    return (y_prompt, y_sample, pk, pv, pc, pC, pn, pm, sk, sv, sc, sC, sn, sm)
```

```python
import functools
import math

import jax
import jax.numpy as jnp
from jax import lax
from jax.experimental import pallas as pl
from jax.experimental.pallas import tpu as pltpu

D_MODEL = 1024
DEPTH = 2
PAST_LEN = 16384

MLSTM_W = D_MODEL // 2
MLSTM_HEADS = 4
MLSTM_DH = MLSTM_W // MLSTM_HEADS
MLSTM_CHUNK = 64
SWA_W = D_MODEL // 4
SWA_DH = 64
SWA_HEADS = SWA_W // SWA_DH
SWA_KV_HEADS = SWA_HEADS // 2
SWA_GROUP = SWA_HEADS // SWA_KV_HEADS
SWA_KVW = SWA_KV_HEADS * SWA_DH
WINDOW = 128
CONV_W = D_MODEL - MLSTM_W - SWA_W
CONV_WIDTH = 3
PEER_HEADS = 8
PEER_NKEYS = 128
PEER_DKEY = 256
HALF_KEY = PEER_DKEY // 2
PEER_TOPK = 16
PEER_SEL = PEER_HEADS * PEER_TOPK
EPS = 1e-6

LANES = 128
SUBLANES = 8
ROW_TILE = 512
ROUTE_TB = 256
PEER_TB = 128
PEER_NSLOT = 8

Z_Q, Z_K, Z_V, Z_O = 0, MLSTM_W, 2 * MLSTM_W, 3 * MLSTM_W
Z_SQ = 4 * MLSTM_W
Z_SK = Z_SQ + SWA_W
Z_SV = Z_SK + SWA_KVW
Z_CB = Z_SV + SWA_KVW
Z_CC = Z_CB + CONV_W
Z_CH = Z_CC + CONV_W
Z_GATE = Z_CH + CONV_W
Z_W = Z_GATE + LANES
ALIBI = tuple(2.0 ** (-8.0 * h / SWA_HEADS) for h in range(1, SWA_HEADS + 1))


def _bdot(a, b, dims):
    return lax.dot_general(a.astype(jnp.bfloat16), b.astype(jnp.bfloat16), (dims, ((), ())),
                           preferred_element_type=jnp.float32)


def _row_spec(arr, n, tm, d):
    g = arr.shape[0]
    if g == n:
        return arr, pl.BlockSpec((tm, d), lambda i, *_: (i, 0))
    per = n // g
    return arr[:, None, :], pl.BlockSpec((None, 1, d), lambda i, *_: (i * tm // per, 0, 0))


def _norm_mod(x, gain, sc, sh):
    y = x * lax.rsqrt(jnp.mean(x * x, axis=-1, keepdims=True) + EPS) * gain
    return y * (1.0 + sc) + sh


def _adaln_kernel(c_ref, w_ref, b_ref, o_ref):
    c = c_ref[...]
    o_ref[...] = _bdot(c * jax.nn.sigmoid(c), w_ref[...], ((1,), (0,))) + b_ref[...]


def adaln_mod(c, w, b):
    bsz, d = c.shape
    m = w.shape[1]
    tn = 512
    return pl.pallas_call(
        _adaln_kernel,
        grid=(m // tn,),
        in_specs=[pl.BlockSpec((bsz, d), lambda j: (0, 0)),
                  pl.BlockSpec((d, tn), lambda j: (0, j)),
                  pl.BlockSpec((1, tn), lambda j: (0, j))],
        out_specs=pl.BlockSpec((bsz, tn), lambda j: (0, j)),
        out_shape=jax.ShapeDtypeStruct((bsz, m), jnp.float32),
        name="adaln_mod",
    )(c, w, b.reshape(1, m))


def _mix_in_kernel(x_ref, gain_ref, sc_ref, sh_ref, w_ref, z_ref):
    h = _norm_mod(x_ref[...], gain_ref[...], sc_ref[...], sh_ref[...])
    z_ref[...] = jnp.dot(h.astype(jnp.bfloat16), w_ref[...], preferred_element_type=jnp.float32)


def mix_in(x, gain, sc, sh, w_bf):
    n, d = x.shape
    zw = w_bf.shape[1]
    tm, tn = min(ROW_TILE, n), zw // 3
    sc, sc_spec = _row_spec(sc, n, tm, d)
    sh, sh_spec = _row_spec(sh, n, tm, d)
    return pl.pallas_call(
        _mix_in_kernel,
        grid=(n // tm, zw // tn),
        in_specs=[pl.BlockSpec((tm, d), lambda i, j: (i, 0)),
                  pl.BlockSpec((1, d), lambda i, j: (0, 0)),
                  sc_spec, sh_spec,
                  pl.BlockSpec((d, tn), lambda i, j: (0, j))],
        out_specs=pl.BlockSpec((tm, tn), lambda i, j: (i, j)),
        out_shape=jax.ShapeDtypeStruct((n, zw), jnp.float32),
        name="mix_in",
    )(x, gain, sc, sh, w_bf)


def _log_sigmoid(x):
    return jnp.minimum(x, 0.0) - jnp.log(1.0 + jnp.exp(-jnp.abs(x)))


def _mlstm_kernel(chunk, valid, q_ref, k_ref, v_ref, o_ref, gate_ref, gb_ref, g_ref, c0_ref, n0_ref, m0_ref,
                  hm_ref, c_ref, n_ref, m_ref):
    tc = q_ref.shape[0]
    nh, dh, L = MLSTM_HEADS, MLSTM_DH, chunk

    @pl.when(pl.program_id(1) == 0)
    def _():
        c_ref[...] = c0_ref[...]
        n_ref[...] = n0_ref[...]
        m_ref[...] = m0_ref[...]

    row = lax.broadcasted_iota(jnp.int32, (L, L), 0)
    col = lax.broadcasted_iota(jnp.int32, (L, L), 1)
    eye = row == col
    visible = (col <= row) & (col < valid)
    rcol = lax.broadcasted_iota(jnp.int32, (L, 1), 0)
    gb = gb_ref[...]
    gain = g_ref[...]

    def to_row(x_col):
        return jnp.sum(jnp.where(eye, x_col, 0.0), axis=0, keepdims=True)

    def one_chunk(ci, carry):
        r0 = pl.multiple_of(ci * L, L)
        pre = gate_ref[pl.ds(r0, L), :] + gb
        for hd in range(nh):
            sl = slice(hd * dh, (hd + 1) * dh)
            q = q_ref[pl.ds(r0, L), sl]
            k = k_ref[pl.ds(r0, L), sl] * (dh ** -0.5)
            v = v_ref[pl.ds(r0, L), sl]
            ig = pre[:, hd:hd + 1]
            lf = jnp.where(rcol < valid, _log_sigmoid(pre[:, nh + hd:nh + hd + 1]), 0.0)
            b = lf
            s = 1
            while s < L:
                b = b + jnp.where(rcol >= s, pltpu.roll(b, s, 0), 0.0)
                s *= 2
            cmat, nrow, m_prev = c_ref[hd], n_ref[hd], m_ref[hd][:, 0:1]
            a = b + m_prev
            d = jnp.where(visible, b - to_row(b) + to_row(ig), -jnp.inf)
            m_t = jnp.maximum(a, jnp.max(d, axis=-1, keepdims=True))
            w_inter = jnp.exp(a - m_t)
            w_intra = jnp.exp(d - m_t)
            qk = _bdot(q, k, ((1,), (1,))) * w_intra
            num = w_inter * _bdot(q, cmat, ((1,), (1,))) + _bdot(qk, v, ((1,), (0,)))
            den = w_inter * jnp.sum(q * nrow, axis=-1, keepdims=True) + jnp.sum(qk, axis=-1, keepdims=True)
            h = num / jnp.maximum(jnp.abs(den), jnp.exp(-m_t))
            hn = h * lax.rsqrt(jnp.mean(h * h, axis=-1, keepdims=True) + EPS) * gain[:, sl]
            hm_ref[pl.ds(r0, L), sl] = hn * jax.nn.sigmoid(o_ref[pl.ds(r0, L), sl])
            m_last = m_t[L - 1:L]
            wl_inter = w_inter[L - 1:L]
            wl = jnp.where(rcol < valid, jnp.exp(b[L - 1:L] - b + ig - m_last), 0.0)
            c_ref[hd] = wl_inter * cmat + _bdot(wl * v, k, ((0,), (0,)))
            n_ref[hd] = wl_inter * nrow + jnp.sum(wl * k, axis=0, keepdims=True)
            m_ref[hd] = jnp.broadcast_to(m_last, (1, dh))
        return carry

    lax.fori_loop(0, tc // L, one_chunk, 0)


def mlstm_layer(z, gate_b, mh_g, c0, n0, m0, seq_len):
    bsz = z.shape[0] // seq_len
    nh, dh = MLSTM_HEADS, MLSTM_DH
    chunk = math.gcd(seq_len, MLSTM_CHUNK)
    valid, padded = chunk, seq_len
    if chunk % SUBLANES:
        assert seq_len < SUBLANES
        chunk = padded = SUBLANES
        z = jnp.pad(z.reshape(bsz, seq_len, -1), ((0, 0), (0, padded - seq_len), (0, 0))).reshape(bsz * padded, -1)
    n = bsz * padded
    tc = min(padded, ROW_TILE)
    steps = padded // tc
    w = nh * dh
    gb = jnp.zeros((1, LANES), jnp.float32).at[0, :2 * nh].set(gate_b)

    def zcol(off, width):
        return pl.BlockSpec((tc, width), lambda b, c: (b * steps + c, off // width))

    def state(shape):
        return pl.BlockSpec((None,) + shape, lambda b, c: (b,) + (0,) * len(shape))

    hm, c1, n1, m1 = pl.pallas_call(
        functools.partial(_mlstm_kernel, chunk, valid),
        grid=(bsz, steps),
        in_specs=[zcol(Z_Q, w), zcol(Z_K, w), zcol(Z_V, w), zcol(Z_O, w), zcol(Z_GATE, LANES),
                  pl.BlockSpec((1, LANES), lambda b, c: (0, 0)),
                  pl.BlockSpec((1, w), lambda b, c: (0, 0)),
                  state((nh, dh, dh)), state((nh, 1, dh)), state((nh, 1, dh))],
        out_specs=[pl.BlockSpec((tc, w), lambda b, c: (b * steps + c, 0)),
                   state((nh, dh, dh)), state((nh, 1, dh)), state((nh, 1, dh))],
        out_shape=[jax.ShapeDtypeStruct((n, w), jnp.float32),
                   jax.ShapeDtypeStruct((bsz, nh, dh, dh), jnp.float32),
                   jax.ShapeDtypeStruct((bsz, nh, 1, dh), jnp.float32),
                   jax.ShapeDtypeStruct((bsz, nh, 1, dh), jnp.float32)],
        compiler_params=pltpu.CompilerParams(dimension_semantics=("arbitrary", "arbitrary")),
        name="mlstm",
    )(z, z, z, z, z, gb, mh_g.reshape(1, w), c0, n0[:, :, None, :],
      jnp.broadcast_to(m0[:, :, None, None], (bsz, nh, 1, dh)))
    if padded != seq_len:
        hm = hm.reshape(bsz, padded, w)[:, :seq_len].reshape(bsz * seq_len, w)
    return hm, c1, n1[:, :, 0, :], m1[:, :, 0, 0]


def _attend(q, kb, vb, dist, visible, slope, sink):
    s = _bdot(q, kb, ((1,), (1,))) * (SWA_DH ** -0.5) - slope * dist.astype(jnp.float32)
    s = jnp.where(visible, s, -jnp.inf)
    mx = jnp.maximum(jnp.max(s, axis=-1, keepdims=True), sink)
    e = jnp.exp(s - mx)
    p = e / (jnp.sum(e, axis=-1, keepdims=True) + jnp.exp(sink - mx))
    return _bdot(p, vb, ((1,), (0,)))


def _swa_banded_kernel(q_ref, kp_ref, kc_ref, vp_ref, vc_ref, sink_ref, o_ref):
    j = pl.program_id(1)
    w = q_ref.shape[0]
    t = lax.broadcasted_iota(jnp.int32, (w, 2 * w), 0)
    i = lax.broadcasted_iota(jnp.int32, (w, 2 * w), 1)
    dist = t + w - i
    visible = (dist >= 0) & (dist <= WINDOW) & ((j > 0) | (i >= w))
    sinks = sink_ref[...]
    for kv in range(SWA_KV_HEADS):
        ks = slice(kv * SWA_DH, (kv + 1) * SWA_DH)
        kb = jnp.concatenate([kp_ref[:, ks], kc_ref[:, ks]], axis=0)
        vb = jnp.concatenate([vp_ref[:, ks], vc_ref[:, ks]], axis=0)
        for g in range(SWA_GROUP):
            h = kv * SWA_GROUP + g
            hs = slice(h * SWA_DH, (h + 1) * SWA_DH)
            o_ref[:, hs] = _attend(q_ref[:, hs], kb, vb, dist, visible, ALIBI[h], sinks[:, h:h + 1])


def swa_banded(z, sinks, seq_len):
    n = z.shape[0]
    w = WINDOW
    nb = seq_len // w
    sink_row = jnp.zeros((1, LANES), jnp.float32).at[0, :SWA_HEADS].set(sinks)

    def cur(off, width):
        return pl.BlockSpec((w, width), lambda b, j: (b * nb + j, off // width))

    def prev(off, width):
        return pl.BlockSpec((w, width), lambda b, j: (b * nb + jnp.maximum(j - 1, 0), off // width))

    return pl.pallas_call(
        _swa_banded_kernel,
        grid=(n // seq_len, nb),
        in_specs=[cur(Z_SQ, SWA_W), prev(Z_SK, SWA_KVW), cur(Z_SK, SWA_KVW), prev(Z_SV, SWA_KVW),
                  cur(Z_SV, SWA_KVW), pl.BlockSpec((1, LANES), lambda b, j: (0, 0))],
        out_specs=pl.BlockSpec((w, SWA_W), lambda b, j: (b * nb + j, 0)),
        out_shape=jax.ShapeDtypeStruct((n, SWA_W), jnp.float32),
        name="swa_banded",
    )(z, z, z, z, z, sink_row)


def _swa_cached_kernel(t_new, n_keys, q_ref, k_ref, v_ref, sink_ref, o_ref):
    m = q_ref.shape[1]
    nkp = k_ref.shape[0]
    r = lax.broadcasted_iota(jnp.int32, (m, nkp), 0)
    i = lax.broadcasted_iota(jnp.int32, (m, nkp), 1)
    first_key_pos = PAST_LEN - (n_keys - t_new)
    dist = (n_keys - t_new) + r % t_new - i
    visible = (dist >= 0) & (dist <= WINDOW) & (i < n_keys) & (first_key_pos + i >= 0)
    rg = lax.broadcasted_iota(jnp.int32, (m, 1), 0) // t_new
    sinks = sink_ref[...]
    for kv in range(SWA_KV_HEADS):
        ks = slice(kv * SWA_DH, (kv + 1) * SWA_DH)
        h0 = kv * SWA_GROUP
        slope = jnp.where(rg == 0, ALIBI[h0], ALIBI[h0 + 1])
        sink = jnp.where(rg == 0, sinks[:, h0:h0 + 1], sinks[:, h0 + 1:h0 + 2])
        o_ref[kv] = _attend(q_ref[kv], k_ref[:, ks], v_ref[:, ks], dist, visible, slope, sink)


def swa_cached(q, k_all, v_all, sinks):
    assert SWA_GROUP == 2
    bsz, t_new, _ = q.shape
    n_keys = k_all.shape[1]
    nkp = -(-n_keys // SUBLANES) * SUBLANES
    pad = ((0, 0), (0, nkp - n_keys), (0, 0))
    k_all, v_all = jnp.pad(k_all, pad), jnp.pad(v_all, pad)
    m = SWA_GROUP * t_new
    qs = q.reshape(bsz, t_new, SWA_KV_HEADS, SWA_GROUP, SWA_DH).transpose(0, 2, 3, 1, 4)
    qs = qs.reshape(bsz, SWA_KV_HEADS, m, SWA_DH)
    sink_row = jnp.zeros((1, LANES), jnp.float32).at[0, :SWA_HEADS].set(sinks)
    o = pl.pallas_call(
        functools.partial(_swa_cached_kernel, t_new, n_keys),
        grid=(bsz,),
        in_specs=[pl.BlockSpec((None, SWA_KV_HEADS, m, SWA_DH), lambda b: (b, 0, 0, 0)),
                  pl.BlockSpec((None, nkp, SWA_KVW), lambda b: (b, 0, 0)),
                  pl.BlockSpec((None, nkp, SWA_KVW), lambda b: (b, 0, 0)),
                  pl.BlockSpec((1, LANES), lambda b: (0, 0))],
        out_specs=pl.BlockSpec((None, SWA_KV_HEADS, m, SWA_DH), lambda b: (b, 0, 0, 0)),
        out_shape=jax.ShapeDtypeStruct((bsz, SWA_KV_HEADS, m, SWA_DH), jnp.float32),
        name="swa_cached",
    )(qs, k_all, v_all, sink_row)
    o = o.reshape(bsz, SWA_KV_HEADS, SWA_GROUP, t_new, SWA_DH).transpose(0, 3, 1, 2, 4)
    return o.reshape(bsz, t_new, SWA_W)


def _mix_out_kernel(seq_len, x_ref, gate_ref, hm_ref, oa_ref, cb_ref, cc_ref, ch_ref, hcc_ref, hch_ref,
                    pa_ref, pb_ref, cw_ref, w_ref, o_ref, u_ref):
    i = pl.program_id(0)
    tm = x_ref.shape[0]
    u = cc_ref[...] * ch_ref[...]
    u_ref[...] = u
    hu = hcc_ref[...] * hch_ref[...]
    r = lax.broadcasted_iota(jnp.int32, u.shape, 0)
    p = (i * tm + r) % seq_len
    pa, pb = pa_ref[...], pb_ref[...]
    last, last2 = hu[SUBLANES - 1:SUBLANES], hu[SUBLANES - 2:SUBLANES - 1]
    u1 = jnp.where(r >= 1, pltpu.roll(u, 1, 0), last)
    u2 = jnp.where(r >= 2, pltpu.roll(u, 2, 0), jnp.where(r == 1, last, last2))
    s1 = jnp.where(p >= 1, u1, pa)
    s2 = jnp.where(p >= 2, u2, jnp.where(p == 1, pa, pb))
    cw = cw_ref[...]
    yc = cb_ref[...] * (cw[0:1] * s2 + cw[1:2] * s1 + cw[2:3] * u)
    cat = jnp.concatenate([hm_ref[...], oa_ref[...], yc], axis=-1)
    mix = jnp.dot(cat.astype(jnp.bfloat16), w_ref[...], preferred_element_type=jnp.float32)
    o_ref[...] = x_ref[...] + gate_ref[...] * mix


def mix_out(x, gate, hm, oa, z, conv_prev, conv_w, w_out_bf, seq_len):
    n, d = x.shape
    c = CONV_W
    tm = min(ROW_TILE, n)
    gate, gate_spec = _row_spec(gate, n, tm, d)
    if seq_len % tm:
        pa = jnp.repeat(conv_prev[:, 1], seq_len, axis=0)
        pb = jnp.repeat(conv_prev[:, 0], seq_len, axis=0)
    else:
        pa, pb = conv_prev[:, 1], conv_prev[:, 0]
    pa, pa_spec = _row_spec(pa, n, tm, c)
    pb, pb_spec = _row_spec(pb, n, tm, c)
    cw = jnp.zeros((SUBLANES, c), jnp.float32).at[:CONV_WIDTH].set(conv_w)

    def zcol(off, width):
        return pl.BlockSpec((tm, width), lambda i: (i, off // width))

    def zhalo(off):
        return pl.BlockSpec((SUBLANES, c), lambda i: (jnp.maximum(i * (tm // SUBLANES) - 1, 0), off // c))

    return pl.pallas_call(
        functools.partial(_mix_out_kernel, seq_len),
        grid=(n // tm,),
        in_specs=[pl.BlockSpec((tm, d), lambda i: (i, 0)), gate_spec,
                  pl.BlockSpec((tm, MLSTM_W), lambda i: (i, 0)),
                  pl.BlockSpec((tm, SWA_W), lambda i: (i, 0)),
                  zcol(Z_CB, c), zcol(Z_CC, c), zcol(Z_CH, c), zhalo(Z_CC), zhalo(Z_CH),
                  pa_spec, pb_spec,
                  pl.BlockSpec((SUBLANES, c), lambda i: (0, 0)),
                  pl.BlockSpec((d, d), lambda i: (0, 0))],
        out_specs=[pl.BlockSpec((tm, d), lambda i: (i, 0)), pl.BlockSpec((tm, c), lambda i: (i, 0))],
        out_shape=[jax.ShapeDtypeStruct((n, d), jnp.float32), jax.ShapeDtypeStruct((n, c), jnp.float32)],
        name="mix_out",
    )(x, gate, hm, oa, z, z, z, z, z, pa, pb, cw, w_out_bf)


def _pair_candidates(s1, i1, s2, i2):
    k, sub = PEER_TOPK, SUBLANES
    assert k == 2 * sub
    row = lax.broadcasted_iota(jnp.int32, (sub, s1.shape[1]), 0)
    sums, ids = [], []

    def emit(a, b0, nvalid):
        c = s1[a:a + 1] + s2[b0:b0 + sub]
        sums.append(c if nvalid >= sub else jnp.where(row < nvalid, c, -jnp.inf))
        ids.append(i1[a:a + 1] * PEER_NKEYS + i2[b0:b0 + sub])

    for a in range(sub):
        nb = k // (a + 1)
        for b0 in range(0, nb, sub):
            emit(a, b0, nb - b0)
    sums.append(s1[sub:] + s2[0:1])
    ids.append(i1[sub:] * PEER_NKEYS + i2[0:1])
    return jnp.concatenate(sums, axis=0), jnp.concatenate(ids, axis=0)


def _top_rows(s, payload=None, k=PEER_TOPK):
    rows = lax.broadcasted_iota(jnp.int32, s.shape, 0)
    big = jnp.int32(s.shape[0])
    vals, ids = [], []
    for _ in range(k):
        m = jnp.max(s, axis=0, keepdims=True)
        r = jnp.min(jnp.where(s == m, rows, big), axis=0, keepdims=True)
        hit = rows == r
        vals.append(m)
        ids.append(r if payload is None else jnp.max(jnp.where(hit, payload, -1), axis=0, keepdims=True))
        s = jnp.where(hit, -jnp.inf, s)
    return jnp.concatenate(vals, axis=0), jnp.concatenate(ids, axis=0)


def _peer_route_kernel(x_ref, gain_ref, sc_ref, sh_ref, wq_ref, sk_ref, h_ref, idx_ref, gt_ref, q_scr):
    tb = x_ref.shape[0]
    h = _norm_mod(x_ref[...], gain_ref[...], sc_ref[...], sh_ref[...])
    h_ref[...] = h
    q = jnp.dot(h.astype(jnp.bfloat16), wq_ref[...], preferred_element_type=jnp.float32)
    for sub in range(tb // LANES):
        for head in range(PEER_HEADS):
            q_scr[sub * PEER_HEADS + head] = q[sub * LANES:(sub + 1) * LANES, head * PEER_DKEY:(head + 1) * PEER_DKEY]

    def per_head(j, carry):
        sub = j // PEER_HEADS
        head = j % PEER_HEADS
        qh = q_scr[j].astype(jnp.bfloat16)
        tops = []
        for p in range(2):
            st = lax.dot_general(sk_ref[p], qh[:, p * HALF_KEY:(p + 1) * HALF_KEY],
                                 (((1,), (1,)), ((), ())), preferred_element_type=jnp.float32)
            tops.append(_top_rows(st))
        (s1, i1), (s2, i2) = tops
        top_s, top_i = _top_rows(*_pair_candidates(s1, i1, s2, i2))
        e = jnp.exp(top_s - jnp.max(top_s, axis=0, keepdims=True))
        off = pl.multiple_of(head * PEER_TOPK, PEER_TOPK)
        idx_ref[sub, pl.ds(off, PEER_TOPK), :] = top_i
        gt_ref[sub, pl.ds(off, PEER_TOPK), :] = e / jnp.sum(e, axis=0, keepdims=True)
        return carry

    lax.fori_loop(0, (tb // LANES) * PEER_HEADS, per_head, 0)


def peer_route(x, gain, sc, sh, wq_bf, sk_bf):
    n, d = x.shape
    tb = ROUTE_TB
    nsub = n // LANES
    sc, sc_spec = _row_spec(sc, n, tb, d)
    sh, sh_spec = _row_spec(sh, n, tb, d)
    sel_spec = pl.BlockSpec((tb // LANES, PEER_SEL, LANES), lambda i: (i, 0, 0))
    return pl.pallas_call(
        _peer_route_kernel,
        grid=(n // tb,),
        in_specs=[pl.BlockSpec((tb, d), lambda i: (i, 0)),
                  pl.BlockSpec((1, d), lambda i: (0, 0)),
                  sc_spec, sh_spec,
                  pl.BlockSpec(wq_bf.shape, lambda i: (0, 0)),
                  pl.BlockSpec(sk_bf.shape, lambda i: (0, 0, 0))],
        out_specs=[pl.BlockSpec((tb, d), lambda i: (i, 0)), sel_spec, sel_spec],
        out_shape=[jax.ShapeDtypeStruct((n, d), jnp.float32),
                   jax.ShapeDtypeStruct((nsub, PEER_SEL, LANES), jnp.int32),
                   jax.ShapeDtypeStruct((nsub, PEER_SEL, LANES), jnp.float32)],
        scratch_shapes=[pltpu.VMEM((tb // LANES * PEER_HEADS, LANES, PEER_DKEY), jnp.float32)],
        compiler_params=pltpu.CompilerParams(dimension_semantics=("arbitrary",)),
        name="peer_route",
    )(x, gain, sc, sh, wq_bf, sk_bf)


def _gelu_tanh(x):
    return 0.5 * x * (1.0 + jnp.tanh(math.sqrt(2.0 / math.pi) * (x + 0.044715 * (x * x * x))))


def _peer_experts_kernel(idx_ref, h_ref, gt_ref, xres_ref, gate_ref, tab_ref, o_ref, buf, sem):
    tb, d = h_ref.shape
    nch = d // LANES
    per_token_gate = gate_ref.shape[0] == tb

    def row_copy(e, slot, r):
        src = tab_ref.at[pl.ds(pl.multiple_of(e * nch, nch), nch)]
        return pltpu.make_async_copy(src, buf.at[slot, :, r, :], sem.at[slot])

    def issue(t, slot):
        for r in range(PEER_SEL):
            row_copy(idx_ref[t, r], slot, r).start(priority=r % 2)

    def wait(slot):
        pltpu.make_async_copy(tab_ref.at[pl.ds(0, PEER_SEL * nch)], buf.at[slot], sem.at[slot]).wait()

    lane = lax.broadcasted_iota(jnp.int32, (PEER_SEL, tb), 1)

    def ffn(t, slot):
        hrow = h_ref[pl.ds(t, 1), :]
        part = None
        for c in range(nch):
            u = lax.bitcast_convert_type(buf[slot, c] & jnp.uint32(0xFFFF0000), jnp.float32)
            term = u * hrow[:, c * LANES:(c + 1) * LANES]
            part = term if part is None else part + term
        s = jnp.sum(part, axis=-1, keepdims=True)
        g = jnp.sum(jnp.where(lane == t, gt_ref[...], 0.0), axis=-1, keepdims=True)
        w = g * _gelu_tanh(s)
        y = jnp.concatenate(
            [jnp.sum(lax.bitcast_convert_type(buf[slot, c] << 16, jnp.float32) * w, axis=0, keepdims=True)
             for c in range(nch)], axis=1)
        gate = gate_ref[pl.ds(t, 1), :] if per_token_gate else gate_ref[...]
        return xres_ref[pl.ds(t, 1), :] + gate * y

    ahead = PEER_NSLOT - 1
    for t0 in range(ahead):
        issue(t0, t0)

    def group(t0, last):
        for j in range(PEER_NSLOT):
            t = t0 + j
            wait(j)
            out = ffn(t, j)
            if not last or j == 0:
                issue(t + ahead, (j + ahead) % PEER_NSLOT)
            o_ref[pl.ds(t, 1), :] = out

    def main_body(gi, carry):
        group(pl.multiple_of(gi * PEER_NSLOT, PEER_NSLOT), False)
        return carry

    lax.fori_loop(0, tb // PEER_NSLOT - 1, main_body, 0)
    group(tb - PEER_NSLOT, True)


def pack_expert_table(u, v):
    ub = lax.bitcast_convert_type(u.astype(jnp.bfloat16), jnp.uint16).astype(jnp.uint32)
    vb = lax.bitcast_convert_type(v.astype(jnp.bfloat16), jnp.uint16).astype(jnp.uint32)
    return (ub << 16) | vb


def peer_experts(h, idx_t, g_t, tab, xres, gate):
    n, d = h.shape
    tb = PEER_TB
    gate, gate_spec = _row_spec(gate, n, tb, d)
    nrow = d // LANES
    return pl.pallas_call(
        _peer_experts_kernel,
        grid=(n // tb,),
        in_specs=[pl.BlockSpec((None, tb, PEER_SEL), lambda i: (i, 0, 0), memory_space=pltpu.SMEM),
                  pl.BlockSpec((tb, d), lambda i: (i, 0)),
                  pl.BlockSpec((None, PEER_SEL, tb), lambda i: (i, 0, 0)),
                  pl.BlockSpec((tb, d), lambda i: (i, 0)),
                  gate_spec,
                  pl.BlockSpec(memory_space=pl.ANY)],
        out_specs=pl.BlockSpec((tb, d), lambda i: (i, 0)),
        out_shape=jax.ShapeDtypeStruct((n, d), jnp.float32),
        scratch_shapes=[pltpu.VMEM((PEER_NSLOT, nrow, PEER_SEL, LANES), jnp.uint32),
                        pltpu.SemaphoreType.DMA((PEER_NSLOT,))],
        compiler_params=pltpu.CompilerParams(dimension_semantics=("arbitrary",)),
        name="peer_experts",
    )(idx_t.transpose(0, 2, 1), h, g_t, xres, gate, tab.reshape(-1, LANES))


def _final_norm_kernel(x_ref, g_ref, o_ref):
    x = x_ref[...]
    o_ref[...] = x * lax.rsqrt(jnp.mean(x * x, axis=-1, keepdims=True) + EPS) * g_ref[...]


def final_norm(x, g):
    n, d = x.shape
    tm = min(ROW_TILE, n)
    return pl.pallas_call(
        _final_norm_kernel,
        grid=(n // tm,),
        in_specs=[pl.BlockSpec((tm, d), lambda i: (i, 0)), pl.BlockSpec((1, d), lambda i: (0, 0))],
        out_specs=pl.BlockSpec((tm, d), lambda i: (i, 0)),
        out_shape=jax.ShapeDtypeStruct((n, d), x.dtype),
        name="final_norm",
    )(x, g.reshape(1, d))


def _per_tile_rows(m, seq_len, tile):
    return m if seq_len % tile == 0 else jnp.repeat(m, seq_len, axis=0)


def trunk_layer(x, seq_len, mod, p, state):
    n, d = x.shape
    bsz = n // seq_len
    sh1, sc1, gt1, sh2, sc2, gt2 = (_per_tile_rows(m, seq_len, ROW_TILE) for m in jnp.split(mod, 6, axis=-1))
    z = mix_in(x, p['g_mix'], sc1, sh1, p['w_in'])
    k_new = z[:, Z_SK:Z_SK + SWA_KVW].reshape(bsz, seq_len, SWA_KVW)
    v_new = z[:, Z_SV:Z_SV + SWA_KVW].reshape(bsz, seq_len, SWA_KVW)
    if state is None:
        win_rows = min(WINDOW, PAST_LEN)
        c0 = jnp.zeros((bsz, MLSTM_HEADS, MLSTM_DH, MLSTM_DH), jnp.float32)
        n0 = jnp.zeros((bsz, MLSTM_HEADS, MLSTM_DH), jnp.float32)
        m0 = jnp.zeros((bsz, MLSTM_HEADS), jnp.float32)
        cbuf = jnp.zeros((bsz, CONV_WIDTH - 1, CONV_W), jnp.float32)
        oa = swa_banded(z, p['sinks'], seq_len)
        k_all, v_all = k_new, v_new
    else:
        kbuf, vbuf, cbuf, c0, n0, m0 = state
        win_rows = kbuf.shape[1]
        k_all = jnp.concatenate([kbuf.reshape(bsz, win_rows, SWA_KVW), k_new], axis=1)
        v_all = jnp.concatenate([vbuf.reshape(bsz, win_rows, SWA_KVW), v_new], axis=1)
        q = z[:, Z_SQ:Z_SQ + SWA_W].reshape(bsz, seq_len, SWA_W)
        oa = swa_cached(q, k_all, v_all, p['sinks']).reshape(n, SWA_W)
    hm, c1, n1, m1 = mlstm_layer(z, p['gate_b'], p['mh_g'], c0, n0, m0, seq_len)
    x, u = mix_out(x, gt1, hm, oa, z, cbuf, p['conv_w'], p['w_out'], seq_len)
    k_keep = k_all[:, -win_rows:].reshape(bsz, win_rows, SWA_KV_HEADS, SWA_DH)
    v_keep = v_all[:, -win_rows:].reshape(bsz, win_rows, SWA_KV_HEADS, SWA_DH)
    cbuf_new = jnp.concatenate([cbuf, u.reshape(bsz, seq_len, CONV_W)], axis=1)[:, -(CONV_WIDTH - 1):]
    h, idx_t, g_t = peer_route(x, p['g_ffn'], sc2, sh2, p['wq'], p['subkeys'])
    x = peer_experts(h, idx_t, g_t, p['tab'], x, gt2)
    return x, (k_keep, v_keep, cbuf_new, c1, n1, m1)


def kernel(x_prompt, x_sample, cache_swa_k, cache_swa_v, state_conv, state_mlstm_C, state_mlstm_n, state_mlstm_m, c_prompt, c_sample, ada_w, ada_b, norm_mix_g, norm_ffn_g, w_in, w_out, mlstm_gate_b, mlstm_norm_g, swa_sinks, conv_w, peer_wq, peer_subkeys, peer_u, peer_v, final_g):
    bp, tp, d = x_prompt.shape
    bs, ts, _ = x_sample.shape
    xp, xs = x_prompt.reshape(bp * tp, d), x_sample.reshape(bs * ts, d)
    c_all = jnp.concatenate([c_prompt, c_sample], axis=0)
    bf = jnp.bfloat16
    n_gate = 2 * MLSTM_HEADS
    new_p, new_s = [], []
    for l in range(DEPTH):
        wl = w_in[l]
        w_perm = jnp.concatenate([wl[:, :Z_SQ], wl[:, Z_SQ + n_gate:], wl[:, Z_SQ:Z_SQ + n_gate],
                                  jnp.zeros((d, LANES - n_gate), wl.dtype)], axis=1)
        p = dict(g_mix=norm_mix_g[l].reshape(1, d), g_ffn=norm_ffn_g[l].reshape(1, d),
                 w_in=w_perm.astype(bf), w_out=w_out[l].astype(bf), gate_b=mlstm_gate_b[l],
                 mh_g=mlstm_norm_g[l], sinks=swa_sinks[l], conv_w=conv_w[l], wq=peer_wq[l].astype(bf),
                 subkeys=peer_subkeys[l].astype(bf), tab=pack_expert_table(peer_u[l], peer_v[l]))
        mod = adaln_mod(c_all, ada_w[l], ada_b[l])
        xp, sp = trunk_layer(xp, tp, mod[:bp], p, None)
        st = (cache_swa_k[l], cache_swa_v[l], state_conv[l], state_mlstm_C[l], state_mlstm_n[l], state_mlstm_m[l])
        xs, ss = trunk_layer(xs, ts, mod[bp:], p, st)
        new_p.append(sp)
        new_s.append(ss)
    y_prompt = final_norm(xp, final_g).reshape(bp, tp, d)
    y_sample = final_norm(xs, final_g).reshape(bs, ts, d)
    pk, pv, pc, pC, pn, pm = [jnp.stack(t) for t in zip(*new_p)]
    sk, sv, sc, sC, sn, sm = [jnp.stack(t) for t in zip(*new_s)]
    return (y_prompt, y_sample, pk, pv, pc, pC, pn, pm, sk, sv, sc, sC, sn, sm)
```

```python
import functools
import math

import jax
import jax.numpy as jnp
from jax import lax
from jax.experimental import pallas as pl
from jax.experimental.pallas import tpu as pltpu

D_MODEL = 1024
DEPTH = 2
PAST_LEN = 16384

MLSTM_W = D_MODEL // 2
MLSTM_HEADS = 4
MLSTM_DH = MLSTM_W // MLSTM_HEADS
MLSTM_CHUNK = 64
SWA_W = D_MODEL // 4
SWA_DH = 64
SWA_HEADS = SWA_W // SWA_DH
SWA_KV_HEADS = SWA_HEADS // 2
SWA_GROUP = SWA_HEADS // SWA_KV_HEADS
SWA_KVW = SWA_KV_HEADS * SWA_DH
WINDOW = 128
CONV_W = D_MODEL - MLSTM_W - SWA_W
CONV_WIDTH = 3
PEER_HEADS = 8
PEER_NKEYS = 128
PEER_DKEY = 256
HALF_KEY = PEER_DKEY // 2
PEER_TOPK = 16
PEER_SEL = PEER_HEADS * PEER_TOPK
EPS = 1e-6

LANES = 128
SUBLANES = 8
ROW_TILE = 512
PEER_TB = LANES
PEER_NSLOT = 8

Z_Q, Z_K, Z_V, Z_O = 0, MLSTM_W, 2 * MLSTM_W, 3 * MLSTM_W
Z_SQ = 4 * MLSTM_W
Z_SK = Z_SQ + SWA_W
Z_SV = Z_SK + SWA_KVW
Z_CB = Z_SV + SWA_KVW
Z_CC = Z_CB + CONV_W
Z_CH = Z_CC + CONV_W
Z_GATE = Z_CH + CONV_W
Z_W = Z_GATE + LANES
ALIBI = tuple(2.0 ** (-8.0 * h / SWA_HEADS) for h in range(1, SWA_HEADS + 1))


def _bdot(a, b, dims):
    return lax.dot_general(a.astype(jnp.bfloat16), b.astype(jnp.bfloat16), (dims, ((), ())),
                           preferred_element_type=jnp.float32)


def _row_spec(arr, n, tm, d):
    g = arr.shape[0]
    if g == n:
        return arr, pl.BlockSpec((tm, d), lambda i, *_: (i, 0))
    per = n // g
    return arr[:, None, :], pl.BlockSpec((None, 1, d), lambda i, *_: (i * tm // per, 0, 0))


def _norm_mod(x, gain, sc, sh):
    y = x * lax.rsqrt(jnp.mean(x * x, axis=-1, keepdims=True) + EPS) * gain
    return y * (1.0 + sc) + sh


def _adaln_kernel(c_ref, w_ref, b_ref, o_ref):
    c = c_ref[...]
    o_ref[...] = _bdot(c * jax.nn.sigmoid(c), w_ref[...], ((1,), (0,))) + b_ref[...]


def adaln_mod(c, w, b):
    bsz, d = c.shape
    m = w.shape[1]
    tn = 512
    return pl.pallas_call(
        _adaln_kernel,
        grid=(m // tn,),
        in_specs=[pl.BlockSpec((bsz, d), lambda j: (0, 0)),
                  pl.BlockSpec((d, tn), lambda j: (0, j)),
                  pl.BlockSpec((1, tn), lambda j: (0, j))],
        out_specs=pl.BlockSpec((bsz, tn), lambda j: (0, j)),
        out_shape=jax.ShapeDtypeStruct((bsz, m), jnp.float32),
        name="adaln_mod",
    )(c, w, b.reshape(1, m))


def _mix_in_kernel(x_ref, gain_ref, sc_ref, sh_ref, w_ref, z_ref):
    h = _norm_mod(x_ref[...], gain_ref[...], sc_ref[...], sh_ref[...])
    z_ref[...] = jnp.dot(h.astype(jnp.bfloat16), w_ref[...], preferred_element_type=jnp.float32)


def mix_in(x, gain, sc, sh, w_bf):
    n, d = x.shape
    zw = w_bf.shape[1]
    tm, tn = min(ROW_TILE, n), zw // 3
    sc, sc_spec = _row_spec(sc, n, tm, d)
    sh, sh_spec = _row_spec(sh, n, tm, d)
    return pl.pallas_call(
        _mix_in_kernel,
        grid=(n // tm, zw // tn),
        in_specs=[pl.BlockSpec((tm, d), lambda i, j: (i, 0)),
                  pl.BlockSpec((1, d), lambda i, j: (0, 0)),
                  sc_spec, sh_spec,
                  pl.BlockSpec((d, tn), lambda i, j: (0, j))],
        out_specs=pl.BlockSpec((tm, tn), lambda i, j: (i, j)),
        out_shape=jax.ShapeDtypeStruct((n, zw), jnp.float32),
        name="mix_in",
    )(x, gain, sc, sh, w_bf)


def _log_sigmoid(x):
    return jnp.minimum(x, 0.0) - jnp.log(1.0 + jnp.exp(-jnp.abs(x)))


def _mlstm_kernel(chunk, valid, q_ref, k_ref, v_ref, o_ref, gate_ref, gb_ref, g_ref, c0_ref, n0_ref, m0_ref,
                  hm_ref, c_ref, n_ref, m_ref):
    tc = q_ref.shape[0]
    nh, dh, L = MLSTM_HEADS, MLSTM_DH, chunk

    @pl.when(pl.program_id(1) == 0)
    def _():
        c_ref[...] = c0_ref[...]
        n_ref[...] = n0_ref[...]
        m_ref[...] = m0_ref[...]

    row = lax.broadcasted_iota(jnp.int32, (L, L), 0)
    col = lax.broadcasted_iota(jnp.int32, (L, L), 1)
    eye = row == col
    visible = (col <= row) & (col < valid)
    rcol = lax.broadcasted_iota(jnp.int32, (L, 1), 0)
    gb = gb_ref[...]
    gain = g_ref[...]

    def to_row(x_col):
        return jnp.sum(jnp.where(eye, x_col, 0.0), axis=0, keepdims=True)

    def one_chunk(ci, carry):
        r0 = pl.multiple_of(ci * L, L)
        pre = gate_ref[pl.ds(r0, L), :] + gb
        for hd in range(nh):
            sl = slice(hd * dh, (hd + 1) * dh)
            q = q_ref[pl.ds(r0, L), sl]
            k = k_ref[pl.ds(r0, L), sl] * (dh ** -0.5)
            v = v_ref[pl.ds(r0, L), sl]
            ig = pre[:, hd:hd + 1]
            lf = jnp.where(rcol < valid, _log_sigmoid(pre[:, nh + hd:nh + hd + 1]), 0.0)
            b = lf
            s = 1
            while s < L:
                b = b + jnp.where(rcol >= s, pltpu.roll(b, s, 0), 0.0)
                s *= 2
            cmat, nrow, m_prev = c_ref[hd], n_ref[hd], m_ref[hd][:, 0:1]
            a = b + m_prev
            d = jnp.where(visible, b - to_row(b) + to_row(ig), -jnp.inf)
            m_t = jnp.maximum(a, jnp.max(d, axis=-1, keepdims=True))
            w_inter = jnp.exp(a - m_t)
            w_intra = jnp.exp(d - m_t)
            qk = _bdot(q, k, ((1,), (1,))) * w_intra
            num = w_inter * _bdot(q, cmat, ((1,), (1,))) + _bdot(qk, v, ((1,), (0,)))
            den = w_inter * jnp.sum(q * nrow, axis=-1, keepdims=True) + jnp.sum(qk, axis=-1, keepdims=True)
            h = num / jnp.maximum(jnp.abs(den), jnp.exp(-m_t))
            hn = h * lax.rsqrt(jnp.mean(h * h, axis=-1, keepdims=True) + EPS) * gain[:, sl]
            hm_ref[pl.ds(r0, L), sl] = hn * jax.nn.sigmoid(o_ref[pl.ds(r0, L), sl])
            m_last = m_t[L - 1:L]
            wl_inter = w_inter[L - 1:L]
            wl = jnp.where(rcol < valid, jnp.exp(b[L - 1:L] - b + ig - m_last), 0.0)
            c_ref[hd] = wl_inter * cmat + _bdot(wl * v, k, ((0,), (0,)))
            n_ref[hd] = wl_inter * nrow + jnp.sum(wl * k, axis=0, keepdims=True)
            m_ref[hd] = jnp.broadcast_to(m_last, (1, dh))
        return carry

    lax.fori_loop(0, tc // L, one_chunk, 0)


def mlstm_layer(z, gate_b, mh_g, c0, n0, m0, seq_len):
    bsz = z.shape[0] // seq_len
    nh, dh = MLSTM_HEADS, MLSTM_DH
    chunk = math.gcd(seq_len, MLSTM_CHUNK)
    valid, padded = chunk, seq_len
    if chunk % SUBLANES:
        assert seq_len < SUBLANES
        chunk = padded = SUBLANES
        z = jnp.pad(z.reshape(bsz, seq_len, -1), ((0, 0), (0, padded - seq_len), (0, 0))).reshape(bsz * padded, -1)
    n = bsz * padded
    tc = min(padded, ROW_TILE)
    steps = padded // tc
    w = nh * dh
    gb = jnp.zeros((1, LANES), jnp.float32).at[0, :2 * nh].set(gate_b)

    def zcol(off, width):
        return pl.BlockSpec((tc, width), lambda b, c: (b * steps + c, off // width))

    def state(shape):
        return pl.BlockSpec((None,) + shape, lambda b, c: (b,) + (0,) * len(shape))

    hm, c1, n1, m1 = pl.pallas_call(
        functools.partial(_mlstm_kernel, chunk, valid),
        grid=(bsz, steps),
        in_specs=[zcol(Z_Q, w), zcol(Z_K, w), zcol(Z_V, w), zcol(Z_O, w), zcol(Z_GATE, LANES),
                  pl.BlockSpec((1, LANES), lambda b, c: (0, 0)),
                  pl.BlockSpec((1, w), lambda b, c: (0, 0)),
                  state((nh, dh, dh)), state((nh, 1, dh)), state((nh, 1, dh))],
        out_specs=[pl.BlockSpec((tc, w), lambda b, c: (b * steps + c, 0)),
                   state((nh, dh, dh)), state((nh, 1, dh)), state((nh, 1, dh))],
        out_shape=[jax.ShapeDtypeStruct((n, w), jnp.float32),
                   jax.ShapeDtypeStruct((bsz, nh, dh, dh), jnp.float32),
                   jax.ShapeDtypeStruct((bsz, nh, 1, dh), jnp.float32),
                   jax.ShapeDtypeStruct((bsz, nh, 1, dh), jnp.float32)],
        compiler_params=pltpu.CompilerParams(dimension_semantics=("arbitrary", "arbitrary")),
        name="mlstm",
    )(z, z, z, z, z, gb, mh_g.reshape(1, w), c0, n0[:, :, None, :],
      jnp.broadcast_to(m0[:, :, None, None], (bsz, nh, 1, dh)))
    if padded != seq_len:
        hm = hm.reshape(bsz, padded, w)[:, :seq_len].reshape(bsz * seq_len, w)
    return hm, c1, n1[:, :, 0, :], m1[:, :, 0, 0]


def _attend(q, kb, vb, dist, visible, slope, sink):
    s = _bdot(q, kb, ((1,), (1,))) * (SWA_DH ** -0.5) - slope * dist.astype(jnp.float32)
    s = jnp.where(visible, s, -jnp.inf)
    mx = jnp.maximum(jnp.max(s, axis=-1, keepdims=True), sink)
    e = jnp.exp(s - mx)
    p = e / (jnp.sum(e, axis=-1, keepdims=True) + jnp.exp(sink - mx))
    return _bdot(p, vb, ((1,), (0,)))


def _swa_banded_kernel(q_ref, kp_ref, kc_ref, vp_ref, vc_ref, sink_ref, o_ref):
    j = pl.program_id(1)
    w = q_ref.shape[0]
    t = lax.broadcasted_iota(jnp.int32, (w, 2 * w), 0)
    i = lax.broadcasted_iota(jnp.int32, (w, 2 * w), 1)
    dist = t + w - i
    visible = (dist >= 0) & (dist <= WINDOW) & ((j > 0) | (i >= w))
    sinks = sink_ref[...]
    for kv in range(SWA_KV_HEADS):
        ks = slice(kv * SWA_DH, (kv + 1) * SWA_DH)
        kb = jnp.concatenate([kp_ref[:, ks], kc_ref[:, ks]], axis=0)
        vb = jnp.concatenate([vp_ref[:, ks], vc_ref[:, ks]], axis=0)
        for g in range(SWA_GROUP):
            h = kv * SWA_GROUP + g
            hs = slice(h * SWA_DH, (h + 1) * SWA_DH)
            o_ref[:, hs] = _attend(q_ref[:, hs], kb, vb, dist, visible, ALIBI[h], sinks[:, h:h + 1])


def swa_banded(z, sinks, seq_len):
    n = z.shape[0]
    w = WINDOW
    nb = seq_len // w
    sink_row = jnp.zeros((1, LANES), jnp.float32).at[0, :SWA_HEADS].set(sinks)

    def cur(off, width):
        return pl.BlockSpec((w, width), lambda b, j: (b * nb + j, off // width))

    def prev(off, width):
        return pl.BlockSpec((w, width), lambda b, j: (b * nb + jnp.maximum(j - 1, 0), off // width))

    return pl.pallas_call(
        _swa_banded_kernel,
        grid=(n // seq_len, nb),
        in_specs=[cur(Z_SQ, SWA_W), prev(Z_SK, SWA_KVW), cur(Z_SK, SWA_KVW), prev(Z_SV, SWA_KVW),
                  cur(Z_SV, SWA_KVW), pl.BlockSpec((1, LANES), lambda b, j: (0, 0))],
        out_specs=pl.BlockSpec((w, SWA_W), lambda b, j: (b * nb + j, 0)),
        out_shape=jax.ShapeDtypeStruct((n, SWA_W), jnp.float32),
        name="swa_banded",
    )(z, z, z, z, z, sink_row)


def _swa_cached_kernel(t_new, n_keys, q_ref, k_ref, v_ref, sink_ref, o_ref):
    m = q_ref.shape[1]
    nkp = k_ref.shape[0]
    r = lax.broadcasted_iota(jnp.int32, (m, nkp), 0)
    i = lax.broadcasted_iota(jnp.int32, (m, nkp), 1)
    first_key_pos = PAST_LEN - (n_keys - t_new)
    dist = (n_keys - t_new) + r % t_new - i
    visible = (dist >= 0) & (dist <= WINDOW) & (i < n_keys) & (first_key_pos + i >= 0)
    rg = lax.broadcasted_iota(jnp.int32, (m, 1), 0) // t_new
    sinks = sink_ref[...]
    for kv in range(SWA_KV_HEADS):
        ks = slice(kv * SWA_DH, (kv + 1) * SWA_DH)
        h0 = kv * SWA_GROUP
        slope = jnp.where(rg == 0, ALIBI[h0], ALIBI[h0 + 1])
        sink = jnp.where(rg == 0, sinks[:, h0:h0 + 1], sinks[:, h0 + 1:h0 + 2])
        o_ref[kv] = _attend(q_ref[kv], k_ref[:, ks], v_ref[:, ks], dist, visible, slope, sink)


def swa_cached(q, k_all, v_all, sinks):
    assert SWA_GROUP == 2
    bsz, t_new, _ = q.shape
    n_keys = k_all.shape[1]
    nkp = -(-n_keys // SUBLANES) * SUBLANES
    pad = ((0, 0), (0, nkp - n_keys), (0, 0))
    k_all, v_all = jnp.pad(k_all, pad), jnp.pad(v_all, pad)
    m = SWA_GROUP * t_new
    qs = q.reshape(bsz, t_new, SWA_KV_HEADS, SWA_GROUP, SWA_DH).transpose(0, 2, 3, 1, 4)
    qs = qs.reshape(bsz, SWA_KV_HEADS, m, SWA_DH)
    sink_row = jnp.zeros((1, LANES), jnp.float32).at[0, :SWA_HEADS].set(sinks)
    o = pl.pallas_call(
        functools.partial(_swa_cached_kernel, t_new, n_keys),
        grid=(bsz,),
        in_specs=[pl.BlockSpec((None, SWA_KV_HEADS, m, SWA_DH), lambda b: (b, 0, 0, 0)),
                  pl.BlockSpec((None, nkp, SWA_KVW), lambda b: (b, 0, 0)),
                  pl.BlockSpec((None, nkp, SWA_KVW), lambda b: (b, 0, 0)),
                  pl.BlockSpec((1, LANES), lambda b: (0, 0))],
        out_specs=pl.BlockSpec((None, SWA_KV_HEADS, m, SWA_DH), lambda b: (b, 0, 0, 0)),
        out_shape=jax.ShapeDtypeStruct((bsz, SWA_KV_HEADS, m, SWA_DH), jnp.float32),
        name="swa_cached",
    )(qs, k_all, v_all, sink_row)
    o = o.reshape(bsz, SWA_KV_HEADS, SWA_GROUP, t_new, SWA_DH).transpose(0, 3, 1, 2, 4)
    return o.reshape(bsz, t_new, SWA_W)


def _mix_out_kernel(seq_len, x_ref, gate_ref, hm_ref, oa_ref, cb_ref, cc_ref, ch_ref, hcc_ref, hch_ref,
                    pa_ref, pb_ref, cw_ref, w_ref, o_ref, u_ref):
    i = pl.program_id(0)
    tm = x_ref.shape[0]
    u = cc_ref[...] * ch_ref[...]
    u_ref[...] = u
    hu = hcc_ref[...] * hch_ref[...]
    r = lax.broadcasted_iota(jnp.int32, u.shape, 0)
    p = (i * tm + r) % seq_len
    pa, pb = pa_ref[...], pb_ref[...]
    last, last2 = hu[SUBLANES - 1:SUBLANES], hu[SUBLANES - 2:SUBLANES - 1]
    u1 = jnp.where(r >= 1, pltpu.roll(u, 1, 0), last)
    u2 = jnp.where(r >= 2, pltpu.roll(u, 2, 0), jnp.where(r == 1, last, last2))
    s1 = jnp.where(p >= 1, u1, pa)
    s2 = jnp.where(p >= 2, u2, jnp.where(p == 1, pa, pb))
    cw = cw_ref[...]
    yc = cb_ref[...] * (cw[0:1] * s2 + cw[1:2] * s1 + cw[2:3] * u)
    cat = jnp.concatenate([hm_ref[...], oa_ref[...], yc], axis=-1)
    mix = jnp.dot(cat.astype(jnp.bfloat16), w_ref[...], preferred_element_type=jnp.float32)
    o_ref[...] = x_ref[...] + gate_ref[...] * mix


def mix_out(x, gate, hm, oa, z, conv_prev, conv_w, w_out_bf, seq_len):
    n, d = x.shape
    c = CONV_W
    tm = min(ROW_TILE, n)
    gate, gate_spec = _row_spec(gate, n, tm, d)
    if seq_len % tm:
        pa = jnp.repeat(conv_prev[:, 1], seq_len, axis=0)
        pb = jnp.repeat(conv_prev[:, 0], seq_len, axis=0)
    else:
        pa, pb = conv_prev[:, 1], conv_prev[:, 0]
    pa, pa_spec = _row_spec(pa, n, tm, c)
    pb, pb_spec = _row_spec(pb, n, tm, c)
    cw = jnp.zeros((SUBLANES, c), jnp.float32).at[:CONV_WIDTH].set(conv_w)

    def zcol(off, width):
        return pl.BlockSpec((tm, width), lambda i: (i, off // width))

    def zhalo(off):
        return pl.BlockSpec((SUBLANES, c), lambda i: (jnp.maximum(i * (tm // SUBLANES) - 1, 0), off // c))

    return pl.pallas_call(
        functools.partial(_mix_out_kernel, seq_len),
        grid=(n // tm,),
        in_specs=[pl.BlockSpec((tm, d), lambda i: (i, 0)), gate_spec,
                  pl.BlockSpec((tm, MLSTM_W), lambda i: (i, 0)),
                  pl.BlockSpec((tm, SWA_W), lambda i: (i, 0)),
                  zcol(Z_CB, c), zcol(Z_CC, c), zcol(Z_CH, c), zhalo(Z_CC), zhalo(Z_CH),
                  pa_spec, pb_spec,
                  pl.BlockSpec((SUBLANES, c), lambda i: (0, 0)),
                  pl.BlockSpec((d, d), lambda i: (0, 0))],
        out_specs=[pl.BlockSpec((tm, d), lambda i: (i, 0)), pl.BlockSpec((tm, c), lambda i: (i, 0))],
        out_shape=[jax.ShapeDtypeStruct((n, d), jnp.float32), jax.ShapeDtypeStruct((n, c), jnp.float32)],
        name="mix_out",
    )(x, gate, hm, oa, z, z, z, z, z, pa, pb, cw, w_out_bf)


def _pair_candidates(s1, i1, s2, i2):
    k, sub = PEER_TOPK, SUBLANES
    assert k == 2 * sub
    row = lax.broadcasted_iota(jnp.int32, (sub, s1.shape[1]), 0)
    sums, ids = [], []

    def emit(a, b0, nvalid):
        c = s1[a:a + 1] + s2[b0:b0 + sub]
        sums.append(c if nvalid >= sub else jnp.where(row < nvalid, c, -jnp.inf))
        ids.append(i1[a:a + 1] * PEER_NKEYS + i2[b0:b0 + sub])

    for a in range(sub):
        nb = k // (a + 1)
        for b0 in range(0, nb, sub):
            emit(a, b0, nb - b0)
    sums.append(s1[sub:] + s2[0:1])
    ids.append(i1[sub:] * PEER_NKEYS + i2[0:1])
    return jnp.concatenate(sums, axis=0), jnp.concatenate(ids, axis=0)


def _top_rows(s, payload=None, k=PEER_TOPK):
    rows = lax.broadcasted_iota(jnp.int32, s.shape, 0)
    big = jnp.int32(s.shape[0])
    vals, ids = [], []
    for _ in range(k):
        m = jnp.max(s, axis=0, keepdims=True)
        r = jnp.min(jnp.where(s == m, rows, big), axis=0, keepdims=True)
        hit = rows == r
        vals.append(m)
        ids.append(r if payload is None else jnp.max(jnp.where(hit, payload, -1), axis=0, keepdims=True))
        s = jnp.where(hit, -jnp.inf, s)
    return jnp.concatenate(vals, axis=0), jnp.concatenate(ids, axis=0)


def _gelu_tanh(x):
    return 0.5 * x * (1.0 + jnp.tanh(math.sqrt(2.0 / math.pi) * (x + 0.044715 * (x * x * x))))


def _peer_kernel(nblk, xr_ref, gain_ref, sc_ref, sh_ref, wq_ref, sk_ref, xres_ref, gate_ref, tab_ref,
                 o_ref, q_scr, h_scr, gt_scr, idx_v, idx_s, buf, sem, idx_sem):
    s = pl.program_id(0)
    tb, d = xr_ref.shape
    nch = d // LANES
    assert tb == LANES
    routing = s < nblk
    reading = s >= 1
    slot_r = s % 2
    slot_e = 1 - slot_r
    per_token_gate = gate_ref.shape[0] == tb

    def row_copy(e, slot, r):
        src = tab_ref.at[pl.ds(pl.multiple_of(e * nch, nch), nch)]
        return pltpu.make_async_copy(src, buf.at[slot, :, r, :], sem.at[slot])

    def issue(t, slot):
        for r in range(PEER_SEL):
            row_copy(idx_s[r, t], slot, r).start(priority=r % 2)

    def wait(slot):
        pltpu.make_async_copy(tab_ref.at[pl.ds(0, PEER_SEL * nch)], buf.at[slot], sem.at[slot]).wait()

    lane = lax.broadcasted_iota(jnp.int32, (PEER_SEL, tb), 1)

    def ffn(t, slot):
        hrow = h_scr[slot_e, pl.ds(t, 1), :]
        part = None
        for c in range(nch):
            u = lax.bitcast_convert_type(buf[slot, c] & jnp.uint32(0xFFFF0000), jnp.float32)
            term = u * hrow[:, c * LANES:(c + 1) * LANES]
            part = term if part is None else part + term
        sdot = jnp.sum(part, axis=-1, keepdims=True)
        g = jnp.sum(jnp.where(lane == t, gt_scr[slot_e], 0.0), axis=-1, keepdims=True)
        w = g * _gelu_tanh(sdot)
        y = jnp.concatenate(
            [jnp.sum(lax.bitcast_convert_type(buf[slot, c] << 16, jnp.float32) * w, axis=0, keepdims=True)
             for c in range(nch)], axis=1)
        gate = gate_ref[pl.ds(t, 1), :] if per_token_gate else gate_ref[...]
        return xres_ref[pl.ds(t, 1), :] + gate * y

    ahead = PEER_NSLOT - 1

    def group(t0, last):
        for j in range(PEER_NSLOT):
            t = t0 + j
            wait(j)
            out = ffn(t, j)
            if not last or j == 0:
                issue(t + ahead, (j + ahead) % PEER_NSLOT)
            o_ref[pl.ds(t, 1), :] = out

    def per_head(head):
        qh = q_scr[head].astype(jnp.bfloat16)
        tops = []
        for p in range(2):
            st = lax.dot_general(sk_ref[p], qh[:, p * HALF_KEY:(p + 1) * HALF_KEY],
                                 (((1,), (1,)), ((), ())), preferred_element_type=jnp.float32)
            tops.append(_top_rows(st))
        (s1, i1), (s2, i2) = tops
        top_s, top_i = _top_rows(*_pair_candidates(s1, i1, s2, i2))
        e = jnp.exp(top_s - jnp.max(top_s, axis=0, keepdims=True))
        off = pl.multiple_of(head * PEER_TOPK, PEER_TOPK)
        idx_v[pl.ds(off, PEER_TOPK), :] = top_i
        gt_scr[slot_r, pl.ds(off, PEER_TOPK), :] = e / jnp.sum(e, axis=0, keepdims=True)

    @pl.when(routing)
    def _():
        h = _norm_mod(xr_ref[...], gain_ref[...], sc_ref[...], sh_ref[...])
        h_scr[slot_r] = h
        q = jnp.dot(h.astype(jnp.bfloat16), wq_ref[...], preferred_element_type=jnp.float32)
        for head in range(PEER_HEADS):
            q_scr[head] = q[:, head * PEER_DKEY:(head + 1) * PEER_DKEY]

    n_pairs = tb // (2 * PEER_NSLOT)
    assert n_pairs == PEER_HEADS

    def pair(gi, last):
        t0 = gi * (2 * PEER_NSLOT)

        @pl.when(reading)
        def _():
            group(pl.multiple_of(t0, PEER_NSLOT), False)

        @pl.when(routing)
        def _():
            per_head(gi)

        @pl.when(reading)
        def _():
            group(pl.multiple_of(t0 + PEER_NSLOT, PEER_NSLOT), last)

    def body(gi, carry):
        pair(gi, False)
        return carry

    lax.fori_loop(0, n_pairs - 1, body, 0)
    pair(n_pairs - 1, True)

    @pl.when(routing)
    def _():
        cp = pltpu.make_async_copy(idx_v, idx_s, idx_sem.at[0])
        cp.start()
        cp.wait()
        for t0 in range(ahead):
            issue(t0, t0)


def pack_expert_table(u, v):
    ub = lax.bitcast_convert_type(u.astype(jnp.bfloat16), jnp.uint16).astype(jnp.uint32)
    vb = lax.bitcast_convert_type(v.astype(jnp.bfloat16), jnp.uint16).astype(jnp.uint32)
    return (ub << 16) | vb


def peer_layer(x, gain, sc, sh, gate, wq_bf, sk_bf, tab):
    n, d = x.shape
    tb = PEER_TB
    nblk = n // tb
    nch = d // LANES

    def cur(s):
        return jnp.minimum(s, nblk - 1)

    def prev(s):
        return jnp.maximum(s - 1, 0)

    def rows(arr, blk):
        g = arr.shape[0]
        if g == n:
            return arr, pl.BlockSpec((tb, d), lambda s: (blk(s), 0))
        per = n // g
        return arr[:, None, :], pl.BlockSpec((None, 1, d), lambda s: (blk(s) * tb // per, 0, 0))

    sc, sc_spec = rows(sc, cur)
    sh, sh_spec = rows(sh, cur)
    gate, gate_spec = rows(gate, prev)
    return pl.pallas_call(
        functools.partial(_peer_kernel, nblk),
        grid=(nblk + 1,),
        in_specs=[pl.BlockSpec((tb, d), lambda s: (cur(s), 0)),
                  pl.BlockSpec((1, d), lambda s: (0, 0)),
                  sc_spec, sh_spec,
                  pl.BlockSpec(wq_bf.shape, lambda s: (0, 0)),
                  pl.BlockSpec(sk_bf.shape, lambda s: (0, 0, 0)),
                  pl.BlockSpec((tb, d), lambda s: (prev(s), 0)),
                  gate_spec,
                  pl.BlockSpec(memory_space=pl.ANY)],
        out_specs=pl.BlockSpec((tb, d), lambda s: (prev(s), 0)),
        out_shape=jax.ShapeDtypeStruct((n, d), jnp.float32),
        scratch_shapes=[pltpu.VMEM((PEER_HEADS, tb, PEER_DKEY), jnp.float32),
                        pltpu.VMEM((2, tb, d), jnp.float32),
                        pltpu.VMEM((2, PEER_SEL, tb), jnp.float32),
                        pltpu.VMEM((PEER_SEL, tb), jnp.int32),
                        pltpu.SMEM((PEER_SEL, tb), jnp.int32),
                        pltpu.VMEM((PEER_NSLOT, nch, PEER_SEL, LANES), jnp.uint32),
                        pltpu.SemaphoreType.DMA((PEER_NSLOT,)),
                        pltpu.SemaphoreType.DMA((1,))],
        compiler_params=pltpu.CompilerParams(dimension_semantics=("arbitrary",)),
        name="peer",
    )(x, gain, sc, sh, wq_bf, sk_bf, x, gate, tab.reshape(-1, LANES))


def _final_norm_kernel(x_ref, g_ref, o_ref):
    x = x_ref[...]
    o_ref[...] = x * lax.rsqrt(jnp.mean(x * x, axis=-1, keepdims=True) + EPS) * g_ref[...]


def final_norm(x, g):
    n, d = x.shape
    tm = min(ROW_TILE, n)
    return pl.pallas_call(
        _final_norm_kernel,
        grid=(n // tm,),
        in_specs=[pl.BlockSpec((tm, d), lambda i: (i, 0)), pl.BlockSpec((1, d), lambda i: (0, 0))],
        out_specs=pl.BlockSpec((tm, d), lambda i: (i, 0)),
        out_shape=jax.ShapeDtypeStruct((n, d), x.dtype),
        name="final_norm",
    )(x, g.reshape(1, d))


def _per_tile_rows(m, seq_len, tile):
    return m if seq_len % tile == 0 else jnp.repeat(m, seq_len, axis=0)


def trunk_layer(x, seq_len, mod, p, state):
    n, d = x.shape
    bsz = n // seq_len
    sh1, sc1, gt1, sh2, sc2, gt2 = (_per_tile_rows(m, seq_len, ROW_TILE) for m in jnp.split(mod, 6, axis=-1))
    z = mix_in(x, p['g_mix'], sc1, sh1, p['w_in'])
    k_new = z[:, Z_SK:Z_SK + SWA_KVW].reshape(bsz, seq_len, SWA_KVW)
    v_new = z[:, Z_SV:Z_SV + SWA_KVW].reshape(bsz, seq_len, SWA_KVW)
    if state is None:
        win_rows = min(WINDOW, PAST_LEN)
        c0 = jnp.zeros((bsz, MLSTM_HEADS, MLSTM_DH, MLSTM_DH), jnp.float32)
        n0 = jnp.zeros((bsz, MLSTM_HEADS, MLSTM_DH), jnp.float32)
        m0 = jnp.zeros((bsz, MLSTM_HEADS), jnp.float32)
        cbuf = jnp.zeros((bsz, CONV_WIDTH - 1, CONV_W), jnp.float32)
        oa = swa_banded(z, p['sinks'], seq_len)
        k_all, v_all = k_new, v_new
    else:
        kbuf, vbuf, cbuf, c0, n0, m0 = state
        win_rows = kbuf.shape[1]
        k_all = jnp.concatenate([kbuf.reshape(bsz, win_rows, SWA_KVW), k_new], axis=1)
        v_all = jnp.concatenate([vbuf.reshape(bsz, win_rows, SWA_KVW), v_new], axis=1)
        q = z[:, Z_SQ:Z_SQ + SWA_W].reshape(bsz, seq_len, SWA_W)
        oa = swa_cached(q, k_all, v_all, p['sinks']).reshape(n, SWA_W)
    hm, c1, n1, m1 = mlstm_layer(z, p['gate_b'], p['mh_g'], c0, n0, m0, seq_len)
    x, u = mix_out(x, gt1, hm, oa, z, cbuf, p['conv_w'], p['w_out'], seq_len)
    k_keep = k_all[:, -win_rows:].reshape(bsz, win_rows, SWA_KV_HEADS, SWA_DH)
    v_keep = v_all[:, -win_rows:].reshape(bsz, win_rows, SWA_KV_HEADS, SWA_DH)
    cbuf_new = jnp.concatenate([cbuf, u.reshape(bsz, seq_len, CONV_W)], axis=1)[:, -(CONV_WIDTH - 1):]
    x = peer_layer(x, p['g_ffn'], sc2, sh2, gt2, p['wq'], p['subkeys'], p['tab'])
    return x, (k_keep, v_keep, cbuf_new, c1, n1, m1)


def kernel(x_prompt, x_sample, cache_swa_k, cache_swa_v, state_conv, state_mlstm_C, state_mlstm_n, state_mlstm_m, c_prompt, c_sample, ada_w, ada_b, norm_mix_g, norm_ffn_g, w_in, w_out, mlstm_gate_b, mlstm_norm_g, swa_sinks, conv_w, peer_wq, peer_subkeys, peer_u, peer_v, final_g):
    bp, tp, d = x_prompt.shape
    bs, ts, _ = x_sample.shape
    xp, xs = x_prompt.reshape(bp * tp, d), x_sample.reshape(bs * ts, d)
    c_all = jnp.concatenate([c_prompt, c_sample], axis=0)
    bf = jnp.bfloat16
    n_gate = 2 * MLSTM_HEADS
    new_p, new_s = [], []
    for l in range(DEPTH):
        wl = w_in[l]
        w_perm = jnp.concatenate([wl[:, :Z_SQ], wl[:, Z_SQ + n_gate:], wl[:, Z_SQ:Z_SQ + n_gate],
                                  jnp.zeros((d, LANES - n_gate), wl.dtype)], axis=1)
        p = dict(g_mix=norm_mix_g[l].reshape(1, d), g_ffn=norm_ffn_g[l].reshape(1, d),
                 w_in=w_perm.astype(bf), w_out=w_out[l].astype(bf), gate_b=mlstm_gate_b[l],
                 mh_g=mlstm_norm_g[l], sinks=swa_sinks[l], conv_w=conv_w[l], wq=peer_wq[l].astype(bf),
                 subkeys=peer_subkeys[l].astype(bf), tab=pack_expert_table(peer_u[l], peer_v[l]))
        mod = adaln_mod(c_all, ada_w[l], ada_b[l])
        xp, sp = trunk_layer(xp, tp, mod[:bp], p, None)
        st = (cache_swa_k[l], cache_swa_v[l], state_conv[l], state_mlstm_C[l], state_mlstm_n[l], state_mlstm_m[l])
        xs, ss = trunk_layer(xs, ts, mod[bp:], p, st)
        new_p.append(sp)
        new_s.append(ss)
    y_prompt = final_norm(xp, final_g).reshape(bp, tp, d)
    y_sample = final_norm(xs, final_g).reshape(bs, ts, d)
    pk, pv, pc, pC, pn, pm = [jnp.stack(t) for t in zip(*new_p)]
    sk, sv, sc, sC, sn, sm = [jnp.stack(t) for t in zip(*new_s)]
    return (y_prompt, y_sample, pk, pv, pc, pC, pn, pm, sk, sv, sc, sC, sn, sm)
```

```python
import functools
import math

import jax
import jax.numpy as jnp
from jax import lax
from jax.experimental import pallas as pl
from jax.experimental.pallas import tpu as pltpu

D_MODEL = 1024
DEPTH = 2
PAST_LEN = 16384

MLSTM_W = D_MODEL // 2
MLSTM_HEADS = 4
MLSTM_DH = MLSTM_W // MLSTM_HEADS
MLSTM_CHUNK = 64
SWA_W = D_MODEL // 4
SWA_DH = 64
SWA_HEADS = SWA_W // SWA_DH
SWA_KV_HEADS = SWA_HEADS // 2
SWA_GROUP = SWA_HEADS // SWA_KV_HEADS
SWA_KVW = SWA_KV_HEADS * SWA_DH
WINDOW = 128
CONV_W = D_MODEL - MLSTM_W - SWA_W
CONV_WIDTH = 3
PEER_HEADS = 8
PEER_NKEYS = 128
PEER_DKEY = 256
HALF_KEY = PEER_DKEY // 2
PEER_TOPK = 16
PEER_SEL = PEER_HEADS * PEER_TOPK
EPS = 1e-6

LANES = 128
SUBLANES = 8
ROW_TILE = 512
PEER_TB = LANES
PEER_NSLOT = 8

Z_Q, Z_K, Z_V, Z_O = 0, MLSTM_W, 2 * MLSTM_W, 3 * MLSTM_W
Z_SQ = 4 * MLSTM_W
Z_SK = Z_SQ + SWA_W
Z_SV = Z_SK + SWA_KVW
Z_CB = Z_SV + SWA_KVW
Z_CC = Z_CB + CONV_W
Z_CH = Z_CC + CONV_W
Z_GATE = Z_CH + CONV_W
Z_W = Z_GATE + LANES
ALIBI = tuple(2.0 ** (-8.0 * h / SWA_HEADS) for h in range(1, SWA_HEADS + 1))


def _bdot(a, b, dims):
    return lax.dot_general(a.astype(jnp.bfloat16), b.astype(jnp.bfloat16), (dims, ((), ())),
                           preferred_element_type=jnp.float32)


def _row_spec(arr, n, tm, d):
    g = arr.shape[0]
    if g == n:
        return arr, pl.BlockSpec((tm, d), lambda i, *_: (i, 0))
    per = n // g
    return arr[:, None, :], pl.BlockSpec((None, 1, d), lambda i, *_: (i * tm // per, 0, 0))


def _norm_mod(x, gain, sc, sh):
    y = x * lax.rsqrt(jnp.mean(x * x, axis=-1, keepdims=True) + EPS) * gain
    return y * (1.0 + sc) + sh


def _adaln_kernel(c_ref, w_ref, b_ref, o_ref):
    c = c_ref[...]
    o_ref[...] = _bdot(c * jax.nn.sigmoid(c), w_ref[...], ((1,), (0,))) + b_ref[...]


def adaln_mod(c, w, b):
    bsz, d = c.shape
    m = w.shape[1]
    tn = 512
    return pl.pallas_call(
        _adaln_kernel,
        grid=(m // tn,),
        in_specs=[pl.BlockSpec((bsz, d), lambda j: (0, 0)),
                  pl.BlockSpec((d, tn), lambda j: (0, j)),
                  pl.BlockSpec((1, tn), lambda j: (0, j))],
        out_specs=pl.BlockSpec((bsz, tn), lambda j: (0, j)),
        out_shape=jax.ShapeDtypeStruct((bsz, m), jnp.float32),
        name="adaln_mod",
    )(c, w, b.reshape(1, m))


def _mix_in_kernel(x_ref, gain_ref, sc_ref, sh_ref, w_ref, z_ref):
    h = _norm_mod(x_ref[...], gain_ref[...], sc_ref[...], sh_ref[...])
    z_ref[...] = jnp.dot(h.astype(jnp.bfloat16), w_ref[...], preferred_element_type=jnp.float32)


def mix_in(x, gain, sc, sh, w_bf):
    n, d = x.shape
    zw = w_bf.shape[1]
    tm, tn = min(ROW_TILE, n), zw // 3
    sc, sc_spec = _row_spec(sc, n, tm, d)
    sh, sh_spec = _row_spec(sh, n, tm, d)
    return pl.pallas_call(
        _mix_in_kernel,
        grid=(n // tm, zw // tn),
        in_specs=[pl.BlockSpec((tm, d), lambda i, j: (i, 0)),
                  pl.BlockSpec((1, d), lambda i, j: (0, 0)),
                  sc_spec, sh_spec,
                  pl.BlockSpec((d, tn), lambda i, j: (0, j))],
        out_specs=pl.BlockSpec((tm, tn), lambda i, j: (i, j)),
        out_shape=jax.ShapeDtypeStruct((n, zw), jnp.float32),
        name="mix_in",
    )(x, gain, sc, sh, w_bf)


def _log_sigmoid(x):
    return jnp.minimum(x, 0.0) - jnp.log(1.0 + jnp.exp(-jnp.abs(x)))


def _mlstm_kernel(chunk, valid, q_ref, k_ref, v_ref, o_ref, gate_ref, gb_ref, g_ref, c0_ref, n0_ref, m0_ref,
                  hm_ref, c_ref, n_ref, m_ref):
    tc = q_ref.shape[0]
    nh, dh, L = MLSTM_HEADS, MLSTM_DH, chunk

    @pl.when(pl.program_id(1) == 0)
    def _():
        c_ref[...] = c0_ref[...]
        n_ref[...] = n0_ref[...]
        m_ref[...] = m0_ref[...]

    row = lax.broadcasted_iota(jnp.int32, (L, L), 0)
    col = lax.broadcasted_iota(jnp.int32, (L, L), 1)
    eye = row == col
    visible = (col <= row) & (col < valid)
    rcol = lax.broadcasted_iota(jnp.int32, (L, 1), 0)
    gb = gb_ref[...]
    gain = g_ref[...]

    def to_row(x_col):
        return jnp.sum(jnp.where(eye, x_col, 0.0), axis=0, keepdims=True)

    def one_chunk(ci, carry):
        r0 = pl.multiple_of(ci * L, L)
        pre = gate_ref[pl.ds(r0, L), :] + gb
        for hd in range(nh):
            sl = slice(hd * dh, (hd + 1) * dh)
            q = q_ref[pl.ds(r0, L), sl]
            k = k_ref[pl.ds(r0, L), sl] * (dh ** -0.5)
            v = v_ref[pl.ds(r0, L), sl]
            ig = pre[:, hd:hd + 1]
            lf = jnp.where(rcol < valid, _log_sigmoid(pre[:, nh + hd:nh + hd + 1]), 0.0)
            b = lf
            s = 1
            while s < L:
                b = b + jnp.where(rcol >= s, pltpu.roll(b, s, 0), 0.0)
                s *= 2
            cmat, nrow, m_prev = c_ref[hd], n_ref[hd], m_ref[hd][:, 0:1]
            a = b + m_prev
            d = jnp.where(visible, b - to_row(b) + to_row(ig), -jnp.inf)
            m_t = jnp.maximum(a, jnp.max(d, axis=-1, keepdims=True))
            w_inter = jnp.exp(a - m_t)
            w_intra = jnp.exp(d - m_t)
            qk = _bdot(q, k, ((1,), (1,))) * w_intra
            num = w_inter * _bdot(q, cmat, ((1,), (1,))) + _bdot(qk, v, ((1,), (0,)))
            den = w_inter * jnp.sum(q * nrow, axis=-1, keepdims=True) + jnp.sum(qk, axis=-1, keepdims=True)
            h = num / jnp.maximum(jnp.abs(den), jnp.exp(-m_t))
            hn = h * lax.rsqrt(jnp.mean(h * h, axis=-1, keepdims=True) + EPS) * gain[:, sl]
            hm_ref[pl.ds(r0, L), sl] = hn * jax.nn.sigmoid(o_ref[pl.ds(r0, L), sl])
            m_last = m_t[L - 1:L]
            wl_inter = w_inter[L - 1:L]
            wl = jnp.where(rcol < valid, jnp.exp(b[L - 1:L] - b + ig - m_last), 0.0)
            c_ref[hd] = wl_inter * cmat + _bdot(wl * v, k, ((0,), (0,)))
            n_ref[hd] = wl_inter * nrow + jnp.sum(wl * k, axis=0, keepdims=True)
            m_ref[hd] = jnp.broadcast_to(m_last, (1, dh))
        return carry

    lax.fori_loop(0, tc // L, one_chunk, 0)


def mlstm_layer(z, gate_b, mh_g, c0, n0, m0, seq_len):
    bsz = z.shape[0] // seq_len
    nh, dh = MLSTM_HEADS, MLSTM_DH
    chunk = math.gcd(seq_len, MLSTM_CHUNK)
    valid, padded = chunk, seq_len
    if chunk % SUBLANES:
        assert seq_len < SUBLANES
        chunk = padded = SUBLANES
        z = jnp.pad(z.reshape(bsz, seq_len, -1), ((0, 0), (0, padded - seq_len), (0, 0))).reshape(bsz * padded, -1)
    n = bsz * padded
    tc = min(padded, ROW_TILE)
    steps = padded // tc
    w = nh * dh
    gb = jnp.zeros((1, LANES), jnp.float32).at[0, :2 * nh].set(gate_b)

    def zcol(off, width):
        return pl.BlockSpec((tc, width), lambda b, c: (b * steps + c, off // width))

    def state(shape):
        return pl.BlockSpec((None,) + shape, lambda b, c: (b,) + (0,) * len(shape))

    hm, c1, n1, m1 = pl.pallas_call(
        functools.partial(_mlstm_kernel, chunk, valid),
        grid=(bsz, steps),
        in_specs=[zcol(Z_Q, w), zcol(Z_K, w), zcol(Z_V, w), zcol(Z_O, w), zcol(Z_GATE, LANES),
                  pl.BlockSpec((1, LANES), lambda b, c: (0, 0)),
                  pl.BlockSpec((1, w), lambda b, c: (0, 0)),
                  state((nh, dh, dh)), state((nh, 1, dh)), state((nh, 1, dh))],
        out_specs=[pl.BlockSpec((tc, w), lambda b, c: (b * steps + c, 0)),
                   state((nh, dh, dh)), state((nh, 1, dh)), state((nh, 1, dh))],
        out_shape=[jax.ShapeDtypeStruct((n, w), jnp.float32),
                   jax.ShapeDtypeStruct((bsz, nh, dh, dh), jnp.float32),
                   jax.ShapeDtypeStruct((bsz, nh, 1, dh), jnp.float32),
                   jax.ShapeDtypeStruct((bsz, nh, 1, dh), jnp.float32)],
        compiler_params=pltpu.CompilerParams(dimension_semantics=("arbitrary", "arbitrary")),
        name="mlstm",
    )(z, z, z, z, z, gb, mh_g.reshape(1, w), c0, n0[:, :, None, :],
      jnp.broadcast_to(m0[:, :, None, None], (bsz, nh, 1, dh)))
    if padded != seq_len:
        hm = hm.reshape(bsz, padded, w)[:, :seq_len].reshape(bsz * seq_len, w)
    return hm, c1, n1[:, :, 0, :], m1[:, :, 0, 0]


def _attend(q, kb, vb, dist, visible, slope, sink):
    s = _bdot(q, kb, ((1,), (1,))) * (SWA_DH ** -0.5) - slope * dist.astype(jnp.float32)
    s = jnp.where(visible, s, -jnp.inf)
    mx = jnp.maximum(jnp.max(s, axis=-1, keepdims=True), sink)
    e = jnp.exp(s - mx)
    p = e / (jnp.sum(e, axis=-1, keepdims=True) + jnp.exp(sink - mx))
    return _bdot(p, vb, ((1,), (0,)))


def _swa_banded_kernel(q_ref, kp_ref, kc_ref, vp_ref, vc_ref, sink_ref, o_ref):
    j = pl.program_id(1)
    w = q_ref.shape[0]
    t = lax.broadcasted_iota(jnp.int32, (w, 2 * w), 0)
    i = lax.broadcasted_iota(jnp.int32, (w, 2 * w), 1)
    dist = t + w - i
    visible = (dist >= 0) & (dist <= WINDOW) & ((j > 0) | (i >= w))
    sinks = sink_ref[...]
    for kv in range(SWA_KV_HEADS):
        ks = slice(kv * SWA_DH, (kv + 1) * SWA_DH)
        kb = jnp.concatenate([kp_ref[:, ks], kc_ref[:, ks]], axis=0)
        vb = jnp.concatenate([vp_ref[:, ks], vc_ref[:, ks]], axis=0)
        for g in range(SWA_GROUP):
            h = kv * SWA_GROUP + g
            hs = slice(h * SWA_DH, (h + 1) * SWA_DH)
            o_ref[:, hs] = _attend(q_ref[:, hs], kb, vb, dist, visible, ALIBI[h], sinks[:, h:h + 1])


def swa_banded(z, sinks, seq_len):
    n = z.shape[0]
    w = WINDOW
    nb = seq_len // w
    sink_row = jnp.zeros((1, LANES), jnp.float32).at[0, :SWA_HEADS].set(sinks)

    def cur(off, width):
        return pl.BlockSpec((w, width), lambda b, j: (b * nb + j, off // width))

    def prev(off, width):
        return pl.BlockSpec((w, width), lambda b, j: (b * nb + jnp.maximum(j - 1, 0), off // width))

    return pl.pallas_call(
        _swa_banded_kernel,
        grid=(n // seq_len, nb),
        in_specs=[cur(Z_SQ, SWA_W), prev(Z_SK, SWA_KVW), cur(Z_SK, SWA_KVW), prev(Z_SV, SWA_KVW),
                  cur(Z_SV, SWA_KVW), pl.BlockSpec((1, LANES), lambda b, j: (0, 0))],
        out_specs=pl.BlockSpec((w, SWA_W), lambda b, j: (b * nb + j, 0)),
        out_shape=jax.ShapeDtypeStruct((n, SWA_W), jnp.float32),
        name="swa_banded",
    )(z, z, z, z, z, sink_row)


def _swa_cached_kernel(t_new, n_keys, q_ref, k_ref, v_ref, sink_ref, o_ref):
    m = q_ref.shape[1]
    nkp = k_ref.shape[0]
    r = lax.broadcasted_iota(jnp.int32, (m, nkp), 0)
    i = lax.broadcasted_iota(jnp.int32, (m, nkp), 1)
    first_key_pos = PAST_LEN - (n_keys - t_new)
    dist = (n_keys - t_new) + r % t_new - i
    visible = (dist >= 0) & (dist <= WINDOW) & (i < n_keys) & (first_key_pos + i >= 0)
    rg = lax.broadcasted_iota(jnp.int32, (m, 1), 0) // t_new
    sinks = sink_ref[...]
    for kv in range(SWA_KV_HEADS):
        ks = slice(kv * SWA_DH, (kv + 1) * SWA_DH)
        h0 = kv * SWA_GROUP
        slope = jnp.where(rg == 0, ALIBI[h0], ALIBI[h0 + 1])
        sink = jnp.where(rg == 0, sinks[:, h0:h0 + 1], sinks[:, h0 + 1:h0 + 2])
        o_ref[kv] = _attend(q_ref[kv], k_ref[:, ks], v_ref[:, ks], dist, visible, slope, sink)


def swa_cached(q, k_all, v_all, sinks):
    assert SWA_GROUP == 2
    bsz, t_new, _ = q.shape
    n_keys = k_all.shape[1]
    nkp = -(-n_keys // SUBLANES) * SUBLANES
    pad = ((0, 0), (0, nkp - n_keys), (0, 0))
    k_all, v_all = jnp.pad(k_all, pad), jnp.pad(v_all, pad)
    m = SWA_GROUP * t_new
    qs = q.reshape(bsz, t_new, SWA_KV_HEADS, SWA_GROUP, SWA_DH).transpose(0, 2, 3, 1, 4)
    qs = qs.reshape(bsz, SWA_KV_HEADS, m, SWA_DH)
    sink_row = jnp.zeros((1, LANES), jnp.float32).at[0, :SWA_HEADS].set(sinks)
    o = pl.pallas_call(
        functools.partial(_swa_cached_kernel, t_new, n_keys),
        grid=(bsz,),
        in_specs=[pl.BlockSpec((None, SWA_KV_HEADS, m, SWA_DH), lambda b: (b, 0, 0, 0)),
                  pl.BlockSpec((None, nkp, SWA_KVW), lambda b: (b, 0, 0)),
                  pl.BlockSpec((None, nkp, SWA_KVW), lambda b: (b, 0, 0)),
                  pl.BlockSpec((1, LANES), lambda b: (0, 0))],
        out_specs=pl.BlockSpec((None, SWA_KV_HEADS, m, SWA_DH), lambda b: (b, 0, 0, 0)),
        out_shape=jax.ShapeDtypeStruct((bsz, SWA_KV_HEADS, m, SWA_DH), jnp.float32),
        name="swa_cached",
    )(qs, k_all, v_all, sink_row)
    o = o.reshape(bsz, SWA_KV_HEADS, SWA_GROUP, t_new, SWA_DH).transpose(0, 3, 1, 2, 4)
    return o.reshape(bsz, t_new, SWA_W)


def _mix_out_kernel(seq_len, x_ref, gate_ref, hm_ref, oa_ref, cb_ref, cc_ref, ch_ref, hcc_ref, hch_ref,
                    pa_ref, pb_ref, cw_ref, w_ref, o_ref, u_ref):
    i = pl.program_id(0)
    tm = x_ref.shape[0]
    u = cc_ref[...] * ch_ref[...]
    u_ref[...] = u
    hu = hcc_ref[...] * hch_ref[...]
    r = lax.broadcasted_iota(jnp.int32, u.shape, 0)
    p = (i * tm + r) % seq_len
    pa, pb = pa_ref[...], pb_ref[...]
    last, last2 = hu[SUBLANES - 1:SUBLANES], hu[SUBLANES - 2:SUBLANES - 1]
    u1 = jnp.where(r >= 1, pltpu.roll(u, 1, 0), last)
    u2 = jnp.where(r >= 2, pltpu.roll(u, 2, 0), jnp.where(r == 1, last, last2))
    s1 = jnp.where(p >= 1, u1, pa)
    s2 = jnp.where(p >= 2, u2, jnp.where(p == 1, pa, pb))
    cw = cw_ref[...]
    yc = cb_ref[...] * (cw[0:1] * s2 + cw[1:2] * s1 + cw[2:3] * u)
    cat = jnp.concatenate([hm_ref[...], oa_ref[...], yc], axis=-1)
    mix = jnp.dot(cat.astype(jnp.bfloat16), w_ref[...], preferred_element_type=jnp.float32)
    o_ref[...] = x_ref[...] + gate_ref[...] * mix


def mix_out(x, gate, hm, oa, z, conv_prev, conv_w, w_out_bf, seq_len):
    n, d = x.shape
    c = CONV_W
    tm = min(ROW_TILE, n)
    gate, gate_spec = _row_spec(gate, n, tm, d)
    if seq_len % tm:
        pa = jnp.repeat(conv_prev[:, 1], seq_len, axis=0)
        pb = jnp.repeat(conv_prev[:, 0], seq_len, axis=0)
    else:
        pa, pb = conv_prev[:, 1], conv_prev[:, 0]
    pa, pa_spec = _row_spec(pa, n, tm, c)
    pb, pb_spec = _row_spec(pb, n, tm, c)
    cw = jnp.zeros((SUBLANES, c), jnp.float32).at[:CONV_WIDTH].set(conv_w)

    def zcol(off, width):
        return pl.BlockSpec((tm, width), lambda i: (i, off // width))

    def zhalo(off):
        return pl.BlockSpec((SUBLANES, c), lambda i: (jnp.maximum(i * (tm // SUBLANES) - 1, 0), off // c))

    return pl.pallas_call(
        functools.partial(_mix_out_kernel, seq_len),
        grid=(n // tm,),
        in_specs=[pl.BlockSpec((tm, d), lambda i: (i, 0)), gate_spec,
                  pl.BlockSpec((tm, MLSTM_W), lambda i: (i, 0)),
                  pl.BlockSpec((tm, SWA_W), lambda i: (i, 0)),
                  zcol(Z_CB, c), zcol(Z_CC, c), zcol(Z_CH, c), zhalo(Z_CC), zhalo(Z_CH),
                  pa_spec, pb_spec,
                  pl.BlockSpec((SUBLANES, c), lambda i: (0, 0)),
                  pl.BlockSpec((d, d), lambda i: (0, 0))],
        out_specs=[pl.BlockSpec((tm, d), lambda i: (i, 0)), pl.BlockSpec((tm, c), lambda i: (i, 0))],
        out_shape=[jax.ShapeDtypeStruct((n, d), jnp.float32), jax.ShapeDtypeStruct((n, c), jnp.float32)],
        name="mix_out",
    )(x, gate, hm, oa, z, z, z, z, z, pa, pb, cw, w_out_bf)


def _pair_candidates(s1, i1, s2, i2):
    k, sub = PEER_TOPK, SUBLANES
    assert k == 2 * sub
    row = lax.broadcasted_iota(jnp.int32, (sub, s1.shape[1]), 0)
    sums, ids = [], []

    def emit(a, b0, nvalid):
        c = s1[a:a + 1] + s2[b0:b0 + sub]
        sums.append(c if nvalid >= sub else jnp.where(row < nvalid, c, -jnp.inf))
        ids.append(i1[a:a + 1] * PEER_NKEYS + i2[b0:b0 + sub])

    for a in range(sub):
        nb = k // (a + 1)
        for b0 in range(0, nb, sub):
            emit(a, b0, nb - b0)
    sums.append(s1[sub:] + s2[0:1])
    ids.append(i1[sub:] * PEER_NKEYS + i2[0:1])
    return jnp.concatenate(sums, axis=0), jnp.concatenate(ids, axis=0)


N_CAND = 10 * SUBLANES


def _extract_top(s, rows, payload=None):
    m = jnp.max(s, axis=0, keepdims=True)
    r = jnp.min(jnp.where(s == m, rows, jnp.int32(s.shape[0])), axis=0, keepdims=True)
    hit = rows == r
    ident = r if payload is None else jnp.max(jnp.where(hit, payload, -1), axis=0, keepdims=True)
    return m, ident, jnp.where(hit, -jnp.inf, s)


def _gelu_tanh(x):
    return 0.5 * x * (1.0 + jnp.tanh(math.sqrt(2.0 / math.pi) * (x + 0.044715 * (x * x * x))))


def _peer_kernel(nblk, xr_ref, gain_ref, sc_ref, sh_ref, wq_ref, sk_ref, xres_ref, gate_ref, tab_ref,
                 o_ref, q_scr, h_scr, gt_scr, idx_v, idx_s, s_scr, tv_scr, ti_scr, cv_scr, ci_scr, fv_scr, fi_scr,
                 buf, sem, idx_sem):
    s = pl.program_id(0)
    tb, d = xr_ref.shape
    nch = d // LANES
    assert tb == LANES
    slot_r = s % 2
    slot_e = 1 - slot_r
    per_token_gate = gate_ref.shape[0] == tb

    def row_copy(e, slot, r):
        src = tab_ref.at[pl.ds(pl.multiple_of(e * nch, nch), nch)]
        return pltpu.make_async_copy(src, buf.at[slot, :, r, :], sem.at[slot])

    def issue(t, slot):
        for r in range(PEER_SEL):
            row_copy(idx_s[r, t], slot, r).start(priority=r % 2)

    def wait(slot):
        pltpu.make_async_copy(tab_ref.at[pl.ds(0, PEER_SEL * nch)], buf.at[slot], sem.at[slot]).wait()

    lane = lax.broadcasted_iota(jnp.int32, (PEER_SEL, tb), 1)

    def ffn(t, slot):
        hrow = h_scr[slot_e, pl.ds(t, 1), :]
        part = None
        for c in range(nch):
            u = lax.bitcast_convert_type(buf[slot, c] & jnp.uint32(0xFFFF0000), jnp.float32)
            term = u * hrow[:, c * LANES:(c + 1) * LANES]
            part = term if part is None else part + term
        sdot = jnp.sum(part, axis=-1, keepdims=True)
        g = jnp.sum(jnp.where(lane == t, gt_scr[slot_e], 0.0), axis=-1, keepdims=True)
        w = g * _gelu_tanh(sdot)
        y = jnp.concatenate(
            [jnp.sum(lax.bitcast_convert_type(buf[slot, c] << 16, jnp.float32) * w, axis=0, keepdims=True)
             for c in range(nch)], axis=1)
        gate = gate_ref[pl.ds(t, 1), :] if per_token_gate else gate_ref[...]
        return xres_ref[pl.ds(t, 1), :] + gate * y

    ahead = PEER_NSLOT - 1

    key_rows = lax.broadcasted_iota(jnp.int32, (PEER_NKEYS, tb), 0)
    cand_rows = lax.broadcasted_iota(jnp.int32, (N_CAND, tb), 0)

    def sl_scores(head):
        qh = q_scr[head].astype(jnp.bfloat16)
        for p in range(2):
            s_scr[p] = lax.dot_general(sk_ref[p], qh[:, p * HALF_KEY:(p + 1) * HALF_KEY],
                                       (((1,), (1,)), ((), ())), preferred_element_type=jnp.float32)

    def sl_stage1(k0, cnt, head):
        for p in range(2):
            sc_ = s_scr[p]
            for k in range(k0, k0 + cnt):
                m, r, sc_ = _extract_top(sc_, key_rows)
                tv_scr[p, k:k + 1, :] = m
                ti_scr[p, k:k + 1, :] = r
            s_scr[p] = sc_

    def sl_cand(head):
        c, ci = _pair_candidates(tv_scr[0], ti_scr[0], tv_scr[1], ti_scr[1])
        cv_scr[...] = c
        ci_scr[...] = ci

    def sl_stage2(k0, cnt, head):
        c, ci = cv_scr[...], ci_scr[...]
        for k in range(k0, k0 + cnt):
            m, e, c = _extract_top(c, cand_rows, ci)
            fv_scr[k:k + 1, :] = m
            fi_scr[k:k + 1, :] = e
        cv_scr[...] = c

    def sl_out(head):
        top_s = fv_scr[...]
        e = jnp.exp(top_s - jnp.max(top_s, axis=0, keepdims=True))
        off = pl.multiple_of(head * PEER_TOPK, PEER_TOPK)
        idx_v[pl.ds(off, PEER_TOPK), :] = fi_scr[...]
        gt_scr[slot_r, pl.ds(off, PEER_TOPK), :] = e / jnp.sum(e, axis=0, keepdims=True)

    per_slice = PEER_TOPK // PEER_NSLOT
    slices = [sl_scores]
    slices += [functools.partial(sl_stage1, k0, per_slice) for k0 in range(0, PEER_TOPK, per_slice)]
    slices += [sl_cand]
    k0 = 0
    for cnt in (4, 3, 3, 3, 3):
        slices.append(functools.partial(sl_stage2, k0, cnt))
        k0 += cnt
    slices += [sl_out]
    assert len(slices) == 2 * PEER_NSLOT and k0 == PEER_TOPK

    def group(t0, head, half, last):
        for j in range(PEER_NSLOT):
            t = t0 + j
            wait(j)
            out = ffn(t, j)
            if not last or j == 0:
                issue(t + ahead, (j + ahead) % PEER_NSLOT)
            slices[half * PEER_NSLOT + j](head)
            o_ref[pl.ds(t, 1), :] = out

    def publish_routing():
        cp = pltpu.make_async_copy(idx_v, idx_s, idx_sem.at[0])
        cp.start()
        cp.wait()
        for t0 in range(ahead):
            issue(t0, t0)

    @pl.when(s == 0)
    def _():
        idx_v[...] = jnp.zeros(idx_v.shape, idx_v.dtype)
        h_scr[slot_e] = jnp.zeros(h_scr.shape[1:], h_scr.dtype)
        gt_scr[slot_e] = jnp.zeros(gt_scr.shape[1:], gt_scr.dtype)
        publish_routing()

    h = _norm_mod(xr_ref[...], gain_ref[...], sc_ref[...], sh_ref[...])
    h_scr[slot_r] = h
    q = jnp.dot(h.astype(jnp.bfloat16), wq_ref[...], preferred_element_type=jnp.float32)
    for head in range(PEER_HEADS):
        q_scr[head] = q[:, head * PEER_DKEY:(head + 1) * PEER_DKEY]

    n_pairs = tb // (2 * PEER_NSLOT)
    assert n_pairs == PEER_HEADS

    def pair(gi, last):
        t0 = pl.multiple_of(gi * (2 * PEER_NSLOT), 2 * PEER_NSLOT)
        group(t0, gi, 0, False)
        group(t0 + PEER_NSLOT, gi, 1, last)

    def body(gi, carry):
        pair(gi, False)
        return carry

    lax.fori_loop(0, n_pairs - 1, body, 0)
    pair(n_pairs - 1, True)

    @pl.when(s < nblk)
    def _():
        publish_routing()


def pack_expert_table(u, v):
    ub = lax.bitcast_convert_type(u.astype(jnp.bfloat16), jnp.uint16).astype(jnp.uint32)
    vb = lax.bitcast_convert_type(v.astype(jnp.bfloat16), jnp.uint16).astype(jnp.uint32)
    return (ub << 16) | vb


def peer_layer(x, gain, sc, sh, gate, wq_bf, sk_bf, tab):
    n, d = x.shape
    tb = PEER_TB
    nblk = n // tb
    nch = d // LANES

    def cur(s):
        return jnp.minimum(s, nblk - 1)

    def prev(s):
        return jnp.maximum(s - 1, 0)

    def rows(arr, blk):
        g = arr.shape[0]
        if g == n:
            return arr, pl.BlockSpec((tb, d), lambda s: (blk(s), 0))
        per = n // g
        return arr[:, None, :], pl.BlockSpec((None, 1, d), lambda s: (blk(s) * tb // per, 0, 0))

    sc, sc_spec = rows(sc, cur)
    sh, sh_spec = rows(sh, cur)
    gate, gate_spec = rows(gate, prev)
    return pl.pallas_call(
        functools.partial(_peer_kernel, nblk),
        grid=(nblk + 1,),
        in_specs=[pl.BlockSpec((tb, d), lambda s: (cur(s), 0)),
                  pl.BlockSpec((1, d), lambda s: (0, 0)),
                  sc_spec, sh_spec,
                  pl.BlockSpec(wq_bf.shape, lambda s: (0, 0)),
                  pl.BlockSpec(sk_bf.shape, lambda s: (0, 0, 0)),
                  pl.BlockSpec((tb, d), lambda s: (prev(s), 0)),
                  gate_spec,
                  pl.BlockSpec(memory_space=pl.ANY)],
        out_specs=pl.BlockSpec((tb, d), lambda s: (prev(s), 0)),
        out_shape=jax.ShapeDtypeStruct((n, d), jnp.float32),
        scratch_shapes=[pltpu.VMEM((PEER_HEADS, tb, PEER_DKEY), jnp.float32),
                        pltpu.VMEM((2, tb, d), jnp.float32),
                        pltpu.VMEM((2, PEER_SEL, tb), jnp.float32),
                        pltpu.VMEM((PEER_SEL, tb), jnp.int32),
                        pltpu.SMEM((PEER_SEL, tb), jnp.int32),
                        pltpu.VMEM((2, PEER_NKEYS, tb), jnp.float32),
                        pltpu.VMEM((2, PEER_TOPK, tb), jnp.float32),
                        pltpu.VMEM((2, PEER_TOPK, tb), jnp.int32),
                        pltpu.VMEM((N_CAND, tb), jnp.float32),
                        pltpu.VMEM((N_CAND, tb), jnp.int32),
                        pltpu.VMEM((PEER_TOPK, tb), jnp.float32),
                        pltpu.VMEM((PEER_TOPK, tb), jnp.int32),
                        pltpu.VMEM((PEER_NSLOT, nch, PEER_SEL, LANES), jnp.uint32),
                        pltpu.SemaphoreType.DMA((PEER_NSLOT,)),
                        pltpu.SemaphoreType.DMA((1,))],
        compiler_params=pltpu.CompilerParams(dimension_semantics=("arbitrary",)),
        name="peer",
    )(x, gain, sc, sh, wq_bf, sk_bf, x, gate, tab.reshape(-1, LANES))


def _final_norm_kernel(x_ref, g_ref, o_ref):
    x = x_ref[...]
    o_ref[...] = x * lax.rsqrt(jnp.mean(x * x, axis=-1, keepdims=True) + EPS) * g_ref[...]


def final_norm(x, g):
    n, d = x.shape
    tm = min(ROW_TILE, n)
    return pl.pallas_call(
        _final_norm_kernel,
        grid=(n // tm,),
        in_specs=[pl.BlockSpec((tm, d), lambda i: (i, 0)), pl.BlockSpec((1, d), lambda i: (0, 0))],
        out_specs=pl.BlockSpec((tm, d), lambda i: (i, 0)),
        out_shape=jax.ShapeDtypeStruct((n, d), x.dtype),
        name="final_norm",
    )(x, g.reshape(1, d))


def _per_tile_rows(m, seq_len, tile):
    return m if seq_len % tile == 0 else jnp.repeat(m, seq_len, axis=0)


def trunk_layer(x, seq_len, mod, p, state):
    n, d = x.shape
    bsz = n // seq_len
    sh1, sc1, gt1, sh2, sc2, gt2 = (_per_tile_rows(m, seq_len, ROW_TILE) for m in jnp.split(mod, 6, axis=-1))
    z = mix_in(x, p['g_mix'], sc1, sh1, p['w_in'])
    k_new = z[:, Z_SK:Z_SK + SWA_KVW].reshape(bsz, seq_len, SWA_KVW)
    v_new = z[:, Z_SV:Z_SV + SWA_KVW].reshape(bsz, seq_len, SWA_KVW)
    if state is None:
        win_rows = min(WINDOW, PAST_LEN)
        c0 = jnp.zeros((bsz, MLSTM_HEADS, MLSTM_DH, MLSTM_DH), jnp.float32)
        n0 = jnp.zeros((bsz, MLSTM_HEADS, MLSTM_DH), jnp.float32)
        m0 = jnp.zeros((bsz, MLSTM_HEADS), jnp.float32)
        cbuf = jnp.zeros((bsz, CONV_WIDTH - 1, CONV_W), jnp.float32)
        oa = swa_banded(z, p['sinks'], seq_len)
        k_all, v_all = k_new, v_new
    else:
        kbuf, vbuf, cbuf, c0, n0, m0 = state
        win_rows = kbuf.shape[1]
        k_all = jnp.concatenate([kbuf.reshape(bsz, win_rows, SWA_KVW), k_new], axis=1)
        v_all = jnp.concatenate([vbuf.reshape(bsz, win_rows, SWA_KVW), v_new], axis=1)
        q = z[:, Z_SQ:Z_SQ + SWA_W].reshape(bsz, seq_len, SWA_W)
        oa = swa_cached(q, k_all, v_all, p['sinks']).reshape(n, SWA_W)
    hm, c1, n1, m1 = mlstm_layer(z, p['gate_b'], p['mh_g'], c0, n0, m0, seq_len)
    x, u = mix_out(x, gt1, hm, oa, z, cbuf, p['conv_w'], p['w_out'], seq_len)
    k_keep = k_all[:, -win_rows:].reshape(bsz, win_rows, SWA_KV_HEADS, SWA_DH)
    v_keep = v_all[:, -win_rows:].reshape(bsz, win_rows, SWA_KV_HEADS, SWA_DH)
    cbuf_new = jnp.concatenate([cbuf, u.reshape(bsz, seq_len, CONV_W)], axis=1)[:, -(CONV_WIDTH - 1):]
    x = peer_layer(x, p['g_ffn'], sc2, sh2, gt2, p['wq'], p['subkeys'], p['tab'])
    return x, (k_keep, v_keep, cbuf_new, c1, n1, m1)


def kernel(x_prompt, x_sample, cache_swa_k, cache_swa_v, state_conv, state_mlstm_C, state_mlstm_n, state_mlstm_m, c_prompt, c_sample, ada_w, ada_b, norm_mix_g, norm_ffn_g, w_in, w_out, mlstm_gate_b, mlstm_norm_g, swa_sinks, conv_w, peer_wq, peer_subkeys, peer_u, peer_v, final_g):
    bp, tp, d = x_prompt.shape
    bs, ts, _ = x_sample.shape
    xp, xs = x_prompt.reshape(bp * tp, d), x_sample.reshape(bs * ts, d)
    c_all = jnp.concatenate([c_prompt, c_sample], axis=0)
    bf = jnp.bfloat16
    n_gate = 2 * MLSTM_HEADS
    new_p, new_s = [], []
    for l in range(DEPTH):
        wl = w_in[l]
        w_perm = jnp.concatenate([wl[:, :Z_SQ], wl[:, Z_SQ + n_gate:], wl[:, Z_SQ:Z_SQ + n_gate],
                                  jnp.zeros((d, LANES - n_gate), wl.dtype)], axis=1)
        p = dict(g_mix=norm_mix_g[l].reshape(1, d), g_ffn=norm_ffn_g[l].reshape(1, d),
                 w_in=w_perm.astype(bf), w_out=w_out[l].astype(bf), gate_b=mlstm_gate_b[l],
                 mh_g=mlstm_norm_g[l], sinks=swa_sinks[l], conv_w=conv_w[l], wq=peer_wq[l].astype(bf),
                 subkeys=peer_subkeys[l].astype(bf), tab=pack_expert_table(peer_u[l], peer_v[l]))
        mod = adaln_mod(c_all, ada_w[l], ada_b[l])
        xp, sp = trunk_layer(xp, tp, mod[:bp], p, None)
        st = (cache_swa_k[l], cache_swa_v[l], state_conv[l], state_mlstm_C[l], state_mlstm_n[l], state_mlstm_m[l])
        xs, ss = trunk_layer(xs, ts, mod[bp:], p, st)
        new_p.append(sp)
        new_s.append(ss)
    y_prompt = final_norm(xp, final_g).reshape(bp, tp, d)
    y_sample = final_norm(xs, final_g).reshape(bs, ts, d)
    pk, pv, pc, pC, pn, pm = [jnp.stack(t) for t in zip(*new_p)]
    sk, sv, sc, sC, sn, sm = [jnp.stack(t) for t in zip(*new_s)]
    return (y_prompt, y_sample, pk, pv, pc, pC, pn, pm, sk, sv, sc, sC, sn, sm)
```

```python
import functools
import math

import jax
import jax.numpy as jnp
from jax import lax
from jax.experimental import pallas as pl
from jax.experimental.pallas import tpu as pltpu

D_MODEL = 1024
DEPTH = 2
PAST_LEN = 16384

MLSTM_W = D_MODEL // 2
MLSTM_HEADS = 4
MLSTM_DH = MLSTM_W // MLSTM_HEADS
MLSTM_CHUNK = 64
SWA_W = D_MODEL // 4
SWA_DH = 64
SWA_HEADS = SWA_W // SWA_DH
SWA_KV_HEADS = SWA_HEADS // 2
SWA_GROUP = SWA_HEADS // SWA_KV_HEADS
SWA_KVW = SWA_KV_HEADS * SWA_DH
WINDOW = 128
CONV_W = D_MODEL - MLSTM_W - SWA_W
CONV_WIDTH = 3
PEER_HEADS = 8
PEER_NKEYS = 128
PEER_DKEY = 256
HALF_KEY = PEER_DKEY // 2
PEER_TOPK = 16
PEER_SEL = PEER_HEADS * PEER_TOPK
EPS = 1e-6

LANES = 128
SUBLANES = 8
ROW_TILE = 512
PEER_TB = LANES
PEER_NSLOT = 8

Z_Q, Z_K, Z_V, Z_O = 0, MLSTM_W, 2 * MLSTM_W, 3 * MLSTM_W
Z_SQ = 4 * MLSTM_W
Z_SK = Z_SQ + SWA_W
Z_SV = Z_SK + SWA_KVW
Z_CB = Z_SV + SWA_KVW
Z_CC = Z_CB + CONV_W
Z_CH = Z_CC + CONV_W
Z_GATE = Z_CH + CONV_W
Z_W = Z_GATE + LANES
ALIBI = tuple(2.0 ** (-8.0 * h / SWA_HEADS) for h in range(1, SWA_HEADS + 1))


def _bdot(a, b, dims):
    return lax.dot_general(a.astype(jnp.bfloat16), b.astype(jnp.bfloat16), (dims, ((), ())),
                           preferred_element_type=jnp.float32)


def _row_spec(arr, n, tm, d):
    g = arr.shape[0]
    if g == n:
        return arr, pl.BlockSpec((tm, d), lambda i, *_: (i, 0))
    per = n // g
    return arr[:, None, :], pl.BlockSpec((None, 1, d), lambda i, *_: (i * tm // per, 0, 0))


def _norm_mod(x, gain, sc, sh):
    y = x * lax.rsqrt(jnp.mean(x * x, axis=-1, keepdims=True) + EPS) * gain
    return y * (1.0 + sc) + sh


def _adaln_kernel(c_ref, w_ref, b_ref, o_ref):
    c = c_ref[...]
    o_ref[...] = _bdot(c * jax.nn.sigmoid(c), w_ref[...], ((1,), (0,))) + b_ref[...]


def adaln_mod(c, w, b):
    bsz, d = c.shape
    m = w.shape[1]
    tn = 512
    return pl.pallas_call(
        _adaln_kernel,
        grid=(m // tn,),
        in_specs=[pl.BlockSpec((bsz, d), lambda j: (0, 0)),
                  pl.BlockSpec((d, tn), lambda j: (0, j)),
                  pl.BlockSpec((1, tn), lambda j: (0, j))],
        out_specs=pl.BlockSpec((bsz, tn), lambda j: (0, j)),
        out_shape=jax.ShapeDtypeStruct((bsz, m), jnp.float32),
        name="adaln_mod",
    )(c, w, b.reshape(1, m))


def _mix_in_kernel(x_ref, gain_ref, sc_ref, sh_ref, w_ref, z_ref):
    h = _norm_mod(x_ref[...], gain_ref[...], sc_ref[...], sh_ref[...])
    z_ref[...] = jnp.dot(h.astype(jnp.bfloat16), w_ref[...], preferred_element_type=jnp.float32)


def mix_in(x, gain, sc, sh, w_bf):
    n, d = x.shape
    zw = w_bf.shape[1]
    tm, tn = min(ROW_TILE, n), zw // 3
    sc, sc_spec = _row_spec(sc, n, tm, d)
    sh, sh_spec = _row_spec(sh, n, tm, d)
    return pl.pallas_call(
        _mix_in_kernel,
        grid=(n // tm, zw // tn),
        in_specs=[pl.BlockSpec((tm, d), lambda i, j: (i, 0)),
                  pl.BlockSpec((1, d), lambda i, j: (0, 0)),
                  sc_spec, sh_spec,
                  pl.BlockSpec((d, tn), lambda i, j: (0, j))],
        out_specs=pl.BlockSpec((tm, tn), lambda i, j: (i, j)),
        out_shape=jax.ShapeDtypeStruct((n, zw), jnp.float32),
        name="mix_in",
    )(x, gain, sc, sh, w_bf)


def _log_sigmoid(x):
    return jnp.minimum(x, 0.0) - jnp.log(1.0 + jnp.exp(-jnp.abs(x)))


def _mlstm_kernel(chunk, valid, q_ref, k_ref, v_ref, o_ref, gate_ref, gb_ref, g_ref, c0_ref, n0_ref, m0_ref,
                  hm_ref, c_ref, n_ref, m_ref):
    tc = q_ref.shape[0]
    nh, dh, L = MLSTM_HEADS, MLSTM_DH, chunk

    @pl.when(pl.program_id(1) == 0)
    def _():
        c_ref[...] = c0_ref[...]
        n_ref[...] = n0_ref[...]
        m_ref[...] = m0_ref[...]

    row = lax.broadcasted_iota(jnp.int32, (L, L), 0)
    col = lax.broadcasted_iota(jnp.int32, (L, L), 1)
    eye = row == col
    visible = (col <= row) & (col < valid)
    rcol = lax.broadcasted_iota(jnp.int32, (L, 1), 0)
    gb = gb_ref[...]
    gain = g_ref[...]

    def to_row(x_col):
        return jnp.sum(jnp.where(eye, x_col, 0.0), axis=0, keepdims=True)

    def one_chunk(ci, carry):
        r0 = pl.multiple_of(ci * L, L)
        pre = gate_ref[pl.ds(r0, L), :] + gb
        for hd in range(nh):
            sl = slice(hd * dh, (hd + 1) * dh)
            q = q_ref[pl.ds(r0, L), sl]
            k = k_ref[pl.ds(r0, L), sl] * (dh ** -0.5)
            v = v_ref[pl.ds(r0, L), sl]
            ig = pre[:, hd:hd + 1]
            lf = jnp.where(rcol < valid, _log_sigmoid(pre[:, nh + hd:nh + hd + 1]), 0.0)
            b = lf
            s = 1
            while s < L:
                b = b + jnp.where(rcol >= s, pltpu.roll(b, s, 0), 0.0)
                s *= 2
            cmat, nrow, m_prev = c_ref[hd], n_ref[hd], m_ref[hd][:, 0:1]
            a = b + m_prev
            d = jnp.where(visible, b - to_row(b) + to_row(ig), -jnp.inf)
            m_t = jnp.maximum(a, jnp.max(d, axis=-1, keepdims=True))
            w_inter = jnp.exp(a - m_t)
            w_intra = jnp.exp(d - m_t)
            qk = _bdot(q, k, ((1,), (1,))) * w_intra
            num = w_inter * _bdot(q, cmat, ((1,), (1,))) + _bdot(qk, v, ((1,), (0,)))
            den = w_inter * jnp.sum(q * nrow, axis=-1, keepdims=True) + jnp.sum(qk, axis=-1, keepdims=True)
            h = num / jnp.maximum(jnp.abs(den), jnp.exp(-m_t))
            hn = h * lax.rsqrt(jnp.mean(h * h, axis=-1, keepdims=True) + EPS) * gain[:, sl]
            hm_ref[pl.ds(r0, L), sl] = hn * jax.nn.sigmoid(o_ref[pl.ds(r0, L), sl])
            m_last = m_t[L - 1:L]
            wl_inter = w_inter[L - 1:L]
            wl = jnp.where(rcol < valid, jnp.exp(b[L - 1:L] - b + ig - m_last), 0.0)
            c_ref[hd] = wl_inter * cmat + _bdot(wl * v, k, ((0,), (0,)))
            n_ref[hd] = wl_inter * nrow + jnp.sum(wl * k, axis=0, keepdims=True)
            m_ref[hd] = jnp.broadcast_to(m_last, (1, dh))
        return carry

    lax.fori_loop(0, tc // L, one_chunk, 0)


def mlstm_layer(z, gate_b, mh_g, c0, n0, m0, seq_len):
    bsz = z.shape[0] // seq_len
    nh, dh = MLSTM_HEADS, MLSTM_DH
    chunk = math.gcd(seq_len, MLSTM_CHUNK)
    valid, padded = chunk, seq_len
    if chunk % SUBLANES:
        assert seq_len < SUBLANES
        chunk = padded = SUBLANES
        z = jnp.pad(z.reshape(bsz, seq_len, -1), ((0, 0), (0, padded - seq_len), (0, 0))).reshape(bsz * padded, -1)
    n = bsz * padded
    tc = min(padded, ROW_TILE)
    steps = padded // tc
    w = nh * dh
    gb = jnp.zeros((1, LANES), jnp.float32).at[0, :2 * nh].set(gate_b)

    def zcol(off, width):
        return pl.BlockSpec((tc, width), lambda b, c: (b * steps + c, off // width))

    def state(shape):
        return pl.BlockSpec((None,) + shape, lambda b, c: (b,) + (0,) * len(shape))

    hm, c1, n1, m1 = pl.pallas_call(
        functools.partial(_mlstm_kernel, chunk, valid),
        grid=(bsz, steps),
        in_specs=[zcol(Z_Q, w), zcol(Z_K, w), zcol(Z_V, w), zcol(Z_O, w), zcol(Z_GATE, LANES),
                  pl.BlockSpec((1, LANES), lambda b, c: (0, 0)),
                  pl.BlockSpec((1, w), lambda b, c: (0, 0)),
                  state((nh, dh, dh)), state((nh, 1, dh)), state((nh, 1, dh))],
        out_specs=[pl.BlockSpec((tc, w), lambda b, c: (b * steps + c, 0)),
                   state((nh, dh, dh)), state((nh, 1, dh)), state((nh, 1, dh))],
        out_shape=[jax.ShapeDtypeStruct((n, w), jnp.float32),
                   jax.ShapeDtypeStruct((bsz, nh, dh, dh), jnp.float32),
                   jax.ShapeDtypeStruct((bsz, nh, 1, dh), jnp.float32),
                   jax.ShapeDtypeStruct((bsz, nh, 1, dh), jnp.float32)],
        compiler_params=pltpu.CompilerParams(dimension_semantics=("arbitrary", "arbitrary")),
        name="mlstm",
    )(z, z, z, z, z, gb, mh_g.reshape(1, w), c0, n0[:, :, None, :],
      jnp.broadcast_to(m0[:, :, None, None], (bsz, nh, 1, dh)))
    if padded != seq_len:
        hm = hm.reshape(bsz, padded, w)[:, :seq_len].reshape(bsz * seq_len, w)
    return hm, c1, n1[:, :, 0, :], m1[:, :, 0, 0]


def _attend(q, kb, vb, dist, visible, slope, sink):
    s = _bdot(q, kb, ((1,), (1,))) * (SWA_DH ** -0.5) - slope * dist.astype(jnp.float32)
    s = jnp.where(visible, s, -jnp.inf)
    mx = jnp.maximum(jnp.max(s, axis=-1, keepdims=True), sink)
    e = jnp.exp(s - mx)
    p = e / (jnp.sum(e, axis=-1, keepdims=True) + jnp.exp(sink - mx))
    return _bdot(p, vb, ((1,), (0,)))


def _swa_banded_kernel(q_ref, kp_ref, kc_ref, vp_ref, vc_ref, sink_ref, o_ref):
    j = pl.program_id(1)
    w = q_ref.shape[0]
    t = lax.broadcasted_iota(jnp.int32, (w, 2 * w), 0)
    i = lax.broadcasted_iota(jnp.int32, (w, 2 * w), 1)
    dist = t + w - i
    visible = (dist >= 0) & (dist <= WINDOW) & ((j > 0) | (i >= w))
    sinks = sink_ref[...]
    for kv in range(SWA_KV_HEADS):
        ks = slice(kv * SWA_DH, (kv + 1) * SWA_DH)
        kb = jnp.concatenate([kp_ref[:, ks], kc_ref[:, ks]], axis=0)
        vb = jnp.concatenate([vp_ref[:, ks], vc_ref[:, ks]], axis=0)
        for g in range(SWA_GROUP):
            h = kv * SWA_GROUP + g
            hs = slice(h * SWA_DH, (h + 1) * SWA_DH)
            o_ref[:, hs] = _attend(q_ref[:, hs], kb, vb, dist, visible, ALIBI[h], sinks[:, h:h + 1])


def swa_banded(z, sinks, seq_len):
    n = z.shape[0]
    w = WINDOW
    nb = seq_len // w
    sink_row = jnp.zeros((1, LANES), jnp.float32).at[0, :SWA_HEADS].set(sinks)

    def cur(off, width):
        return pl.BlockSpec((w, width), lambda b, j: (b * nb + j, off // width))

    def prev(off, width):
        return pl.BlockSpec((w, width), lambda b, j: (b * nb + jnp.maximum(j - 1, 0), off // width))

    return pl.pallas_call(
        _swa_banded_kernel,
        grid=(n // seq_len, nb),
        in_specs=[cur(Z_SQ, SWA_W), prev(Z_SK, SWA_KVW), cur(Z_SK, SWA_KVW), prev(Z_SV, SWA_KVW),
                  cur(Z_SV, SWA_KVW), pl.BlockSpec((1, LANES), lambda b, j: (0, 0))],
        out_specs=pl.BlockSpec((w, SWA_W), lambda b, j: (b * nb + j, 0)),
        out_shape=jax.ShapeDtypeStruct((n, SWA_W), jnp.float32),
        name="swa_banded",
    )(z, z, z, z, z, sink_row)


def _swa_cached_kernel(t_new, n_keys, q_ref, k_ref, v_ref, sink_ref, o_ref):
    m = q_ref.shape[1]
    nkp = k_ref.shape[0]
    r = lax.broadcasted_iota(jnp.int32, (m, nkp), 0)
    i = lax.broadcasted_iota(jnp.int32, (m, nkp), 1)
    first_key_pos = PAST_LEN - (n_keys - t_new)
    dist = (n_keys - t_new) + r % t_new - i
    visible = (dist >= 0) & (dist <= WINDOW) & (i < n_keys) & (first_key_pos + i >= 0)
    rg = lax.broadcasted_iota(jnp.int32, (m, 1), 0) // t_new
    sinks = sink_ref[...]
    for kv in range(SWA_KV_HEADS):
        ks = slice(kv * SWA_DH, (kv + 1) * SWA_DH)
        h0 = kv * SWA_GROUP
        slope = jnp.where(rg == 0, ALIBI[h0], ALIBI[h0 + 1])
        sink = jnp.where(rg == 0, sinks[:, h0:h0 + 1], sinks[:, h0 + 1:h0 + 2])
        o_ref[kv] = _attend(q_ref[kv], k_ref[:, ks], v_ref[:, ks], dist, visible, slope, sink)


def swa_cached(q, k_all, v_all, sinks):
    assert SWA_GROUP == 2
    bsz, t_new, _ = q.shape
    n_keys = k_all.shape[1]
    nkp = -(-n_keys // SUBLANES) * SUBLANES
    pad = ((0, 0), (0, nkp - n_keys), (0, 0))
    k_all, v_all = jnp.pad(k_all, pad), jnp.pad(v_all, pad)
    m = SWA_GROUP * t_new
    qs = q.reshape(bsz, t_new, SWA_KV_HEADS, SWA_GROUP, SWA_DH).transpose(0, 2, 3, 1, 4)
    qs = qs.reshape(bsz, SWA_KV_HEADS, m, SWA_DH)
    sink_row = jnp.zeros((1, LANES), jnp.float32).at[0, :SWA_HEADS].set(sinks)
    o = pl.pallas_call(
        functools.partial(_swa_cached_kernel, t_new, n_keys),
        grid=(bsz,),
        in_specs=[pl.BlockSpec((None, SWA_KV_HEADS, m, SWA_DH), lambda b: (b, 0, 0, 0)),
                  pl.BlockSpec((None, nkp, SWA_KVW), lambda b: (b, 0, 0)),
                  pl.BlockSpec((None, nkp, SWA_KVW), lambda b: (b, 0, 0)),
                  pl.BlockSpec((1, LANES), lambda b: (0, 0))],
        out_specs=pl.BlockSpec((None, SWA_KV_HEADS, m, SWA_DH), lambda b: (b, 0, 0, 0)),
        out_shape=jax.ShapeDtypeStruct((bsz, SWA_KV_HEADS, m, SWA_DH), jnp.float32),
        name="swa_cached",
    )(qs, k_all, v_all, sink_row)
    o = o.reshape(bsz, SWA_KV_HEADS, SWA_GROUP, t_new, SWA_DH).transpose(0, 3, 1, 2, 4)
    return o.reshape(bsz, t_new, SWA_W)


def _mix_out_kernel(seq_len, x_ref, gate_ref, hm_ref, oa_ref, cb_ref, cc_ref, ch_ref, hcc_ref, hch_ref,
                    pa_ref, pb_ref, cw_ref, w_ref, o_ref, u_ref):
    i = pl.program_id(0)
    tm = x_ref.shape[0]
    u = cc_ref[...] * ch_ref[...]
    u_ref[...] = u
    hu = hcc_ref[...] * hch_ref[...]
    r = lax.broadcasted_iota(jnp.int32, u.shape, 0)
    p = (i * tm + r) % seq_len
    pa, pb = pa_ref[...], pb_ref[...]
    last, last2 = hu[SUBLANES - 1:SUBLANES], hu[SUBLANES - 2:SUBLANES - 1]
    u1 = jnp.where(r >= 1, pltpu.roll(u, 1, 0), last)
    u2 = jnp.where(r >= 2, pltpu.roll(u, 2, 0), jnp.where(r == 1, last, last2))
    s1 = jnp.where(p >= 1, u1, pa)
    s2 = jnp.where(p >= 2, u2, jnp.where(p == 1, pa, pb))
    cw = cw_ref[...]
    yc = cb_ref[...] * (cw[0:1] * s2 + cw[1:2] * s1 + cw[2:3] * u)
    cat = jnp.concatenate([hm_ref[...], oa_ref[...], yc], axis=-1)
    mix = jnp.dot(cat.astype(jnp.bfloat16), w_ref[...], preferred_element_type=jnp.float32)
    o_ref[...] = x_ref[...] + gate_ref[...] * mix


def mix_out(x, gate, hm, oa, z, conv_prev, conv_w, w_out_bf, seq_len):
    n, d = x.shape
    c = CONV_W
    tm = min(ROW_TILE, n)
    gate, gate_spec = _row_spec(gate, n, tm, d)
    if seq_len % tm:
        pa = jnp.repeat(conv_prev[:, 1], seq_len, axis=0)
        pb = jnp.repeat(conv_prev[:, 0], seq_len, axis=0)
    else:
        pa, pb = conv_prev[:, 1], conv_prev[:, 0]
    pa, pa_spec = _row_spec(pa, n, tm, c)
    pb, pb_spec = _row_spec(pb, n, tm, c)
    cw = jnp.zeros((SUBLANES, c), jnp.float32).at[:CONV_WIDTH].set(conv_w)

    def zcol(off, width):
        return pl.BlockSpec((tm, width), lambda i: (i, off // width))

    def zhalo(off):
        return pl.BlockSpec((SUBLANES, c), lambda i: (jnp.maximum(i * (tm // SUBLANES) - 1, 0), off // c))

    return pl.pallas_call(
        functools.partial(_mix_out_kernel, seq_len),
        grid=(n // tm,),
        in_specs=[pl.BlockSpec((tm, d), lambda i: (i, 0)), gate_spec,
                  pl.BlockSpec((tm, MLSTM_W), lambda i: (i, 0)),
                  pl.BlockSpec((tm, SWA_W), lambda i: (i, 0)),
                  zcol(Z_CB, c), zcol(Z_CC, c), zcol(Z_CH, c), zhalo(Z_CC), zhalo(Z_CH),
                  pa_spec, pb_spec,
                  pl.BlockSpec((SUBLANES, c), lambda i: (0, 0)),
                  pl.BlockSpec((d, d), lambda i: (0, 0))],
        out_specs=[pl.BlockSpec((tm, d), lambda i: (i, 0)), pl.BlockSpec((tm, c), lambda i: (i, 0))],
        out_shape=[jax.ShapeDtypeStruct((n, d), jnp.float32), jax.ShapeDtypeStruct((n, c), jnp.float32)],
        name="mix_out",
    )(x, gate, hm, oa, z, z, z, z, z, pa, pb, cw, w_out_bf)


def _pair_candidates(s1, i1, s2, i2):
    k, sub = PEER_TOPK, SUBLANES
    assert k == 2 * sub
    row = lax.broadcasted_iota(jnp.int32, (sub, s1.shape[1]), 0)
    sums, ids = [], []

    def emit(a, b0, nvalid):
        c = s1[a:a + 1] + s2[b0:b0 + sub]
        sums.append(c if nvalid >= sub else jnp.where(row < nvalid, c, -jnp.inf))
        ids.append(i1[a:a + 1] * PEER_NKEYS + i2[b0:b0 + sub])

    for a in range(sub):
        nb = k // (a + 1)
        for b0 in range(0, nb, sub):
            emit(a, b0, nb - b0)
    sums.append(s1[sub:] + s2[0:1])
    ids.append(i1[sub:] * PEER_NKEYS + i2[0:1])
    return jnp.concatenate(sums, axis=0), jnp.concatenate(ids, axis=0)


N_CAND = 10 * SUBLANES


def _extract_top(s, rows, payload=None):
    m = jnp.max(s, axis=0, keepdims=True)
    r = jnp.min(jnp.where(s == m, rows, jnp.int32(s.shape[0])), axis=0, keepdims=True)
    hit = rows == r
    ident = r if payload is None else jnp.max(jnp.where(hit, payload, -1), axis=0, keepdims=True)
    return m, ident, jnp.where(hit, -jnp.inf, s)


def _gelu_tanh(x):
    return 0.5 * x * (1.0 + jnp.tanh(math.sqrt(2.0 / math.pi) * (x + 0.044715 * (x * x * x))))


def _peer_kernel(nblk, xr_ref, gain_ref, sc_ref, sh_ref, wq_ref, sk_ref, xres_ref, gate_ref, tab_ref,
                 o_ref, q_scr, h_scr, gt_scr, idx_v, idx_s, s_scr, tv_scr, ti_scr, cv_scr, ci_scr, fv_scr, fi_scr,
                 buf, sem, idx_sem):
    s = pl.program_id(0)
    tb, d = xr_ref.shape
    nch = d // LANES
    assert tb == LANES
    slot_r = s % 2
    slot_e = 1 - slot_r
    per_token_gate = gate_ref.shape[0] == tb

    def row_copy(e, slot, r):
        src = tab_ref.at[pl.ds(pl.multiple_of(e * nch, nch), nch)]
        return pltpu.make_async_copy(src, buf.at[slot, :, r, :], sem.at[slot])

    def issue(t, slot):
        for r in range(PEER_SEL):
            row_copy(idx_s[r, t], slot, r).start(priority=r % 2)

    def wait(slot):
        pltpu.make_async_copy(tab_ref.at[pl.ds(0, PEER_SEL * nch)], buf.at[slot], sem.at[slot]).wait()

    lane = lax.broadcasted_iota(jnp.int32, (PEER_SEL, tb), 1)

    def ffn(t, slot):
        hrow = h_scr[slot_e, pl.ds(t, 1), :]
        part = None
        for c in range(nch):
            u = lax.bitcast_convert_type(buf[slot, c] & jnp.uint32(0xFFFF0000), jnp.float32)
            term = u * hrow[:, c * LANES:(c + 1) * LANES]
            part = term if part is None else part + term
        sdot = jnp.sum(part, axis=-1, keepdims=True)
        g = jnp.sum(jnp.where(lane == t, gt_scr[slot_e], 0.0), axis=-1, keepdims=True)
        w = g * _gelu_tanh(sdot)
        y = jnp.concatenate(
            [jnp.sum(lax.bitcast_convert_type(buf[slot, c] << 16, jnp.float32) * w, axis=0, keepdims=True)
             for c in range(nch)], axis=1)
        gate = gate_ref[pl.ds(t, 1), :] if per_token_gate else gate_ref[...]
        return xres_ref[pl.ds(t, 1), :] + gate * y

    ahead = PEER_NSLOT - 1

    key_rows = lax.broadcasted_iota(jnp.int32, (PEER_NKEYS, tb), 0)
    cand_rows = lax.broadcasted_iota(jnp.int32, (N_CAND, tb), 0)

    def sl_scores(head):
        qh = q_scr[head].astype(jnp.bfloat16)
        for p in range(2):
            s_scr[p] = lax.dot_general(sk_ref[p], qh[:, p * HALF_KEY:(p + 1) * HALF_KEY],
                                       (((1,), (1,)), ((), ())), preferred_element_type=jnp.float32)

    def sl_stage1(k0, cnt, head):
        for p in range(2):
            sc_ = s_scr[p]
            for k in range(k0, k0 + cnt):
                m, r, sc_ = _extract_top(sc_, key_rows)
                tv_scr[p, k:k + 1, :] = m
                ti_scr[p, k:k + 1, :] = r
            s_scr[p] = sc_

    def sl_cand(head):
        c, ci = _pair_candidates(tv_scr[0], ti_scr[0], tv_scr[1], ti_scr[1])
        cv_scr[...] = c
        ci_scr[...] = ci

    def sl_stage2(k0, cnt, head):
        c, ci = cv_scr[...], ci_scr[...]
        for k in range(k0, k0 + cnt):
            m, e, c = _extract_top(c, cand_rows, ci)
            fv_scr[k:k + 1, :] = m
            fi_scr[k:k + 1, :] = e
        cv_scr[...] = c

    def sl_out(head):
        top_s = fv_scr[...]
        e = jnp.exp(top_s - jnp.max(top_s, axis=0, keepdims=True))
        off = pl.multiple_of(head * PEER_TOPK, PEER_TOPK)
        idx_v[pl.ds(off, PEER_TOPK), :] = fi_scr[...]
        gt_scr[slot_r, pl.ds(off, PEER_TOPK), :] = e / jnp.sum(e, axis=0, keepdims=True)

    per_slice = PEER_TOPK // PEER_NSLOT
    slices = [sl_scores]
    slices += [functools.partial(sl_stage1, k0, per_slice) for k0 in range(0, PEER_TOPK, per_slice)]
    slices += [sl_cand]
    k0 = 0
    for cnt in (4, 3, 3, 3, 3):
        slices.append(functools.partial(sl_stage2, k0, cnt))
        k0 += cnt
    slices += [sl_out]
    assert len(slices) == 2 * PEER_NSLOT and k0 == PEER_TOPK

    def group(t0, head, half, last):
        for j in range(PEER_NSLOT):
            t = t0 + j
            wait(j)
            out = ffn(t, j)
            if not last or j == 0:
                issue(t + ahead, (j + ahead) % PEER_NSLOT)
            slices[half * PEER_NSLOT + j](head)
            o_ref[pl.ds(t, 1), :] = out

    def publish_routing():
        cp = pltpu.make_async_copy(idx_v, idx_s, idx_sem.at[0])
        cp.start()
        cp.wait()
        for t0 in range(ahead):
            issue(t0, t0)

    @pl.when(s == 0)
    def _():
        assert tab_ref.shape[0] // nch >= PEER_SEL * tb
        idx_v[...] = (lax.broadcasted_iota(jnp.int32, idx_v.shape, 0) * tb
                      + lax.broadcasted_iota(jnp.int32, idx_v.shape, 1))
        h_scr[slot_e] = jnp.zeros(h_scr.shape[1:], h_scr.dtype)
        gt_scr[slot_e] = jnp.zeros(gt_scr.shape[1:], gt_scr.dtype)
        publish_routing()

    h = _norm_mod(xr_ref[...], gain_ref[...], sc_ref[...], sh_ref[...])
    h_scr[slot_r] = h
    q = jnp.dot(h.astype(jnp.bfloat16), wq_ref[...], preferred_element_type=jnp.float32)
    for head in range(PEER_HEADS):
        q_scr[head] = q[:, head * PEER_DKEY:(head + 1) * PEER_DKEY]

    n_pairs = tb // (2 * PEER_NSLOT)
    assert n_pairs == PEER_HEADS

    def pair(gi, last):
        t0 = pl.multiple_of(gi * (2 * PEER_NSLOT), 2 * PEER_NSLOT)
        group(t0, gi, 0, False)
        group(t0 + PEER_NSLOT, gi, 1, last)

    def body(gi, carry):
        pair(gi, False)
        return carry

    lax.fori_loop(0, n_pairs - 1, body, 0)
    pair(n_pairs - 1, True)

    @pl.when(s < nblk)
    def _():
        publish_routing()


def pack_expert_table(u, v):
    ub = lax.bitcast_convert_type(u.astype(jnp.bfloat16), jnp.uint16).astype(jnp.uint32)
    vb = lax.bitcast_convert_type(v.astype(jnp.bfloat16), jnp.uint16).astype(jnp.uint32)
    return (ub << 16) | vb


def peer_layer(x, gain, sc, sh, gate, wq_bf, sk_bf, tab):
    n, d = x.shape
    tb = PEER_TB
    nblk = n // tb
    nch = d // LANES

    def cur(s):
        return jnp.minimum(s, nblk - 1)

    def prev(s):
        return jnp.maximum(s - 1, 0)

    def rows(arr, blk):
        g = arr.shape[0]
        if g == n:
            return arr, pl.BlockSpec((tb, d), lambda s: (blk(s), 0))
        per = n // g
        return arr[:, None, :], pl.BlockSpec((None, 1, d), lambda s: (blk(s) * tb // per, 0, 0))

    sc, sc_spec = rows(sc, cur)
    sh, sh_spec = rows(sh, cur)
    gate, gate_spec = rows(gate, prev)
    return pl.pallas_call(
        functools.partial(_peer_kernel, nblk),
        grid=(nblk + 1,),
        in_specs=[pl.BlockSpec((tb, d), lambda s: (cur(s), 0)),
                  pl.BlockSpec((1, d), lambda s: (0, 0)),
                  sc_spec, sh_spec,
                  pl.BlockSpec(wq_bf.shape, lambda s: (0, 0)),
                  pl.BlockSpec(sk_bf.shape, lambda s: (0, 0, 0)),
                  pl.BlockSpec((tb, d), lambda s: (prev(s), 0)),
                  gate_spec,
                  pl.BlockSpec(memory_space=pl.ANY)],
        out_specs=pl.BlockSpec((tb, d), lambda s: (prev(s), 0)),
        out_shape=jax.ShapeDtypeStruct((n, d), jnp.float32),
        scratch_shapes=[pltpu.VMEM((PEER_HEADS, tb, PEER_DKEY), jnp.float32),
                        pltpu.VMEM((2, tb, d), jnp.float32),
                        pltpu.VMEM((2, PEER_SEL, tb), jnp.float32),
                        pltpu.VMEM((PEER_SEL, tb), jnp.int32),
                        pltpu.SMEM((PEER_SEL, tb), jnp.int32),
                        pltpu.VMEM((2, PEER_NKEYS, tb), jnp.float32),
                        pltpu.VMEM((2, PEER_TOPK, tb), jnp.float32),
                        pltpu.VMEM((2, PEER_TOPK, tb), jnp.int32),
                        pltpu.VMEM((N_CAND, tb), jnp.float32),
                        pltpu.VMEM((N_CAND, tb), jnp.int32),
                        pltpu.VMEM((PEER_TOPK, tb), jnp.float32),
                        pltpu.VMEM((PEER_TOPK, tb), jnp.int32),
                        pltpu.VMEM((PEER_NSLOT, nch, PEER_SEL, LANES), jnp.uint32),
                        pltpu.SemaphoreType.DMA((PEER_NSLOT,)),
                        pltpu.SemaphoreType.DMA((1,))],
        compiler_params=pltpu.CompilerParams(dimension_semantics=("arbitrary",)),
        name="peer",
    )(x, gain, sc, sh, wq_bf, sk_bf, x, gate, tab.reshape(-1, LANES))


def _final_norm_kernel(x_ref, g_ref, o_ref):
    x = x_ref[...]
    o_ref[...] = x * lax.rsqrt(jnp.mean(x * x, axis=-1, keepdims=True) + EPS) * g_ref[...]


def final_norm(x, g):
    n, d = x.shape
    tm = min(ROW_TILE, n)
    return pl.pallas_call(
        _final_norm_kernel,
        grid=(n // tm,),
        in_specs=[pl.BlockSpec((tm, d), lambda i: (i, 0)), pl.BlockSpec((1, d), lambda i: (0, 0))],
        out_specs=pl.BlockSpec((tm, d), lambda i: (i, 0)),
        out_shape=jax.ShapeDtypeStruct((n, d), x.dtype),
        name="final_norm",
    )(x, g.reshape(1, d))


def _per_tile_rows(m, seq_len, tile):
    return m if seq_len % tile == 0 else jnp.repeat(m, seq_len, axis=0)


def trunk_layer(x, seq_len, mod, p, state):
    n, d = x.shape
    bsz = n // seq_len
    sh1, sc1, gt1, sh2, sc2, gt2 = (_per_tile_rows(m, seq_len, ROW_TILE) for m in jnp.split(mod, 6, axis=-1))
    z = mix_in(x, p['g_mix'], sc1, sh1, p['w_in'])
    k_new = z[:, Z_SK:Z_SK + SWA_KVW].reshape(bsz, seq_len, SWA_KVW)
    v_new = z[:, Z_SV:Z_SV + SWA_KVW].reshape(bsz, seq_len, SWA_KVW)
    if state is None:
        win_rows = min(WINDOW, PAST_LEN)
        c0 = jnp.zeros((bsz, MLSTM_HEADS, MLSTM_DH, MLSTM_DH), jnp.float32)
        n0 = jnp.zeros((bsz, MLSTM_HEADS, MLSTM_DH), jnp.float32)
        m0 = jnp.zeros((bsz, MLSTM_HEADS), jnp.float32)
        cbuf = jnp.zeros((bsz, CONV_WIDTH - 1, CONV_W), jnp.float32)
        oa = swa_banded(z, p['sinks'], seq_len)
        k_all, v_all = k_new, v_new
    else:
        kbuf, vbuf, cbuf, c0, n0, m0 = state
        win_rows = kbuf.shape[1]
        k_all = jnp.concatenate([kbuf.reshape(bsz, win_rows, SWA_KVW), k_new], axis=1)
        v_all = jnp.concatenate([vbuf.reshape(bsz, win_rows, SWA_KVW), v_new], axis=1)
        q = z[:, Z_SQ:Z_SQ + SWA_W].reshape(bsz, seq_len, SWA_W)
        oa = swa_cached(q, k_all, v_all, p['sinks']).reshape(n, SWA_W)
    hm, c1, n1, m1 = mlstm_layer(z, p['gate_b'], p['mh_g'], c0, n0, m0, seq_len)
    x, u = mix_out(x, gt1, hm, oa, z, cbuf, p['conv_w'], p['w_out'], seq_len)
    k_keep = k_all[:, -win_rows:].reshape(bsz, win_rows, SWA_KV_HEADS, SWA_DH)
    v_keep = v_all[:, -win_rows:].reshape(bsz, win_rows, SWA_KV_HEADS, SWA_DH)
    cbuf_new = jnp.concatenate([cbuf, u.reshape(bsz, seq_len, CONV_W)], axis=1)[:, -(CONV_WIDTH - 1):]
    x = peer_layer(x, p['g_ffn'], sc2, sh2, gt2, p['wq'], p['subkeys'], p['tab'])
    return x, (k_keep, v_keep, cbuf_new, c1, n1, m1)


def kernel(x_prompt, x_sample, cache_swa_k, cache_swa_v, state_conv, state_mlstm_C, state_mlstm_n, state_mlstm_m, c_prompt, c_sample, ada_w, ada_b, norm_mix_g, norm_ffn_g, w_in, w_out, mlstm_gate_b, mlstm_norm_g, swa_sinks, conv_w, peer_wq, peer_subkeys, peer_u, peer_v, final_g):
    bp, tp, d = x_prompt.shape
    bs, ts, _ = x_sample.shape
    xp, xs = x_prompt.reshape(bp * tp, d), x_sample.reshape(bs * ts, d)
    c_all = jnp.concatenate([c_prompt, c_sample], axis=0)
    bf = jnp.bfloat16
    n_gate = 2 * MLSTM_HEADS
    new_p, new_s = [], []
    for l in range(DEPTH):
        wl = w_in[l]
        w_perm = jnp.concatenate([wl[:, :Z_SQ], wl[:, Z_SQ + n_gate:], wl[:, Z_SQ:Z_SQ + n_gate],
                                  jnp.zeros((d, LANES - n_gate), wl.dtype)], axis=1)
        p = dict(g_mix=norm_mix_g[l].reshape(1, d), g_ffn=norm_ffn_g[l].reshape(1, d),
                 w_in=w_perm.astype(bf), w_out=w_out[l].astype(bf), gate_b=mlstm_gate_b[l],
                 mh_g=mlstm_norm_g[l], sinks=swa_sinks[l], conv_w=conv_w[l], wq=peer_wq[l].astype(bf),
                 subkeys=peer_subkeys[l].astype(bf), tab=pack_expert_table(peer_u[l], peer_v[l]))
        mod = adaln_mod(c_all, ada_w[l], ada_b[l])
        xp, sp = trunk_layer(xp, tp, mod[:bp], p, None)
        st = (cache_swa_k[l], cache_swa_v[l], state_conv[l], state_mlstm_C[l], state_mlstm_n[l], state_mlstm_m[l])
        xs, ss = trunk_layer(xs, ts, mod[bp:], p, st)
        new_p.append(sp)
        new_s.append(ss)
    y_prompt = final_norm(xp, final_g).reshape(bp, tp, d)
    y_sample = final_norm(xs, final_g).reshape(bs, ts, d)
    pk, pv, pc, pC, pn, pm = [jnp.stack(t) for t in zip(*new_p)]
    sk, sv, sc, sC, sn, sm = [jnp.stack(t) for t in zip(*new_s)]
    return (y_prompt, y_sample, pk, pv, pc, pC, pn, pm, sk, sv, sc, sC, sn, sm)
```

```python
import functools
import math

import jax
import jax.numpy as jnp
from jax import lax
from jax.experimental import pallas as pl
from jax.experimental.pallas import tpu as pltpu

D_MODEL = 1024
DEPTH = 2
PAST_LEN = 16384

MLSTM_W = D_MODEL // 2
MLSTM_HEADS = 4
MLSTM_DH = MLSTM_W // MLSTM_HEADS
MLSTM_CHUNK = 64
SWA_W = D_MODEL // 4
SWA_DH = 64
SWA_HEADS = SWA_W // SWA_DH
SWA_KV_HEADS = SWA_HEADS // 2
SWA_GROUP = SWA_HEADS // SWA_KV_HEADS
SWA_KVW = SWA_KV_HEADS * SWA_DH
WINDOW = 128
CONV_W = D_MODEL - MLSTM_W - SWA_W
CONV_WIDTH = 3
PEER_HEADS = 8
PEER_NKEYS = 128
PEER_DKEY = 256
HALF_KEY = PEER_DKEY // 2
PEER_TOPK = 16
PEER_SEL = PEER_HEADS * PEER_TOPK
EPS = 1e-6

LANES = 128
SUBLANES = 8
ROW_TILE = 512
PEER_TB = LANES
PEER_NSLOT = 8
MLSTM_SEQS = 2

Z_Q, Z_K, Z_V, Z_O = 0, MLSTM_W, 2 * MLSTM_W, 3 * MLSTM_W
Z_SQ = 4 * MLSTM_W
Z_SK = Z_SQ + SWA_W
Z_SV = Z_SK + SWA_KVW
Z_CB = Z_SV + SWA_KVW
Z_CC = Z_CB + CONV_W
Z_CH = Z_CC + CONV_W
Z_GATE = Z_CH + CONV_W
Z_W = Z_GATE + LANES
ALIBI = tuple(2.0 ** (-8.0 * h / SWA_HEADS) for h in range(1, SWA_HEADS + 1))


def _bdot(a, b, dims):
    return lax.dot_general(a.astype(jnp.bfloat16), b.astype(jnp.bfloat16), (dims, ((), ())),
                           preferred_element_type=jnp.float32)


def _row_spec(arr, n, tm, d):
    g = arr.shape[0]
    if g == n:
        return arr, pl.BlockSpec((tm, d), lambda i, *_: (i, 0))
    per = n // g
    return arr[:, None, :], pl.BlockSpec((None, 1, d), lambda i, *_: (i * tm // per, 0, 0))


def _norm_mod(x, gain, sc, sh):
    y = x * lax.rsqrt(jnp.mean(x * x, axis=-1, keepdims=True) + EPS) * gain
    return y * (1.0 + sc) + sh


def _adaln_kernel(c_ref, w_ref, b_ref, o_ref):
    c = c_ref[...]
    o_ref[...] = _bdot(c * jax.nn.sigmoid(c), w_ref[...], ((1,), (0,))) + b_ref[...]


def adaln_mod(c, w, b):
    bsz, d = c.shape
    m = w.shape[1]
    tn = 512
    return pl.pallas_call(
        _adaln_kernel,
        grid=(m // tn,),
        in_specs=[pl.BlockSpec((bsz, d), lambda j: (0, 0)),
                  pl.BlockSpec((d, tn), lambda j: (0, j)),
                  pl.BlockSpec((1, tn), lambda j: (0, j))],
        out_specs=pl.BlockSpec((bsz, tn), lambda j: (0, j)),
        out_shape=jax.ShapeDtypeStruct((bsz, m), jnp.float32),
        name="adaln_mod",
    )(c, w, b.reshape(1, m))


def _mix_in_kernel(x_ref, gain_ref, sc_ref, sh_ref, w_ref, z_ref):
    h = _norm_mod(x_ref[...], gain_ref[...], sc_ref[...], sh_ref[...])
    z_ref[...] = jnp.dot(h.astype(jnp.bfloat16), w_ref[...], preferred_element_type=jnp.float32)


def mix_in(x, gain, sc, sh, w_bf):
    n, d = x.shape
    zw = w_bf.shape[1]
    tm, tn = min(ROW_TILE, n), zw
    sc, sc_spec = _row_spec(sc, n, tm, d)
    sh, sh_spec = _row_spec(sh, n, tm, d)
    return pl.pallas_call(
        _mix_in_kernel,
        grid=(n // tm, zw // tn),
        in_specs=[pl.BlockSpec((tm, d), lambda i, j: (i, 0)),
                  pl.BlockSpec((1, d), lambda i, j: (0, 0)),
                  sc_spec, sh_spec,
                  pl.BlockSpec((d, tn), lambda i, j: (0, j))],
        out_specs=pl.BlockSpec((tm, tn), lambda i, j: (i, j)),
        out_shape=jax.ShapeDtypeStruct((n, zw), jnp.float32),
        name="mix_in",
    )(x, gain, sc, sh, w_bf)


def _log_sigmoid(x):
    return jnp.minimum(x, 0.0) - jnp.log(1.0 + jnp.exp(-jnp.abs(x)))


def _mlstm_kernel(chunk, valid, q_ref, k_ref, v_ref, o_ref, gate_ref, gb_ref, g_ref, c0_ref, n0_ref, m0_ref,
                  hm_ref, c_ref, n_ref, m_ref):
    nb, tc = q_ref.shape[:2]
    nh, dh, L = MLSTM_HEADS, MLSTM_DH, chunk

    @pl.when(pl.program_id(1) == 0)
    def _():
        c_ref[...] = c0_ref[...]
        n_ref[...] = n0_ref[...]
        m_ref[...] = m0_ref[...]

    row = lax.broadcasted_iota(jnp.int32, (L, L), 0)
    col = lax.broadcasted_iota(jnp.int32, (L, L), 1)
    eye = row == col
    visible = (col <= row) & (col < valid)
    rcol = lax.broadcasted_iota(jnp.int32, (L, 1), 0)
    gb = gb_ref[...]
    gain = g_ref[...]

    def to_row(x_col):
        return jnp.sum(jnp.where(eye, x_col, 0.0), axis=0, keepdims=True)

    def one_chunk(ci, carry):
        r0 = pl.multiple_of(ci * L, L)
        for sq, hd in [(a, b) for a in range(nb) for b in range(nh)]:
            pre = gate_ref[sq, pl.ds(r0, L), :] + gb
            sl = slice(hd * dh, (hd + 1) * dh)
            q = q_ref[sq, pl.ds(r0, L), sl]
            k = k_ref[sq, pl.ds(r0, L), sl] * (dh ** -0.5)
            v = v_ref[sq, pl.ds(r0, L), sl]
            ig = pre[:, hd:hd + 1]
            lf = jnp.where(rcol < valid, _log_sigmoid(pre[:, nh + hd:nh + hd + 1]), 0.0)
            b = lf
            s = 1
            while s < L:
                b = b + jnp.where(rcol >= s, pltpu.roll(b, s, 0), 0.0)
                s *= 2
            cmat, nrow, m_prev = c_ref[sq, hd], n_ref[sq, hd], m_ref[sq, hd][:, 0:1]
            a = b + m_prev
            d = jnp.where(visible, b - to_row(b) + to_row(ig), -jnp.inf)
            m_t = jnp.maximum(a, jnp.max(d, axis=-1, keepdims=True))
            w_inter = jnp.exp(a - m_t)
            w_intra = jnp.exp(d - m_t)
            qk = _bdot(q, k, ((1,), (1,))) * w_intra
            num = w_inter * _bdot(q, cmat, ((1,), (1,))) + _bdot(qk, v, ((1,), (0,)))
            den = w_inter * jnp.sum(q * nrow, axis=-1, keepdims=True) + jnp.sum(qk, axis=-1, keepdims=True)
            h = num / jnp.maximum(jnp.abs(den), jnp.exp(-m_t))
            hn = h * lax.rsqrt(jnp.mean(h * h, axis=-1, keepdims=True) + EPS) * gain[:, sl]
            hm_ref[sq, pl.ds(r0, L), sl] = hn * jax.nn.sigmoid(o_ref[sq, pl.ds(r0, L), sl])
            m_last = m_t[L - 1:L]
            wl_inter = w_inter[L - 1:L]
            wl = jnp.where(rcol < valid, jnp.exp(b[L - 1:L] - b + ig - m_last), 0.0)
            c_ref[sq, hd] = wl_inter * cmat + _bdot(wl * v, k, ((0,), (0,)))
            n_ref[sq, hd] = wl_inter * nrow + jnp.sum(wl * k, axis=0, keepdims=True)
            m_ref[sq, hd] = jnp.broadcast_to(m_last, (1, dh))
        return carry

    lax.fori_loop(0, tc // L, one_chunk, 0)


def mlstm_layer(z, gate_b, mh_g, c0, n0, m0, seq_len):
    bsz = z.shape[0] // seq_len
    nh, dh = MLSTM_HEADS, MLSTM_DH
    chunk = math.gcd(seq_len, MLSTM_CHUNK)
    valid, padded = chunk, seq_len
    if chunk % SUBLANES:
        assert seq_len < SUBLANES
        chunk = padded = SUBLANES
        z = jnp.pad(z.reshape(bsz, seq_len, -1), ((0, 0), (0, padded - seq_len), (0, 0)))
    z = z.reshape(bsz, padded, -1)
    nb = MLSTM_SEQS
    assert bsz % nb == 0
    tc = min(padded, ROW_TILE)
    w = nh * dh
    gb = jnp.zeros((1, LANES), jnp.float32).at[0, :2 * nh].set(gate_b)

    def zcol(off, width):
        return pl.BlockSpec((nb, tc, width), lambda b, c: (b, c, off // width))

    def state(shape):
        return pl.BlockSpec((nb,) + shape, lambda b, c: (b,) + (0,) * len(shape))

    hm, c1, n1, m1 = pl.pallas_call(
        functools.partial(_mlstm_kernel, chunk, valid),
        grid=(bsz // nb, padded // tc),
        in_specs=[zcol(Z_Q, w), zcol(Z_K, w), zcol(Z_V, w), zcol(Z_O, w), zcol(Z_GATE, LANES),
                  pl.BlockSpec((1, LANES), lambda b, c: (0, 0)),
                  pl.BlockSpec((1, w), lambda b, c: (0, 0)),
                  state((nh, dh, dh)), state((nh, 1, dh)), state((nh, 1, dh))],
        out_specs=[pl.BlockSpec((nb, tc, w), lambda b, c: (b, c, 0)),
                   state((nh, dh, dh)), state((nh, 1, dh)), state((nh, 1, dh))],
        out_shape=[jax.ShapeDtypeStruct((bsz, padded, w), jnp.float32),
                   jax.ShapeDtypeStruct((bsz, nh, dh, dh), jnp.float32),
                   jax.ShapeDtypeStruct((bsz, nh, 1, dh), jnp.float32),
                   jax.ShapeDtypeStruct((bsz, nh, 1, dh), jnp.float32)],
        compiler_params=pltpu.CompilerParams(dimension_semantics=("arbitrary", "arbitrary")),
        name="mlstm",
    )(z, z, z, z, z, gb, mh_g.reshape(1, w), c0, n0[:, :, None, :],
      jnp.broadcast_to(m0[:, :, None, None], (bsz, nh, 1, dh)))
    return hm[:, :seq_len].reshape(bsz * seq_len, w), c1, n1[:, :, 0, :], m1[:, :, 0, 0]


def _attend(q, kb, vb, dist, visible, slope, sink):
    s = _bdot(q, kb, ((1,), (1,))) * (SWA_DH ** -0.5) - slope * dist.astype(jnp.float32)
    s = jnp.where(visible, s, -jnp.inf)
    mx = jnp.maximum(jnp.max(s, axis=-1, keepdims=True), sink)
    e = jnp.exp(s - mx)
    p = e / (jnp.sum(e, axis=-1, keepdims=True) + jnp.exp(sink - mx))
    return _bdot(p, vb, ((1,), (0,)))


def _swa_banded_kernel(q_ref, kp_ref, kc_ref, vp_ref, vc_ref, sink_ref, o_ref):
    j = pl.program_id(1)
    w = q_ref.shape[0]
    t = lax.broadcasted_iota(jnp.int32, (w, 2 * w), 0)
    i = lax.broadcasted_iota(jnp.int32, (w, 2 * w), 1)
    dist = t + w - i
    visible = (dist >= 0) & (dist <= WINDOW) & ((j > 0) | (i >= w))
    sinks = sink_ref[...]
    for kv in range(SWA_KV_HEADS):
        ks = slice(kv * SWA_DH, (kv + 1) * SWA_DH)
        kb = jnp.concatenate([kp_ref[:, ks], kc_ref[:, ks]], axis=0)
        vb = jnp.concatenate([vp_ref[:, ks], vc_ref[:, ks]], axis=0)
        for g in range(SWA_GROUP):
            h = kv * SWA_GROUP + g
            hs = slice(h * SWA_DH, (h + 1) * SWA_DH)
            o_ref[:, hs] = _attend(q_ref[:, hs], kb, vb, dist, visible, ALIBI[h], sinks[:, h:h + 1])


def swa_banded(z, sinks, seq_len):
    n = z.shape[0]
    w = WINDOW
    nb = seq_len // w
    sink_row = jnp.zeros((1, LANES), jnp.float32).at[0, :SWA_HEADS].set(sinks)

    def cur(off, width):
        return pl.BlockSpec((w, width), lambda b, j: (b * nb + j, off // width))

    def prev(off, width):
        return pl.BlockSpec((w, width), lambda b, j: (b * nb + jnp.maximum(j - 1, 0), off // width))

    return pl.pallas_call(
        _swa_banded_kernel,
        grid=(n // seq_len, nb),
        in_specs=[cur(Z_SQ, SWA_W), prev(Z_SK, SWA_KVW), cur(Z_SK, SWA_KVW), prev(Z_SV, SWA_KVW),
                  cur(Z_SV, SWA_KVW), pl.BlockSpec((1, LANES), lambda b, j: (0, 0))],
        out_specs=pl.BlockSpec((w, SWA_W), lambda b, j: (b * nb + j, 0)),
        out_shape=jax.ShapeDtypeStruct((n, SWA_W), jnp.float32),
        name="swa_banded",
    )(z, z, z, z, z, sink_row)


def _swa_cached_kernel(t_new, n_keys, q_ref, k_ref, v_ref, sink_ref, o_ref):
    m = q_ref.shape[1]
    nkp = k_ref.shape[0]
    r = lax.broadcasted_iota(jnp.int32, (m, nkp), 0)
    i = lax.broadcasted_iota(jnp.int32, (m, nkp), 1)
    first_key_pos = PAST_LEN - (n_keys - t_new)
    dist = (n_keys - t_new) + r % t_new - i
    visible = (dist >= 0) & (dist <= WINDOW) & (i < n_keys) & (first_key_pos + i >= 0)
    rg = lax.broadcasted_iota(jnp.int32, (m, 1), 0) // t_new
    sinks = sink_ref[...]
    for kv in range(SWA_KV_HEADS):
        ks = slice(kv * SWA_DH, (kv + 1) * SWA_DH)
        h0 = kv * SWA_GROUP
        slope = jnp.where(rg == 0, ALIBI[h0], ALIBI[h0 + 1])
        sink = jnp.where(rg == 0, sinks[:, h0:h0 + 1], sinks[:, h0 + 1:h0 + 2])
        o_ref[kv] = _attend(q_ref[kv], k_ref[:, ks], v_ref[:, ks], dist, visible, slope, sink)


def swa_cached(q, k_all, v_all, sinks):
    assert SWA_GROUP == 2
    bsz, t_new, _ = q.shape
    n_keys = k_all.shape[1]
    nkp = -(-n_keys // SUBLANES) * SUBLANES
    pad = ((0, 0), (0, nkp - n_keys), (0, 0))
    k_all, v_all = jnp.pad(k_all, pad), jnp.pad(v_all, pad)
    m = SWA_GROUP * t_new
    qs = q.reshape(bsz, t_new, SWA_KV_HEADS, SWA_GROUP, SWA_DH).transpose(0, 2, 3, 1, 4)
    qs = qs.reshape(bsz, SWA_KV_HEADS, m, SWA_DH)
    sink_row = jnp.zeros((1, LANES), jnp.float32).at[0, :SWA_HEADS].set(sinks)
    o = pl.pallas_call(
        functools.partial(_swa_cached_kernel, t_new, n_keys),
        grid=(bsz,),
        in_specs=[pl.BlockSpec((None, SWA_KV_HEADS, m, SWA_DH), lambda b: (b, 0, 0, 0)),
                  pl.BlockSpec((None, nkp, SWA_KVW), lambda b: (b, 0, 0)),
                  pl.BlockSpec((None, nkp, SWA_KVW), lambda b: (b, 0, 0)),
                  pl.BlockSpec((1, LANES), lambda b: (0, 0))],
        out_specs=pl.BlockSpec((None, SWA_KV_HEADS, m, SWA_DH), lambda b: (b, 0, 0, 0)),
        out_shape=jax.ShapeDtypeStruct((bsz, SWA_KV_HEADS, m, SWA_DH), jnp.float32),
        name="swa_cached",
    )(qs, k_all, v_all, sink_row)
    o = o.reshape(bsz, SWA_KV_HEADS, SWA_GROUP, t_new, SWA_DH).transpose(0, 3, 1, 2, 4)
    return o.reshape(bsz, t_new, SWA_W)


def _mix_out_kernel(seq_len, x_ref, gate_ref, hm_ref, oa_ref, cb_ref, cc_ref, ch_ref, hcc_ref, hch_ref,
                    pa_ref, pb_ref, cw_ref, w_ref, o_ref, u_ref):
    i = pl.program_id(0)
    tm = x_ref.shape[0]
    u = cc_ref[...] * ch_ref[...]
    u_ref[...] = u
    hu = hcc_ref[...] * hch_ref[...]
    r = lax.broadcasted_iota(jnp.int32, u.shape, 0)
    p = (i * tm + r) % seq_len
    pa, pb = pa_ref[...], pb_ref[...]
    last, last2 = hu[SUBLANES - 1:SUBLANES], hu[SUBLANES - 2:SUBLANES - 1]
    u1 = jnp.where(r >= 1, pltpu.roll(u, 1, 0), last)
    u2 = jnp.where(r >= 2, pltpu.roll(u, 2, 0), jnp.where(r == 1, last, last2))
    s1 = jnp.where(p >= 1, u1, pa)
    s2 = jnp.where(p >= 2, u2, jnp.where(p == 1, pa, pb))
    cw = cw_ref[...]
    yc = cb_ref[...] * (cw[0:1] * s2 + cw[1:2] * s1 + cw[2:3] * u)
    cat = jnp.concatenate([hm_ref[...], oa_ref[...], yc], axis=-1)
    mix = jnp.dot(cat.astype(jnp.bfloat16), w_ref[...], preferred_element_type=jnp.float32)
    o_ref[...] = x_ref[...] + gate_ref[...] * mix


def mix_out(x, gate, hm, oa, z, conv_prev, conv_w, w_out_bf, seq_len):
    n, d = x.shape
    c = CONV_W
    tm = min(ROW_TILE, n)
    gate, gate_spec = _row_spec(gate, n, tm, d)
    if seq_len % tm:
        pa = jnp.repeat(conv_prev[:, 1], seq_len, axis=0)
        pb = jnp.repeat(conv_prev[:, 0], seq_len, axis=0)
    else:
        pa, pb = conv_prev[:, 1], conv_prev[:, 0]
    pa, pa_spec = _row_spec(pa, n, tm, c)
    pb, pb_spec = _row_spec(pb, n, tm, c)
    cw = jnp.zeros((SUBLANES, c), jnp.float32).at[:CONV_WIDTH].set(conv_w)

    def zcol(off, width):
        return pl.BlockSpec((tm, width), lambda i: (i, off // width))

    def zhalo(off):
        return pl.BlockSpec((SUBLANES, c), lambda i: (jnp.maximum(i * (tm // SUBLANES) - 1, 0), off // c))

    return pl.pallas_call(
        functools.partial(_mix_out_kernel, seq_len),
        grid=(n // tm,),
        in_specs=[pl.BlockSpec((tm, d), lambda i: (i, 0)), gate_spec,
                  pl.BlockSpec((tm, MLSTM_W), lambda i: (i, 0)),
                  pl.BlockSpec((tm, SWA_W), lambda i: (i, 0)),
                  zcol(Z_CB, c), zcol(Z_CC, c), zcol(Z_CH, c), zhalo(Z_CC), zhalo(Z_CH),
                  pa_spec, pb_spec,
                  pl.BlockSpec((SUBLANES, c), lambda i: (0, 0)),
                  pl.BlockSpec((d, d), lambda i: (0, 0))],
        out_specs=[pl.BlockSpec((tm, d), lambda i: (i, 0)), pl.BlockSpec((tm, c), lambda i: (i, 0))],
        out_shape=[jax.ShapeDtypeStruct((n, d), jnp.float32), jax.ShapeDtypeStruct((n, c), jnp.float32)],
        name="mix_out",
    )(x, gate, hm, oa, z, z, z, z, z, pa, pb, cw, w_out_bf)


def _pair_candidates(s1, i1, s2, i2):
    k, sub = PEER_TOPK, SUBLANES
    assert k == 2 * sub
    row = lax.broadcasted_iota(jnp.int32, (sub, s1.shape[1]), 0)
    sums, ids = [], []

    def emit(a, b0, nvalid):
        c = s1[a:a + 1] + s2[b0:b0 + sub]
        sums.append(c if nvalid >= sub else jnp.where(row < nvalid, c, -jnp.inf))
        ids.append(i1[a:a + 1] * PEER_NKEYS + i2[b0:b0 + sub])

    for a in range(sub):
        nb = k // (a + 1)
        for b0 in range(0, nb, sub):
            emit(a, b0, nb - b0)
    sums.append(s1[sub:] + s2[0:1])
    ids.append(i1[sub:] * PEER_NKEYS + i2[0:1])
    return jnp.concatenate(sums, axis=0), jnp.concatenate(ids, axis=0)


N_CAND = 10 * SUBLANES


def _extract_top(s, rows, payload=None):
    m = jnp.max(s, axis=0, keepdims=True)
    r = jnp.min(jnp.where(s == m, rows, jnp.int32(s.shape[0])), axis=0, keepdims=True)
    hit = rows == r
    ident = r if payload is None else jnp.max(jnp.where(hit, payload, -1), axis=0, keepdims=True)
    return m, ident, jnp.where(hit, -jnp.inf, s)


def _gelu_tanh(x):
    return 0.5 * x * (1.0 + jnp.tanh(math.sqrt(2.0 / math.pi) * (x + 0.044715 * (x * x * x))))


def _peer_kernel(nblk, xr_ref, gain_ref, sc_ref, sh_ref, wq_ref, sk_ref, xres_ref, gate_ref, tab_ref,
                 o_ref, q_scr, h_scr, gt_scr, idx_v, idx_s, s_scr, tv_scr, ti_scr, cv_scr, ci_scr, fv_scr, fi_scr,
                 buf, sem, idx_sem):
    s = pl.program_id(0)
    tb, d = xr_ref.shape
    nch = d // LANES
    assert tb == LANES
    slot_r = s % 2
    slot_e = 1 - slot_r
    per_token_gate = gate_ref.shape[0] == tb

    def row_copy(e, slot, r):
        src = tab_ref.at[pl.ds(pl.multiple_of(e * nch, nch), nch)]
        return pltpu.make_async_copy(src, buf.at[slot, :, r, :], sem.at[slot])

    def issue(t, slot):
        for r in range(PEER_SEL):
            row_copy(idx_s[r, t], slot, r).start(priority=r % 2)

    def wait(slot):
        pltpu.make_async_copy(tab_ref.at[pl.ds(0, PEER_SEL * nch)], buf.at[slot], sem.at[slot]).wait()

    lane = lax.broadcasted_iota(jnp.int32, (PEER_SEL, tb), 1)

    def ffn(t, slot):
        hrow = h_scr[slot_e, pl.ds(t, 1), :]
        part = None
        for c in range(nch):
            u = lax.bitcast_convert_type(buf[slot, c] & jnp.uint32(0xFFFF0000), jnp.float32)
            term = u * hrow[:, c * LANES:(c + 1) * LANES]
            part = term if part is None else part + term
        sdot = jnp.sum(part, axis=-1, keepdims=True)
        g = jnp.sum(jnp.where(lane == t, gt_scr[slot_e], 0.0), axis=-1, keepdims=True)
        w = g * _gelu_tanh(sdot)
        y = jnp.concatenate(
            [jnp.sum(lax.bitcast_convert_type(buf[slot, c] << 16, jnp.float32) * w, axis=0, keepdims=True)
             for c in range(nch)], axis=1)
        gate = gate_ref[pl.ds(t, 1), :] if per_token_gate else gate_ref[...]
        return xres_ref[pl.ds(t, 1), :] + gate * y

    ahead = PEER_NSLOT - 1

    key_rows = lax.broadcasted_iota(jnp.int32, (PEER_NKEYS, tb), 0)
    cand_rows = lax.broadcasted_iota(jnp.int32, (N_CAND, tb), 0)

    def sl_scores(head):
        qh = q_scr[head].astype(jnp.bfloat16)
        for p in range(2):
            s_scr[p] = lax.dot_general(sk_ref[p], qh[:, p * HALF_KEY:(p + 1) * HALF_KEY],
                                       (((1,), (1,)), ((), ())), preferred_element_type=jnp.float32)

    def sl_stage1(k0, cnt, head):
        for p in range(2):
            sc_ = s_scr[p]
            for k in range(k0, k0 + cnt):
                m, r, sc_ = _extract_top(sc_, key_rows)
                tv_scr[p, k:k + 1, :] = m
                ti_scr[p, k:k + 1, :] = r
            s_scr[p] = sc_

    def sl_cand(head):
        c, ci = _pair_candidates(tv_scr[0], ti_scr[0], tv_scr[1], ti_scr[1])
        cv_scr[...] = c
        ci_scr[...] = ci

    def sl_stage2(k0, cnt, head):
        c, ci = cv_scr[...], ci_scr[...]
        for k in range(k0, k0 + cnt):
            m, e, c = _extract_top(c, cand_rows, ci)
            fv_scr[k:k + 1, :] = m
            fi_scr[k:k + 1, :] = e
        cv_scr[...] = c

    def sl_out(head):
        top_s = fv_scr[...]
        e = jnp.exp(top_s - jnp.max(top_s, axis=0, keepdims=True))
        off = pl.multiple_of(head * PEER_TOPK, PEER_TOPK)
        idx_v[pl.ds(off, PEER_TOPK), :] = fi_scr[...]
        gt_scr[slot_r, pl.ds(off, PEER_TOPK), :] = e / jnp.sum(e, axis=0, keepdims=True)

    per_slice = PEER_TOPK // PEER_NSLOT
    slices = [sl_scores]
    slices += [functools.partial(sl_stage1, k0, per_slice) for k0 in range(0, PEER_TOPK, per_slice)]
    slices += [sl_cand]
    k0 = 0
    for cnt in (4, 3, 3, 3, 3):
        slices.append(functools.partial(sl_stage2, k0, cnt))
        k0 += cnt
    slices += [sl_out]
    assert len(slices) == 2 * PEER_NSLOT and k0 == PEER_TOPK

    def group(t0, head, half, last):
        for j in range(PEER_NSLOT):
            t = t0 + j
            wait(j)
            out = ffn(t, j)
            if not last or j == 0:
                issue(t + ahead, (j + ahead) % PEER_NSLOT)
            slices[half * PEER_NSLOT + j](head)
            o_ref[pl.ds(t, 1), :] = out

    def publish_routing():
        cp = pltpu.make_async_copy(idx_v, idx_s, idx_sem.at[0])
        cp.start()
        cp.wait()
        for t0 in range(ahead):
            issue(t0, t0)

    @pl.when(s == 0)
    def _():
        assert tab_ref.shape[0] // nch >= PEER_SEL * tb
        idx_v[...] = (lax.broadcasted_iota(jnp.int32, idx_v.shape, 0) * tb
                      + lax.broadcasted_iota(jnp.int32, idx_v.shape, 1))
        h_scr[slot_e] = jnp.zeros(h_scr.shape[1:], h_scr.dtype)
        gt_scr[slot_e] = jnp.zeros(gt_scr.shape[1:], gt_scr.dtype)
        publish_routing()

    h = _norm_mod(xr_ref[...], gain_ref[...], sc_ref[...], sh_ref[...])
    h_scr[slot_r] = h
    q = jnp.dot(h.astype(jnp.bfloat16), wq_ref[...], preferred_element_type=jnp.float32)
    for head in range(PEER_HEADS):
        q_scr[head] = q[:, head * PEER_DKEY:(head + 1) * PEER_DKEY]

    n_pairs = tb // (2 * PEER_NSLOT)
    assert n_pairs == PEER_HEADS

    def pair(gi, last):
        t0 = pl.multiple_of(gi * (2 * PEER_NSLOT), 2 * PEER_NSLOT)
        group(t0, gi, 0, False)
        group(t0 + PEER_NSLOT, gi, 1, last)

    def body(gi, carry):
        pair(gi, False)
        return carry

    lax.fori_loop(0, n_pairs - 1, body, 0)
    pair(n_pairs - 1, True)

    @pl.when(s < nblk)
    def _():
        publish_routing()


def pack_expert_table(u, v):
    ub = lax.bitcast_convert_type(u.astype(jnp.bfloat16), jnp.uint16).astype(jnp.uint32)
    vb = lax.bitcast_convert_type(v.astype(jnp.bfloat16), jnp.uint16).astype(jnp.uint32)
    return (ub << 16) | vb


def peer_layer(x, gain, sc, sh, gate, wq_bf, sk_bf, tab):
    n, d = x.shape
    tb = PEER_TB
    nblk = n // tb
    nch = d // LANES

    def cur(s):
        return jnp.minimum(s, nblk - 1)

    def prev(s):
        return jnp.maximum(s - 1, 0)

    def rows(arr, blk):
        g = arr.shape[0]
        if g == n:
            return arr, pl.BlockSpec((tb, d), lambda s: (blk(s), 0))
        per = n // g
        return arr[:, None, :], pl.BlockSpec((None, 1, d), lambda s: (blk(s) * tb // per, 0, 0))

    sc, sc_spec = rows(sc, cur)
    sh, sh_spec = rows(sh, cur)
    gate, gate_spec = rows(gate, prev)
    return pl.pallas_call(
        functools.partial(_peer_kernel, nblk),
        grid=(nblk + 1,),
        in_specs=[pl.BlockSpec((tb, d), lambda s: (cur(s), 0)),
                  pl.BlockSpec((1, d), lambda s: (0, 0)),
                  sc_spec, sh_spec,
                  pl.BlockSpec(wq_bf.shape, lambda s: (0, 0)),
                  pl.BlockSpec(sk_bf.shape, lambda s: (0, 0, 0)),
                  pl.BlockSpec((tb, d), lambda s: (prev(s), 0)),
                  gate_spec,
                  pl.BlockSpec(memory_space=pl.ANY)],
        out_specs=pl.BlockSpec((tb, d), lambda s: (prev(s), 0)),
        out_shape=jax.ShapeDtypeStruct((n, d), jnp.float32),
        scratch_shapes=[pltpu.VMEM((PEER_HEADS, tb, PEER_DKEY), jnp.float32),
                        pltpu.VMEM((2, tb, d), jnp.float32),
                        pltpu.VMEM((2, PEER_SEL, tb), jnp.float32),
                        pltpu.VMEM((PEER_SEL, tb), jnp.int32),
                        pltpu.SMEM((PEER_SEL, tb), jnp.int32),
                        pltpu.VMEM((2, PEER_NKEYS, tb), jnp.float32),
                        pltpu.VMEM((2, PEER_TOPK, tb), jnp.float32),
                        pltpu.VMEM((2, PEER_TOPK, tb), jnp.int32),
                        pltpu.VMEM((N_CAND, tb), jnp.float32),
                        pltpu.VMEM((N_CAND, tb), jnp.int32),
                        pltpu.VMEM((PEER_TOPK, tb), jnp.float32),
                        pltpu.VMEM((PEER_TOPK, tb), jnp.int32),
                        pltpu.VMEM((PEER_NSLOT, nch, PEER_SEL, LANES), jnp.uint32),
                        pltpu.SemaphoreType.DMA((PEER_NSLOT,)),
                        pltpu.SemaphoreType.DMA((1,))],
        compiler_params=pltpu.CompilerParams(dimension_semantics=("arbitrary",)),
        name="peer",
    )(x, gain, sc, sh, wq_bf, sk_bf, x, gate, tab.reshape(-1, LANES))


def _final_norm_kernel(x_ref, g_ref, o_ref):
    x = x_ref[...]
    o_ref[...] = x * lax.rsqrt(jnp.mean(x * x, axis=-1, keepdims=True) + EPS) * g_ref[...]


def final_norm(x, g):
    n, d = x.shape
    tm = min(ROW_TILE, n)
    return pl.pallas_call(
        _final_norm_kernel,
        grid=(n // tm,),
        in_specs=[pl.BlockSpec((tm, d), lambda i: (i, 0)), pl.BlockSpec((1, d), lambda i: (0, 0))],
        out_specs=pl.BlockSpec((tm, d), lambda i: (i, 0)),
        out_shape=jax.ShapeDtypeStruct((n, d), x.dtype),
        name="final_norm",
    )(x, g.reshape(1, d))


def _per_tile_rows(m, seq_len, tile):
    return m if seq_len % tile == 0 else jnp.repeat(m, seq_len, axis=0)


def trunk_layer(x, seq_len, mod, p, state):
    n, d = x.shape
    bsz = n // seq_len
    sh1, sc1, gt1, sh2, sc2, gt2 = (_per_tile_rows(m, seq_len, ROW_TILE) for m in jnp.split(mod, 6, axis=-1))
    z = mix_in(x, p['g_mix'], sc1, sh1, p['w_in'])
    k_new = z[:, Z_SK:Z_SK + SWA_KVW].reshape(bsz, seq_len, SWA_KVW)
    v_new = z[:, Z_SV:Z_SV + SWA_KVW].reshape(bsz, seq_len, SWA_KVW)
    if state is None:
        win_rows = min(WINDOW, PAST_LEN)
        c0 = jnp.zeros((bsz, MLSTM_HEADS, MLSTM_DH, MLSTM_DH), jnp.float32)
        n0 = jnp.zeros((bsz, MLSTM_HEADS, MLSTM_DH), jnp.float32)
        m0 = jnp.zeros((bsz, MLSTM_HEADS), jnp.float32)
        cbuf = jnp.zeros((bsz, CONV_WIDTH - 1, CONV_W), jnp.float32)
        oa = swa_banded(z, p['sinks'], seq_len)
        k_all, v_all = k_new, v_new
    else:
        kbuf, vbuf, cbuf, c0, n0, m0 = state
        win_rows = kbuf.shape[1]
        k_all = jnp.concatenate([kbuf.reshape(bsz, win_rows, SWA_KVW), k_new], axis=1)
        v_all = jnp.concatenate([vbuf.reshape(bsz, win_rows, SWA_KVW), v_new], axis=1)
        q = z[:, Z_SQ:Z_SQ + SWA_W].reshape(bsz, seq_len, SWA_W)
        oa = swa_cached(q, k_all, v_all, p['sinks']).reshape(n, SWA_W)
    hm, c1, n1, m1 = mlstm_layer(z, p['gate_b'], p['mh_g'], c0, n0, m0, seq_len)
    x, u = mix_out(x, gt1, hm, oa, z, cbuf, p['conv_w'], p['w_out'], seq_len)
    k_keep = k_all[:, -win_rows:].reshape(bsz, win_rows, SWA_KV_HEADS, SWA_DH)
    v_keep = v_all[:, -win_rows:].reshape(bsz, win_rows, SWA_KV_HEADS, SWA_DH)
    cbuf_new = jnp.concatenate([cbuf, u.reshape(bsz, seq_len, CONV_W)], axis=1)[:, -(CONV_WIDTH - 1):]
    x = peer_layer(x, p['g_ffn'], sc2, sh2, gt2, p['wq'], p['subkeys'], p['tab'])
    return x, (k_keep, v_keep, cbuf_new, c1, n1, m1)


def kernel(x_prompt, x_sample, cache_swa_k, cache_swa_v, state_conv, state_mlstm_C, state_mlstm_n, state_mlstm_m, c_prompt, c_sample, ada_w, ada_b, norm_mix_g, norm_ffn_g, w_in, w_out, mlstm_gate_b, mlstm_norm_g, swa_sinks, conv_w, peer_wq, peer_subkeys, peer_u, peer_v, final_g):
    bp, tp, d = x_prompt.shape
    bs, ts, _ = x_sample.shape
    xp, xs = x_prompt.reshape(bp * tp, d), x_sample.reshape(bs * ts, d)
    c_all = jnp.concatenate([c_prompt, c_sample], axis=0)
    bf = jnp.bfloat16
    n_gate = 2 * MLSTM_HEADS
    new_p, new_s = [], []
    for l in range(DEPTH):
        wl = w_in[l]
        w_perm = jnp.concatenate([wl[:, :Z_SQ], wl[:, Z_SQ + n_gate:], wl[:, Z_SQ:Z_SQ + n_gate],
                                  jnp.zeros((d, LANES - n_gate), wl.dtype)], axis=1)
        p = dict(g_mix=norm_mix_g[l].reshape(1, d), g_ffn=norm_ffn_g[l].reshape(1, d),
                 w_in=w_perm.astype(bf), w_out=w_out[l].astype(bf), gate_b=mlstm_gate_b[l],
                 mh_g=mlstm_norm_g[l], sinks=swa_sinks[l], conv_w=conv_w[l], wq=peer_wq[l].astype(bf),
                 subkeys=peer_subkeys[l].astype(bf), tab=pack_expert_table(peer_u[l], peer_v[l]))
        mod = adaln_mod(c_all, ada_w[l], ada_b[l])
        xp, sp = trunk_layer(xp, tp, mod[:bp], p, None)
        st = (cache_swa_k[l], cache_swa_v[l], state_conv[l], state_mlstm_C[l], state_mlstm_n[l], state_mlstm_m[l])
        xs, ss = trunk_layer(xs, ts, mod[bp:], p, st)
        new_p.append(sp)
        new_s.append(ss)
    y_prompt = final_norm(xp, final_g).reshape(bp, tp, d)
    y_sample = final_norm(xs, final_g).reshape(bs, ts, d)
    pk, pv, pc, pC, pn, pm = [jnp.stack(t) for t in zip(*new_p)]
    sk, sv, sc, sC, sn, sm = [jnp.stack(t) for t in zip(*new_s)]
    return (y_prompt, y_sample, pk, pv, pc, pC, pn, pm, sk, sv, sc, sC, sn, sm)
```

```python
import functools
import math

import jax
import jax.numpy as jnp
from jax import lax
from jax.experimental import pallas as pl
from jax.experimental.pallas import tpu as pltpu

D_MODEL = 1024
DEPTH = 2
PAST_LEN = 16384

MLSTM_W = D_MODEL // 2
MLSTM_HEADS = 4
MLSTM_DH = MLSTM_W // MLSTM_HEADS
MLSTM_CHUNK = 64
SWA_W = D_MODEL // 4
SWA_DH = 64
SWA_HEADS = SWA_W // SWA_DH
SWA_KV_HEADS = SWA_HEADS // 2
SWA_GROUP = SWA_HEADS // SWA_KV_HEADS
SWA_KVW = SWA_KV_HEADS * SWA_DH
WINDOW = 128
CONV_W = D_MODEL - MLSTM_W - SWA_W
CONV_WIDTH = 3
PEER_HEADS = 8
PEER_NKEYS = 128
PEER_DKEY = 256
HALF_KEY = PEER_DKEY // 2
PEER_TOPK = 16
PEER_SEL = PEER_HEADS * PEER_TOPK
EPS = 1e-6

LANES = 128
SUBLANES = 8
ROW_TILE = 512
PEER_TB = LANES
PEER_NSLOT = 8
MLSTM_SEQS = 2

Z_Q, Z_K, Z_V, Z_O = 0, MLSTM_W, 2 * MLSTM_W, 3 * MLSTM_W
Z_SQ = 4 * MLSTM_W
Z_SK = Z_SQ + SWA_W
Z_SV = Z_SK + SWA_KVW
Z_CB = Z_SV + SWA_KVW
Z_CC = Z_CB + CONV_W
Z_CH = Z_CC + CONV_W
Z_GATE = Z_CH + CONV_W
Z_W = Z_GATE + LANES
ALIBI = tuple(2.0 ** (-8.0 * h / SWA_HEADS) for h in range(1, SWA_HEADS + 1))


def _bdot(a, b, dims):
    return lax.dot_general(a.astype(jnp.bfloat16), b.astype(jnp.bfloat16), (dims, ((), ())),
                           preferred_element_type=jnp.float32)


def _row_spec(arr, n, tm, d):
    g = arr.shape[0]
    if g == n:
        return arr, pl.BlockSpec((tm, d), lambda i, *_: (i, 0))
    per = n // g
    return arr[:, None, :], pl.BlockSpec((None, 1, d), lambda i, *_: (i * tm // per, 0, 0))


def _norm_mod(x, gain, sc, sh):
    y = x * lax.rsqrt(jnp.mean(x * x, axis=-1, keepdims=True) + EPS) * gain
    return y * (1.0 + sc) + sh


def _adaln_kernel(c_ref, w_ref, b_ref, o_ref):
    c = c_ref[...]
    o_ref[...] = _bdot(c * jax.nn.sigmoid(c), w_ref[...], ((1,), (0,))) + b_ref[...]


def adaln_mod(c, w, b):
    bsz, d = c.shape
    m = w.shape[1]
    tn = 512
    return pl.pallas_call(
        _adaln_kernel,
        grid=(m // tn,),
        in_specs=[pl.BlockSpec((bsz, d), lambda j: (0, 0)),
                  pl.BlockSpec((d, tn), lambda j: (0, j)),
                  pl.BlockSpec((1, tn), lambda j: (0, j))],
        out_specs=pl.BlockSpec((bsz, tn), lambda j: (0, j)),
        out_shape=jax.ShapeDtypeStruct((bsz, m), jnp.float32),
        name="adaln_mod",
    )(c, w, b.reshape(1, m))


def _mix_in_kernel(x_ref, gain_ref, sc_ref, sh_ref, w_ref, z_ref):
    h = _norm_mod(x_ref[...], gain_ref[...], sc_ref[...], sh_ref[...])
    z_ref[...] = jnp.dot(h.astype(jnp.bfloat16), w_ref[...], preferred_element_type=jnp.float32)


def mix_in(x, gain, sc, sh, w_bf):
    n, d = x.shape
    zw = w_bf.shape[1]
    tm, tn = min(ROW_TILE, n), zw
    sc, sc_spec = _row_spec(sc, n, tm, d)
    sh, sh_spec = _row_spec(sh, n, tm, d)
    return pl.pallas_call(
        _mix_in_kernel,
        grid=(n // tm, zw // tn),
        in_specs=[pl.BlockSpec((tm, d), lambda i, j: (i, 0)),
                  pl.BlockSpec((1, d), lambda i, j: (0, 0)),
                  sc_spec, sh_spec,
                  pl.BlockSpec((d, tn), lambda i, j: (0, j))],
        out_specs=pl.BlockSpec((tm, tn), lambda i, j: (i, j)),
        out_shape=jax.ShapeDtypeStruct((n, zw), jnp.float32),
        name="mix_in",
    )(x, gain, sc, sh, w_bf)


def _log_sigmoid(x):
    return jnp.minimum(x, 0.0) - jnp.log(1.0 + jnp.exp(-jnp.abs(x)))


def _mlstm_kernel(chunk, valid, q_ref, k_ref, v_ref, o_ref, gate_ref, gb_ref, g_ref, c0_ref, n0_ref, m0_ref,
                  hm_ref, c_ref, n_ref, m_ref):
    nb, tc = q_ref.shape[:2]
    nh, dh, L = MLSTM_HEADS, MLSTM_DH, chunk

    @pl.when(pl.program_id(1) == 0)
    def _():
        c_ref[...] = c0_ref[...]
        n_ref[...] = n0_ref[...]
        m_ref[...] = m0_ref[...]

    row = lax.broadcasted_iota(jnp.int32, (L, L), 0)
    col = lax.broadcasted_iota(jnp.int32, (L, L), 1)
    eye = row == col
    visible = (col <= row) & (col < valid)
    rcol = lax.broadcasted_iota(jnp.int32, (L, 1), 0)
    gb = gb_ref[...]
    gain = g_ref[...]

    def to_row(x_col):
        return jnp.sum(jnp.where(eye, x_col, 0.0), axis=0, keepdims=True)

    def one_chunk(ci, carry):
        r0 = pl.multiple_of(ci * L, L)
        for sq, hd in [(a, b) for a in range(nb) for b in range(nh)]:
            pre = gate_ref[sq, pl.ds(r0, L), :] + gb
            sl = slice(hd * dh, (hd + 1) * dh)
            q = q_ref[sq, pl.ds(r0, L), sl]
            k = k_ref[sq, pl.ds(r0, L), sl] * (dh ** -0.5)
            v = v_ref[sq, pl.ds(r0, L), sl]
            ig = pre[:, hd:hd + 1]
            lf = jnp.where(rcol < valid, _log_sigmoid(pre[:, nh + hd:nh + hd + 1]), 0.0)
            b = lf
            s = 1
            while s < L:
                b = b + jnp.where(rcol >= s, pltpu.roll(b, s, 0), 0.0)
                s *= 2
            cmat, nrow, m_prev = c_ref[sq, hd], n_ref[sq, hd], m_ref[sq, hd][:, 0:1]
            a = b + m_prev
            d = jnp.where(visible, b - to_row(b) + to_row(ig), -jnp.inf)
            m_t = jnp.maximum(a, jnp.max(d, axis=-1, keepdims=True))
            w_inter = jnp.exp(a - m_t)
            w_intra = jnp.exp(d - m_t)
            qk = _bdot(q, k, ((1,), (1,))) * w_intra
            num = w_inter * _bdot(q, cmat, ((1,), (1,))) + _bdot(qk, v, ((1,), (0,)))
            den = w_inter * jnp.sum(q * nrow, axis=-1, keepdims=True) + jnp.sum(qk, axis=-1, keepdims=True)
            h = num / jnp.maximum(jnp.abs(den), jnp.exp(-m_t))
            hn = h * lax.rsqrt(jnp.mean(h * h, axis=-1, keepdims=True) + EPS) * gain[:, sl]
            hm_ref[sq, pl.ds(r0, L), sl] = hn * jax.nn.sigmoid(o_ref[sq, pl.ds(r0, L), sl])
            m_last = m_t[L - 1:L]
            wl_inter = w_inter[L - 1:L]
            wl = jnp.where(rcol < valid, jnp.exp(b[L - 1:L] - b + ig - m_last), 0.0)
            c_ref[sq, hd] = wl_inter * cmat + _bdot(wl * v, k, ((0,), (0,)))
            n_ref[sq, hd] = wl_inter * nrow + jnp.sum(wl * k, axis=0, keepdims=True)
            m_ref[sq, hd] = jnp.broadcast_to(m_last, (1, dh))
        return carry

    lax.fori_loop(0, tc // L, one_chunk, 0)


def mlstm_layer(z, gate_b, mh_g, c0, n0, m0, seq_len):
    bsz = z.shape[0] // seq_len
    nh, dh = MLSTM_HEADS, MLSTM_DH
    chunk = math.gcd(seq_len, MLSTM_CHUNK)
    valid, padded = chunk, seq_len
    if chunk % SUBLANES:
        assert seq_len < SUBLANES
        chunk = padded = SUBLANES
        z = jnp.pad(z.reshape(bsz, seq_len, -1), ((0, 0), (0, padded - seq_len), (0, 0)))
    z = z.reshape(bsz, padded, -1)
    nb = MLSTM_SEQS
    assert bsz % nb == 0
    tc = min(padded, ROW_TILE)
    w = nh * dh
    gb = jnp.zeros((1, LANES), jnp.float32).at[0, :2 * nh].set(gate_b)

    def zcol(off, width):
        return pl.BlockSpec((nb, tc, width), lambda b, c: (b, c, off // width))

    def state(shape):
        return pl.BlockSpec((nb,) + shape, lambda b, c: (b,) + (0,) * len(shape))

    hm, c1, n1, m1 = pl.pallas_call(
        functools.partial(_mlstm_kernel, chunk, valid),
        grid=(bsz // nb, padded // tc),
        in_specs=[zcol(Z_Q, w), zcol(Z_K, w), zcol(Z_V, w), zcol(Z_O, w), zcol(Z_GATE, LANES),
                  pl.BlockSpec((1, LANES), lambda b, c: (0, 0)),
                  pl.BlockSpec((1, w), lambda b, c: (0, 0)),
                  state((nh, dh, dh)), state((nh, 1, dh)), state((nh, 1, dh))],
        out_specs=[pl.BlockSpec((nb, tc, w), lambda b, c: (b, c, 0)),
                   state((nh, dh, dh)), state((nh, 1, dh)), state((nh, 1, dh))],
        out_shape=[jax.ShapeDtypeStruct((bsz, padded, w), jnp.float32),
                   jax.ShapeDtypeStruct((bsz, nh, dh, dh), jnp.float32),
                   jax.ShapeDtypeStruct((bsz, nh, 1, dh), jnp.float32),
                   jax.ShapeDtypeStruct((bsz, nh, 1, dh), jnp.float32)],
        compiler_params=pltpu.CompilerParams(dimension_semantics=("arbitrary", "arbitrary")),
        name="mlstm",
    )(z, z, z, z, z, gb, mh_g.reshape(1, w), c0, n0[:, :, None, :],
      jnp.broadcast_to(m0[:, :, None, None], (bsz, nh, 1, dh)))
    return hm[:, :seq_len].reshape(bsz * seq_len, w), c1, n1[:, :, 0, :], m1[:, :, 0, 0]


def _attend(q, kb, vb, dist, visible, slope, sink):
    s = _bdot(q, kb, ((1,), (1,))) * (SWA_DH ** -0.5) - slope * dist.astype(jnp.float32)
    s = jnp.where(visible, s, -jnp.inf)
    mx = jnp.maximum(jnp.max(s, axis=-1, keepdims=True), sink)
    e = jnp.exp(s - mx)
    p = e / (jnp.sum(e, axis=-1, keepdims=True) + jnp.exp(sink - mx))
    return _bdot(p, vb, ((1,), (0,)))


def _swa_banded_kernel(q_ref, kp_ref, kc_ref, vp_ref, vc_ref, sink_ref, o_ref):
    j = pl.program_id(1)
    w = q_ref.shape[0]
    t = lax.broadcasted_iota(jnp.int32, (w, 2 * w), 0)
    i = lax.broadcasted_iota(jnp.int32, (w, 2 * w), 1)
    dist = t + w - i
    visible = (dist >= 0) & (dist <= WINDOW) & ((j > 0) | (i >= w))
    sinks = sink_ref[...]
    for kv in range(SWA_KV_HEADS):
        ks = slice(kv * SWA_DH, (kv + 1) * SWA_DH)
        kb = jnp.concatenate([kp_ref[:, ks], kc_ref[:, ks]], axis=0)
        vb = jnp.concatenate([vp_ref[:, ks], vc_ref[:, ks]], axis=0)
        for g in range(SWA_GROUP):
            h = kv * SWA_GROUP + g
            hs = slice(h * SWA_DH, (h + 1) * SWA_DH)
            o_ref[:, hs] = _attend(q_ref[:, hs], kb, vb, dist, visible, ALIBI[h], sinks[:, h:h + 1])


def swa_banded(z, sinks, seq_len):
    n = z.shape[0]
    w = WINDOW
    nb = seq_len // w
    sink_row = jnp.zeros((1, LANES), jnp.float32).at[0, :SWA_HEADS].set(sinks)

    def cur(off, width):
        return pl.BlockSpec((w, width), lambda b, j: (b * nb + j, off // width))

    def prev(off, width):
        return pl.BlockSpec((w, width), lambda b, j: (b * nb + jnp.maximum(j - 1, 0), off // width))

    return pl.pallas_call(
        _swa_banded_kernel,
        grid=(n // seq_len, nb),
        in_specs=[cur(Z_SQ, SWA_W), prev(Z_SK, SWA_KVW), cur(Z_SK, SWA_KVW), prev(Z_SV, SWA_KVW),
                  cur(Z_SV, SWA_KVW), pl.BlockSpec((1, LANES), lambda b, j: (0, 0))],
        out_specs=pl.BlockSpec((w, SWA_W), lambda b, j: (b * nb + j, 0)),
        out_shape=jax.ShapeDtypeStruct((n, SWA_W), jnp.float32),
        name="swa_banded",
    )(z, z, z, z, z, sink_row)


def _swa_cached_kernel(t_new, n_keys, q_ref, k_ref, v_ref, sink_ref, o_ref):
    m = q_ref.shape[1]
    nkp = k_ref.shape[0]
    r = lax.broadcasted_iota(jnp.int32, (m, nkp), 0)
    i = lax.broadcasted_iota(jnp.int32, (m, nkp), 1)
    first_key_pos = PAST_LEN - (n_keys - t_new)
    dist = (n_keys - t_new) + r % t_new - i
    visible = (dist >= 0) & (dist <= WINDOW) & (i < n_keys) & (first_key_pos + i >= 0)
    rg = lax.broadcasted_iota(jnp.int32, (m, 1), 0) // t_new
    sinks = sink_ref[...]
    for kv in range(SWA_KV_HEADS):
        ks = slice(kv * SWA_DH, (kv + 1) * SWA_DH)
        h0 = kv * SWA_GROUP
        slope = jnp.where(rg == 0, ALIBI[h0], ALIBI[h0 + 1])
        sink = jnp.where(rg == 0, sinks[:, h0:h0 + 1], sinks[:, h0 + 1:h0 + 2])
        o_ref[kv] = _attend(q_ref[kv], k_ref[:, ks], v_ref[:, ks], dist, visible, slope, sink)


def swa_cached(q, k_all, v_all, sinks):
    assert SWA_GROUP == 2
    bsz, t_new, _ = q.shape
    n_keys = k_all.shape[1]
    nkp = -(-n_keys // SUBLANES) * SUBLANES
    pad = ((0, 0), (0, nkp - n_keys), (0, 0))
    k_all, v_all = jnp.pad(k_all, pad), jnp.pad(v_all, pad)
    m = SWA_GROUP * t_new
    qs = q.reshape(bsz, t_new, SWA_KV_HEADS, SWA_GROUP, SWA_DH).transpose(0, 2, 3, 1, 4)
    qs = qs.reshape(bsz, SWA_KV_HEADS, m, SWA_DH)
    sink_row = jnp.zeros((1, LANES), jnp.float32).at[0, :SWA_HEADS].set(sinks)
    o = pl.pallas_call(
        functools.partial(_swa_cached_kernel, t_new, n_keys),
        grid=(bsz,),
        in_specs=[pl.BlockSpec((None, SWA_KV_HEADS, m, SWA_DH), lambda b: (b, 0, 0, 0)),
                  pl.BlockSpec((None, nkp, SWA_KVW), lambda b: (b, 0, 0)),
                  pl.BlockSpec((None, nkp, SWA_KVW), lambda b: (b, 0, 0)),
                  pl.BlockSpec((1, LANES), lambda b: (0, 0))],
        out_specs=pl.BlockSpec((None, SWA_KV_HEADS, m, SWA_DH), lambda b: (b, 0, 0, 0)),
        out_shape=jax.ShapeDtypeStruct((bsz, SWA_KV_HEADS, m, SWA_DH), jnp.float32),
        name="swa_cached",
    )(qs, k_all, v_all, sink_row)
    o = o.reshape(bsz, SWA_KV_HEADS, SWA_GROUP, t_new, SWA_DH).transpose(0, 3, 1, 2, 4)
    return o.reshape(bsz, t_new, SWA_W)


def _mix_out_kernel(seq_len, x_ref, gate_ref, hm_ref, oa_ref, cb_ref, cc_ref, ch_ref, hcc_ref, hch_ref,
                    pa_ref, pb_ref, cw_ref, w_ref, o_ref, u_ref):
    i = pl.program_id(0)
    tm = x_ref.shape[0]
    u = cc_ref[...] * ch_ref[...]
    u_ref[...] = u
    hu = hcc_ref[...] * hch_ref[...]
    r = lax.broadcasted_iota(jnp.int32, u.shape, 0)
    p = (i * tm + r) % seq_len
    pa, pb = pa_ref[...], pb_ref[...]
    last, last2 = hu[SUBLANES - 1:SUBLANES], hu[SUBLANES - 2:SUBLANES - 1]
    u1 = jnp.where(r >= 1, pltpu.roll(u, 1, 0), last)
    u2 = jnp.where(r >= 2, pltpu.roll(u, 2, 0), jnp.where(r == 1, last, last2))
    s1 = jnp.where(p >= 1, u1, pa)
    s2 = jnp.where(p >= 2, u2, jnp.where(p == 1, pa, pb))
    cw = cw_ref[...]
    yc = cb_ref[...] * (cw[0:1] * s2 + cw[1:2] * s1 + cw[2:3] * u)
    cat = jnp.concatenate([hm_ref[...], oa_ref[...], yc], axis=-1)
    mix = jnp.dot(cat.astype(jnp.bfloat16), w_ref[...], preferred_element_type=jnp.float32)
    o_ref[...] = x_ref[...] + gate_ref[...] * mix


def mix_out(x, gate, hm, oa, z, conv_prev, conv_w, w_out_bf, seq_len):
    n, d = x.shape
    c = CONV_W
    tm = min(ROW_TILE, n)
    gate, gate_spec = _row_spec(gate, n, tm, d)
    if seq_len % tm:
        pa = jnp.repeat(conv_prev[:, 1], seq_len, axis=0)
        pb = jnp.repeat(conv_prev[:, 0], seq_len, axis=0)
    else:
        pa, pb = conv_prev[:, 1], conv_prev[:, 0]
    pa, pa_spec = _row_spec(pa, n, tm, c)
    pb, pb_spec = _row_spec(pb, n, tm, c)
    cw = jnp.zeros((SUBLANES, c), jnp.float32).at[:CONV_WIDTH].set(conv_w)

    def zcol(off, width):
        return pl.BlockSpec((tm, width), lambda i: (i, off // width))

    def zhalo(off):
        return pl.BlockSpec((SUBLANES, c), lambda i: (jnp.maximum(i * (tm // SUBLANES) - 1, 0), off // c))

    return pl.pallas_call(
        functools.partial(_mix_out_kernel, seq_len),
        grid=(n // tm,),
        in_specs=[pl.BlockSpec((tm, d), lambda i: (i, 0)), gate_spec,
                  pl.BlockSpec((tm, MLSTM_W), lambda i: (i, 0)),
                  pl.BlockSpec((tm, SWA_W), lambda i: (i, 0)),
                  zcol(Z_CB, c), zcol(Z_CC, c), zcol(Z_CH, c), zhalo(Z_CC), zhalo(Z_CH),
                  pa_spec, pb_spec,
                  pl.BlockSpec((SUBLANES, c), lambda i: (0, 0)),
                  pl.BlockSpec((d, d), lambda i: (0, 0))],
        out_specs=[pl.BlockSpec((tm, d), lambda i: (i, 0)), pl.BlockSpec((tm, c), lambda i: (i, 0))],
        out_shape=[jax.ShapeDtypeStruct((n, d), jnp.float32), jax.ShapeDtypeStruct((n, c), jnp.float32)],
        name="mix_out",
    )(x, gate, hm, oa, z, z, z, z, z, pa, pb, cw, w_out_bf)


def _pair_candidates(s1, i1, s2, i2):
    k, sub = PEER_TOPK, SUBLANES
    assert k == 2 * sub
    row = lax.broadcasted_iota(jnp.int32, (sub, s1.shape[1]), 0)
    sums, ids = [], []

    def emit(a, b0, nvalid):
        c = s1[a:a + 1] + s2[b0:b0 + sub]
        sums.append(c if nvalid >= sub else jnp.where(row < nvalid, c, -jnp.inf))
        ids.append(i1[a:a + 1] * PEER_NKEYS + i2[b0:b0 + sub])

    for a in range(sub):
        nb = k // (a + 1)
        for b0 in range(0, nb, sub):
            emit(a, b0, nb - b0)
    sums.append(s1[sub:] + s2[0:1])
    ids.append(i1[sub:] * PEER_NKEYS + i2[0:1])
    return jnp.concatenate(sums, axis=0), jnp.concatenate(ids, axis=0)


N_CAND = 10 * SUBLANES


def _extract_top(s, rows, payload=None):
    m = jnp.max(s, axis=0, keepdims=True)
    r = jnp.min(jnp.where(s == m, rows, jnp.int32(s.shape[0])), axis=0, keepdims=True)
    hit = rows == r
    ident = r if payload is None else jnp.max(jnp.where(hit, payload, -1), axis=0, keepdims=True)
    return m, ident, jnp.where(hit, -jnp.inf, s)


def _gelu_tanh(x):
    return 0.5 * x * (1.0 + jnp.tanh(math.sqrt(2.0 / math.pi) * (x + 0.044715 * (x * x * x))))


def _peer_kernel(nblk, xr_ref, gain_ref, sc_ref, sh_ref, wq_ref, sk_ref, xres_ref, gate_ref, tab_ref,
                 o_ref, q_scr, h_scr, gt_scr, idx_v, idx_s, s_scr, tv_scr, ti_scr, cv_scr, ci_scr, fv_scr, fi_scr,
                 buf, sem, idx_sem):
    s = pl.program_id(0)
    tb, d = xr_ref.shape
    nch = d // LANES
    assert tb == LANES
    slot_r = s % 2
    slot_e = 1 - slot_r
    per_token_gate = gate_ref.shape[0] == tb

    def row_copy(e, slot, r):
        src = tab_ref.at[pl.ds(pl.multiple_of(e * nch, nch), nch)]
        return pltpu.make_async_copy(src, buf.at[slot, :, r, :], sem.at[slot])

    def issue(t, slot):
        for r in range(PEER_SEL):
            row_copy(idx_s[r, t], slot, r).start(priority=r % 2)

    def wait(slot):
        pltpu.make_async_copy(tab_ref.at[pl.ds(0, PEER_SEL * nch)], buf.at[slot], sem.at[slot]).wait()

    lane = lax.broadcasted_iota(jnp.int32, (PEER_SEL, tb), 1)

    def ffn(t, slot):
        hrow = h_scr[slot_e, pl.ds(t, 1), :]
        part = None
        for c in range(nch):
            u = lax.bitcast_convert_type(buf[slot, c] & jnp.uint32(0xFFFF0000), jnp.float32)
            term = u * hrow[:, c * LANES:(c + 1) * LANES]
            part = term if part is None else part + term
        sdot = jnp.sum(part, axis=-1, keepdims=True)
        g = jnp.sum(jnp.where(lane == t, gt_scr[slot_e], 0.0), axis=-1, keepdims=True)
        w = g * _gelu_tanh(sdot)
        y = jnp.concatenate(
            [jnp.sum(lax.bitcast_convert_type(buf[slot, c] << 16, jnp.float32) * w, axis=0, keepdims=True)
             for c in range(nch)], axis=1)
        gate = gate_ref[pl.ds(t, 1), :] if per_token_gate else gate_ref[...]
        return xres_ref[pl.ds(t, 1), :] + gate * y

    ahead = PEER_NSLOT - 1

    key_rows = lax.broadcasted_iota(jnp.int32, (PEER_NKEYS, tb), 0)
    cand_rows = lax.broadcasted_iota(jnp.int32, (N_CAND, tb), 0)

    def sl_scores(head):
        qh = q_scr[head].astype(jnp.bfloat16)
        for p in range(2):
            s_scr[p] = lax.dot_general(sk_ref[p], qh[:, p * HALF_KEY:(p + 1) * HALF_KEY],
                                       (((1,), (1,)), ((), ())), preferred_element_type=jnp.float32)

    def sl_stage1(k0, cnt, head):
        for p in range(2):
            sc_ = s_scr[p]
            for k in range(k0, k0 + cnt):
                m, r, sc_ = _extract_top(sc_, key_rows)
                tv_scr[p, k:k + 1, :] = m
                ti_scr[p, k:k + 1, :] = r
            s_scr[p] = sc_

    def sl_cand(head):
        c, ci = _pair_candidates(tv_scr[0], ti_scr[0], tv_scr[1], ti_scr[1])
        cv_scr[...] = c
        ci_scr[...] = ci

    def sl_stage2(k0, cnt, head):
        c, ci = cv_scr[...], ci_scr[...]
        for k in range(k0, k0 + cnt):
            m, e, c = _extract_top(c, cand_rows, ci)
            fv_scr[k:k + 1, :] = m
            fi_scr[k:k + 1, :] = e
        cv_scr[...] = c

    def sl_out(head):
        top_s = fv_scr[...]
        e = jnp.exp(top_s - jnp.max(top_s, axis=0, keepdims=True))
        off = pl.multiple_of(head * PEER_TOPK, PEER_TOPK)
        idx_v[pl.ds(off, PEER_TOPK), :] = fi_scr[...]
        gt_scr[slot_r, pl.ds(off, PEER_TOPK), :] = e / jnp.sum(e, axis=0, keepdims=True)

    per_slice = PEER_TOPK // PEER_NSLOT
    slices = [sl_scores]
    slices += [functools.partial(sl_stage1, k0, per_slice) for k0 in range(0, PEER_TOPK, per_slice)]
    slices += [sl_cand]
    k0 = 0
    for cnt in (4, 3, 3, 3, 3):
        slices.append(functools.partial(sl_stage2, k0, cnt))
        k0 += cnt
    slices += [sl_out]
    assert len(slices) == 2 * PEER_NSLOT and k0 == PEER_TOPK

    def publish_routing():
        cp = pltpu.make_async_copy(idx_v, idx_s, idx_sem.at[0])
        cp.start()
        cp.wait()

    def group(t0, head, todo, last):
        for j in range(PEER_NSLOT):
            t = t0 + j
            wait(j)
            out = ffn(t, j)
            if not last or j == 0:
                issue(t + ahead, (j + ahead) % PEER_NSLOT)
            for i in todo[j]:
                slices[i](head)
            if last:
                @pl.when(s < nblk)
                def _():
                    if j == 0:
                        publish_routing()
                    else:
                        issue(j - 1, j - 1)
            o_ref[pl.ds(t, 1), :] = out

    @pl.when(s == 0)
    def _():
        assert tab_ref.shape[0] // nch >= PEER_SEL * tb
        idx_v[...] = (lax.broadcasted_iota(jnp.int32, idx_v.shape, 0) * tb
                      + lax.broadcasted_iota(jnp.int32, idx_v.shape, 1))
        h_scr[slot_e] = jnp.zeros(h_scr.shape[1:], h_scr.dtype)
        gt_scr[slot_e] = jnp.zeros(gt_scr.shape[1:], gt_scr.dtype)
        publish_routing()
        for t0 in range(ahead):
            issue(t0, t0)

    h = _norm_mod(xr_ref[...], gain_ref[...], sc_ref[...], sh_ref[...])
    h_scr[slot_r] = h
    q = jnp.dot(h.astype(jnp.bfloat16), wq_ref[...], preferred_element_type=jnp.float32)
    for head in range(PEER_HEADS):
        q_scr[head] = q[:, head * PEER_DKEY:(head + 1) * PEER_DKEY]

    n_pairs = tb // (2 * PEER_NSLOT)
    assert n_pairs == PEER_HEADS

    one_each = [[j] for j in range(2 * PEER_NSLOT)]
    two_each = [[2 * j, 2 * j + 1] for j in range(PEER_NSLOT)]

    def body(gi, carry):
        t0 = pl.multiple_of(gi * (2 * PEER_NSLOT), 2 * PEER_NSLOT)
        group(t0, gi, one_each[:PEER_NSLOT], False)
        group(t0 + PEER_NSLOT, gi, one_each[PEER_NSLOT:], False)
        return carry

    lax.fori_loop(0, n_pairs - 1, body, 0)
    group(tb - 2 * PEER_NSLOT, n_pairs - 1, two_each, False)
    group(tb - PEER_NSLOT, n_pairs - 1, [[]] * PEER_NSLOT, True)


def pack_expert_table(u, v):
    ub = lax.bitcast_convert_type(u.astype(jnp.bfloat16), jnp.uint16).astype(jnp.uint32)
    vb = lax.bitcast_convert_type(v.astype(jnp.bfloat16), jnp.uint16).astype(jnp.uint32)
    return (ub << 16) | vb


def peer_layer(x, gain, sc, sh, gate, wq_bf, sk_bf, tab):
    n, d = x.shape
    tb = PEER_TB
    nblk = n // tb
    nch = d // LANES

    def cur(s):
        return jnp.minimum(s, nblk - 1)

    def prev(s):
        return jnp.maximum(s - 1, 0)

    def rows(arr, blk):
        g = arr.shape[0]
        if g == n:
            return arr, pl.BlockSpec((tb, d), lambda s: (blk(s), 0))
        per = n // g
        return arr[:, None, :], pl.BlockSpec((None, 1, d), lambda s: (blk(s) * tb // per, 0, 0))

    sc, sc_spec = rows(sc, cur)
    sh, sh_spec = rows(sh, cur)
    gate, gate_spec = rows(gate, prev)
    return pl.pallas_call(
        functools.partial(_peer_kernel, nblk),
        grid=(nblk + 1,),
        in_specs=[pl.BlockSpec((tb, d), lambda s: (cur(s), 0)),
                  pl.BlockSpec((1, d), lambda s: (0, 0)),
                  sc_spec, sh_spec,
                  pl.BlockSpec(wq_bf.shape, lambda s: (0, 0)),
                  pl.BlockSpec(sk_bf.shape, lambda s: (0, 0, 0)),
                  pl.BlockSpec((tb, d), lambda s: (prev(s), 0)),
                  gate_spec,
                  pl.BlockSpec(memory_space=pl.ANY)],
        out_specs=pl.BlockSpec((tb, d), lambda s: (prev(s), 0)),
        out_shape=jax.ShapeDtypeStruct((n, d), jnp.float32),
        scratch_shapes=[pltpu.VMEM((PEER_HEADS, tb, PEER_DKEY), jnp.float32),
                        pltpu.VMEM((2, tb, d), jnp.float32),
                        pltpu.VMEM((2, PEER_SEL, tb), jnp.float32),
                        pltpu.VMEM((PEER_SEL, tb), jnp.int32),
                        pltpu.SMEM((PEER_SEL, tb), jnp.int32),
                        pltpu.VMEM((2, PEER_NKEYS, tb), jnp.float32),
                        pltpu.VMEM((2, PEER_TOPK, tb), jnp.float32),
                        pltpu.VMEM((2, PEER_TOPK, tb), jnp.int32),
                        pltpu.VMEM((N_CAND, tb), jnp.float32),
                        pltpu.VMEM((N_CAND, tb), jnp.int32),
                        pltpu.VMEM((PEER_TOPK, tb), jnp.float32),
                        pltpu.VMEM((PEER_TOPK, tb), jnp.int32),
                        pltpu.VMEM((PEER_NSLOT, nch, PEER_SEL, LANES), jnp.uint32),
                        pltpu.SemaphoreType.DMA((PEER_NSLOT,)),
                        pltpu.SemaphoreType.DMA((1,))],
        compiler_params=pltpu.CompilerParams(dimension_semantics=("arbitrary",)),
        name="peer",
    )(x, gain, sc, sh, wq_bf, sk_bf, x, gate, tab.reshape(-1, LANES))


def _final_norm_kernel(x_ref, g_ref, o_ref):
    x = x_ref[...]
    o_ref[...] = x * lax.rsqrt(jnp.mean(x * x, axis=-1, keepdims=True) + EPS) * g_ref[...]


def final_norm(x, g):
    n, d = x.shape
    tm = min(ROW_TILE, n)
    return pl.pallas_call(
        _final_norm_kernel,
        grid=(n // tm,),
        in_specs=[pl.BlockSpec((tm, d), lambda i: (i, 0)), pl.BlockSpec((1, d), lambda i: (0, 0))],
        out_specs=pl.BlockSpec((tm, d), lambda i: (i, 0)),
        out_shape=jax.ShapeDtypeStruct((n, d), x.dtype),
        name="final_norm",
    )(x, g.reshape(1, d))


def _per_tile_rows(m, seq_len, tile):
    return m if seq_len % tile == 0 else jnp.repeat(m, seq_len, axis=0)


def trunk_layer(x, seq_len, mod, p, state):
    n, d = x.shape
    bsz = n // seq_len
    sh1, sc1, gt1, sh2, sc2, gt2 = (_per_tile_rows(m, seq_len, ROW_TILE) for m in jnp.split(mod, 6, axis=-1))
    z = mix_in(x, p['g_mix'], sc1, sh1, p['w_in'])
    k_new = z[:, Z_SK:Z_SK + SWA_KVW].reshape(bsz, seq_len, SWA_KVW)
    v_new = z[:, Z_SV:Z_SV + SWA_KVW].reshape(bsz, seq_len, SWA_KVW)
    if state is None:
        win_rows = min(WINDOW, PAST_LEN)
        c0 = jnp.zeros((bsz, MLSTM_HEADS, MLSTM_DH, MLSTM_DH), jnp.float32)
        n0 = jnp.zeros((bsz, MLSTM_HEADS, MLSTM_DH), jnp.float32)
        m0 = jnp.zeros((bsz, MLSTM_HEADS), jnp.float32)
        cbuf = jnp.zeros((bsz, CONV_WIDTH - 1, CONV_W), jnp.float32)
        oa = swa_banded(z, p['sinks'], seq_len)
        k_all, v_all = k_new, v_new
    else:
        kbuf, vbuf, cbuf, c0, n0, m0 = state
        win_rows = kbuf.shape[1]
        k_all = jnp.concatenate([kbuf.reshape(bsz, win_rows, SWA_KVW), k_new], axis=1)
        v_all = jnp.concatenate([vbuf.reshape(bsz, win_rows, SWA_KVW), v_new], axis=1)
        q = z[:, Z_SQ:Z_SQ + SWA_W].reshape(bsz, seq_len, SWA_W)
        oa = swa_cached(q, k_all, v_all, p['sinks']).reshape(n, SWA_W)
    hm, c1, n1, m1 = mlstm_layer(z, p['gate_b'], p['mh_g'], c0, n0, m0, seq_len)
    x, u = mix_out(x, gt1, hm, oa, z, cbuf, p['conv_w'], p['w_out'], seq_len)
    k_keep = k_all[:, -win_rows:].reshape(bsz, win_rows, SWA_KV_HEADS, SWA_DH)
    v_keep = v_all[:, -win_rows:].reshape(bsz, win_rows, SWA_KV_HEADS, SWA_DH)
    cbuf_new = jnp.concatenate([cbuf, u.reshape(bsz, seq_len, CONV_W)], axis=1)[:, -(CONV_WIDTH - 1):]
    x = peer_layer(x, p['g_ffn'], sc2, sh2, gt2, p['wq'], p['subkeys'], p['tab'])
    return x, (k_keep, v_keep, cbuf_new, c1, n1, m1)


def kernel(x_prompt, x_sample, cache_swa_k, cache_swa_v, state_conv, state_mlstm_C, state_mlstm_n, state_mlstm_m, c_prompt, c_sample, ada_w, ada_b, norm_mix_g, norm_ffn_g, w_in, w_out, mlstm_gate_b, mlstm_norm_g, swa_sinks, conv_w, peer_wq, peer_subkeys, peer_u, peer_v, final_g):
    bp, tp, d = x_prompt.shape
    bs, ts, _ = x_sample.shape
    xp, xs = x_prompt.reshape(bp * tp, d), x_sample.reshape(bs * ts, d)
    c_all = jnp.concatenate([c_prompt, c_sample], axis=0)
    bf = jnp.bfloat16
    n_gate = 2 * MLSTM_HEADS
    new_p, new_s = [], []
    for l in range(DEPTH):
        wl = w_in[l]
        w_perm = jnp.concatenate([wl[:, :Z_SQ], wl[:, Z_SQ + n_gate:], wl[:, Z_SQ:Z_SQ + n_gate],
                                  jnp.zeros((d, LANES - n_gate), wl.dtype)], axis=1)
        p = dict(g_mix=norm_mix_g[l].reshape(1, d), g_ffn=norm_ffn_g[l].reshape(1, d),
                 w_in=w_perm.astype(bf), w_out=w_out[l].astype(bf), gate_b=mlstm_gate_b[l],
                 mh_g=mlstm_norm_g[l], sinks=swa_sinks[l], conv_w=conv_w[l], wq=peer_wq[l].astype(bf),
                 subkeys=peer_subkeys[l].astype(bf), tab=pack_expert_table(peer_u[l], peer_v[l]))
        mod = adaln_mod(c_all, ada_w[l], ada_b[l])
        xp, sp = trunk_layer(xp, tp, mod[:bp], p, None)
        st = (cache_swa_k[l], cache_swa_v[l], state_conv[l], state_mlstm_C[l], state_mlstm_n[l], state_mlstm_m[l])
        xs, ss = trunk_layer(xs, ts, mod[bp:], p, st)
        new_p.append(sp)
        new_s.append(ss)
    y_prompt = final_norm(xp, final_g).reshape(bp, tp, d)
    y_sample = final_norm(xs, final_g).reshape(bs, ts, d)
    pk, pv, pc, pC, pn, pm = [jnp.stack(t) for t in zip(*new_p)]
    sk, sv, sc, sC, sn, sm = [jnp.stack(t) for t in zip(*new_s)]
    return (y_prompt, y_sample, pk, pv, pc, pC, pn, pm, sk, sv, sc, sC, sn, sm)
```

```python
import functools
import math

import jax
import jax.numpy as jnp
from jax import lax
from jax.experimental import pallas as pl
from jax.experimental.pallas import tpu as pltpu

D_MODEL = 1024
DEPTH = 2
PAST_LEN = 16384

MLSTM_W = D_MODEL // 2
MLSTM_HEADS = 4
MLSTM_DH = MLSTM_W // MLSTM_HEADS
MLSTM_CHUNK = 64
SWA_W = D_MODEL // 4
SWA_DH = 64
SWA_HEADS = SWA_W // SWA_DH
SWA_KV_HEADS = SWA_HEADS // 2
SWA_GROUP = SWA_HEADS // SWA_KV_HEADS
SWA_KVW = SWA_KV_HEADS * SWA_DH
WINDOW = 128
CONV_W = D_MODEL - MLSTM_W - SWA_W
CONV_WIDTH = 3
PEER_HEADS = 8
PEER_NKEYS = 128
PEER_DKEY = 256
HALF_KEY = PEER_DKEY // 2
PEER_TOPK = 16
PEER_SEL = PEER_HEADS * PEER_TOPK
EPS = 1e-6

LANES = 128
SUBLANES = 8
ROW_TILE = 512
PEER_TB = LANES
PEER_NSLOT = 8
MLSTM_SEQS = 2

Z_Q, Z_K, Z_V, Z_O = 0, MLSTM_W, 2 * MLSTM_W, 3 * MLSTM_W
Z_SQ = 4 * MLSTM_W
Z_SK = Z_SQ + SWA_W
Z_SV = Z_SK + SWA_KVW
Z_CB = Z_SV + SWA_KVW
Z_CC = Z_CB + CONV_W
Z_CH = Z_CC + CONV_W
Z_GATE = Z_CH + CONV_W
Z_W = Z_GATE + LANES
ALIBI = tuple(2.0 ** (-8.0 * h / SWA_HEADS) for h in range(1, SWA_HEADS + 1))


def _bdot(a, b, dims):
    return lax.dot_general(a.astype(jnp.bfloat16), b.astype(jnp.bfloat16), (dims, ((), ())),
                           preferred_element_type=jnp.float32)


def _row_spec(arr, n, tm, d):
    g = arr.shape[0]
    if g == n:
        return arr, pl.BlockSpec((tm, d), lambda i, *_: (i, 0))
    per = n // g
    return arr[:, None, :], pl.BlockSpec((None, 1, d), lambda i, *_: (i * tm // per, 0, 0))


def _norm_mod(x, gain, sc, sh):
    y = x * lax.rsqrt(jnp.mean(x * x, axis=-1, keepdims=True) + EPS) * gain
    return y * (1.0 + sc) + sh


def _adaln_kernel(c_ref, w_ref, b_ref, o_ref):
    c = c_ref[...]
    o_ref[...] = _bdot(c * jax.nn.sigmoid(c), w_ref[...], ((1,), (0,))) + b_ref[...]


def adaln_mod(c, w, b):
    bsz, d = c.shape
    m = w.shape[1]
    tn = 512
    return pl.pallas_call(
        _adaln_kernel,
        grid=(m // tn,),
        in_specs=[pl.BlockSpec((bsz, d), lambda j: (0, 0)),
                  pl.BlockSpec((d, tn), lambda j: (0, j)),
                  pl.BlockSpec((1, tn), lambda j: (0, j))],
        out_specs=pl.BlockSpec((bsz, tn), lambda j: (0, j)),
        out_shape=jax.ShapeDtypeStruct((bsz, m), jnp.float32),
        name="adaln_mod",
    )(c, w, b.reshape(1, m))


def _mix_in_kernel(x_ref, gain_ref, sc_ref, sh_ref, w_ref, z_ref):
    h = _norm_mod(x_ref[...], gain_ref[...], sc_ref[...], sh_ref[...])
    z_ref[...] = jnp.dot(h.astype(jnp.bfloat16), w_ref[...], preferred_element_type=jnp.float32)


def mix_in(x, gain, sc, sh, w_bf):
    n, d = x.shape
    zw = w_bf.shape[1]
    tm, tn = min(ROW_TILE, n), zw
    sc, sc_spec = _row_spec(sc, n, tm, d)
    sh, sh_spec = _row_spec(sh, n, tm, d)
    return pl.pallas_call(
        _mix_in_kernel,
        grid=(n // tm, zw // tn),
        in_specs=[pl.BlockSpec((tm, d), lambda i, j: (i, 0)),
                  pl.BlockSpec((1, d), lambda i, j: (0, 0)),
                  sc_spec, sh_spec,
                  pl.BlockSpec((d, tn), lambda i, j: (0, j))],
        out_specs=pl.BlockSpec((tm, tn), lambda i, j: (i, j)),
        out_shape=jax.ShapeDtypeStruct((n, zw), jnp.float32),
        name="mix_in",
    )(x, gain, sc, sh, w_bf)


def _log_sigmoid(x):
    return jnp.minimum(x, 0.0) - jnp.log(1.0 + jnp.exp(-jnp.abs(x)))


def _mlstm_kernel(chunk, valid, q_ref, k_ref, v_ref, o_ref, gate_ref, gb_ref, g_ref, c0_ref, n0_ref, m0_ref,
                  hm_ref, c_ref, n_ref, m_ref):
    nb, tc = q_ref.shape[:2]
    nh, dh, L = MLSTM_HEADS, MLSTM_DH, chunk

    @pl.when(pl.program_id(1) == 0)
    def _():
        c_ref[...] = c0_ref[...]
        n_ref[...] = n0_ref[...]
        m_ref[...] = m0_ref[...]

    row = lax.broadcasted_iota(jnp.int32, (L, L), 0)
    col = lax.broadcasted_iota(jnp.int32, (L, L), 1)
    eye = row == col
    visible = (col <= row) & (col < valid)
    rcol = lax.broadcasted_iota(jnp.int32, (L, 1), 0)
    gb = gb_ref[...]
    gain = g_ref[...]

    def to_row(x_col):
        return jnp.sum(jnp.where(eye, x_col, 0.0), axis=0, keepdims=True)

    def one_chunk(ci, carry):
        r0 = pl.multiple_of(ci * L, L)
        for sq, hd in [(a, b) for a in range(nb) for b in range(nh)]:
            pre = gate_ref[sq, pl.ds(r0, L), :] + gb
            sl = slice(hd * dh, (hd + 1) * dh)
            q = q_ref[sq, pl.ds(r0, L), sl]
            k = k_ref[sq, pl.ds(r0, L), sl] * (dh ** -0.5)
            v = v_ref[sq, pl.ds(r0, L), sl]
            ig = pre[:, hd:hd + 1]
            lf = jnp.where(rcol < valid, _log_sigmoid(pre[:, nh + hd:nh + hd + 1]), 0.0)
            b = lf
            s = 1
            while s < L:
                b = b + jnp.where(rcol >= s, pltpu.roll(b, s, 0), 0.0)
                s *= 2
            cmat, nrow, m_prev = c_ref[sq, hd], n_ref[sq, hd], m_ref[sq, hd][:, 0:1]
            a = b + m_prev
            d = jnp.where(visible, b - to_row(b) + to_row(ig), -jnp.inf)
            m_t = jnp.maximum(a, jnp.max(d, axis=-1, keepdims=True))
            w_inter = jnp.exp(a - m_t)
            w_intra = jnp.exp(d - m_t)
            qk = _bdot(q, k, ((1,), (1,))) * w_intra
            num = w_inter * _bdot(q, cmat, ((1,), (1,))) + _bdot(qk, v, ((1,), (0,)))
            den = w_inter * jnp.sum(q * nrow, axis=-1, keepdims=True) + jnp.sum(qk, axis=-1, keepdims=True)
            h = num / jnp.maximum(jnp.abs(den), jnp.exp(-m_t))
            hn = h * lax.rsqrt(jnp.mean(h * h, axis=-1, keepdims=True) + EPS) * gain[:, sl]
            hm_ref[sq, pl.ds(r0, L), sl] = hn * jax.nn.sigmoid(o_ref[sq, pl.ds(r0, L), sl])
            m_last = m_t[L - 1:L]
            wl_inter = w_inter[L - 1:L]
            wl = jnp.where(rcol < valid, jnp.exp(b[L - 1:L] - b + ig - m_last), 0.0)
            c_ref[sq, hd] = wl_inter * cmat + _bdot(wl * v, k, ((0,), (0,)))
            n_ref[sq, hd] = wl_inter * nrow + jnp.sum(wl * k, axis=0, keepdims=True)
            m_ref[sq, hd] = jnp.broadcast_to(m_last, (1, dh))
        return carry

    lax.fori_loop(0, tc // L, one_chunk, 0)


def mlstm_layer(z, gate_b, mh_g, c0, n0, m0, seq_len):
    bsz = z.shape[0] // seq_len
    nh, dh = MLSTM_HEADS, MLSTM_DH
    chunk = math.gcd(seq_len, MLSTM_CHUNK)
    valid, padded = chunk, seq_len
    if chunk % SUBLANES:
        assert seq_len < SUBLANES
        chunk = padded = SUBLANES
        z = jnp.pad(z.reshape(bsz, seq_len, -1), ((0, 0), (0, padded - seq_len), (0, 0)))
    z = z.reshape(bsz, padded, -1)
    nb = MLSTM_SEQS
    assert bsz % nb == 0
    tc = min(padded, ROW_TILE)
    w = nh * dh
    gb = jnp.zeros((1, LANES), jnp.float32).at[0, :2 * nh].set(gate_b)

    def zcol(off, width):
        return pl.BlockSpec((nb, tc, width), lambda b, c: (b, c, off // width))

    def state(shape):
        return pl.BlockSpec((nb,) + shape, lambda b, c: (b,) + (0,) * len(shape))

    hm, c1, n1, m1 = pl.pallas_call(
        functools.partial(_mlstm_kernel, chunk, valid),
        grid=(bsz // nb, padded // tc),
        in_specs=[zcol(Z_Q, w), zcol(Z_K, w), zcol(Z_V, w), zcol(Z_O, w), zcol(Z_GATE, LANES),
                  pl.BlockSpec((1, LANES), lambda b, c: (0, 0)),
                  pl.BlockSpec((1, w), lambda b, c: (0, 0)),
                  state((nh, dh, dh)), state((nh, 1, dh)), state((nh, 1, dh))],
        out_specs=[pl.BlockSpec((nb, tc, w), lambda b, c: (b, c, 0)),
                   state((nh, dh, dh)), state((nh, 1, dh)), state((nh, 1, dh))],
        out_shape=[jax.ShapeDtypeStruct((bsz, padded, w), jnp.float32),
                   jax.ShapeDtypeStruct((bsz, nh, dh, dh), jnp.float32),
                   jax.ShapeDtypeStruct((bsz, nh, 1, dh), jnp.float32),
                   jax.ShapeDtypeStruct((bsz, nh, 1, dh), jnp.float32)],
        compiler_params=pltpu.CompilerParams(dimension_semantics=("arbitrary", "arbitrary")),
        name="mlstm",
    )(z, z, z, z, z, gb, mh_g.reshape(1, w), c0, n0[:, :, None, :],
      jnp.broadcast_to(m0[:, :, None, None], (bsz, nh, 1, dh)))
    return hm[:, :seq_len].reshape(bsz * seq_len, w), c1, n1[:, :, 0, :], m1[:, :, 0, 0]


def _attend(q, kb, vb, dist, visible, slope, sink):
    s = _bdot(q, kb, ((1,), (1,))) * (SWA_DH ** -0.5) - slope * dist.astype(jnp.float32)
    s = jnp.where(visible, s, -jnp.inf)
    mx = jnp.maximum(jnp.max(s, axis=-1, keepdims=True), sink)
    e = jnp.exp(s - mx)
    p = e / (jnp.sum(e, axis=-1, keepdims=True) + jnp.exp(sink - mx))
    return _bdot(p, vb, ((1,), (0,)))


def _swa_banded_kernel(q_ref, kp_ref, kc_ref, vp_ref, vc_ref, sink_ref, o_ref):
    j = pl.program_id(1)
    w = q_ref.shape[0]
    t = lax.broadcasted_iota(jnp.int32, (w, 2 * w), 0)
    i = lax.broadcasted_iota(jnp.int32, (w, 2 * w), 1)
    dist = t + w - i
    visible = (dist >= 0) & (dist <= WINDOW) & ((j > 0) | (i >= w))
    sinks = sink_ref[...]
    for kv in range(SWA_KV_HEADS):
        ks = slice(kv * SWA_DH, (kv + 1) * SWA_DH)
        kb = jnp.concatenate([kp_ref[:, ks], kc_ref[:, ks]], axis=0)
        vb = jnp.concatenate([vp_ref[:, ks], vc_ref[:, ks]], axis=0)
        for g in range(SWA_GROUP):
            h = kv * SWA_GROUP + g
            hs = slice(h * SWA_DH, (h + 1) * SWA_DH)
            o_ref[:, hs] = _attend(q_ref[:, hs], kb, vb, dist, visible, ALIBI[h], sinks[:, h:h + 1])


def swa_banded(z, sinks, seq_len):
    n = z.shape[0]
    w = WINDOW
    nb = seq_len // w
    sink_row = jnp.zeros((1, LANES), jnp.float32).at[0, :SWA_HEADS].set(sinks)

    def cur(off, width):
        return pl.BlockSpec((w, width), lambda b, j: (b * nb + j, off // width))

    def prev(off, width):
        return pl.BlockSpec((w, width), lambda b, j: (b * nb + jnp.maximum(j - 1, 0), off // width))

    return pl.pallas_call(
        _swa_banded_kernel,
        grid=(n // seq_len, nb),
        in_specs=[cur(Z_SQ, SWA_W), prev(Z_SK, SWA_KVW), cur(Z_SK, SWA_KVW), prev(Z_SV, SWA_KVW),
                  cur(Z_SV, SWA_KVW), pl.BlockSpec((1, LANES), lambda b, j: (0, 0))],
        out_specs=pl.BlockSpec((w, SWA_W), lambda b, j: (b * nb + j, 0)),
        out_shape=jax.ShapeDtypeStruct((n, SWA_W), jnp.float32),
        name="swa_banded",
    )(z, z, z, z, z, sink_row)


def _swa_cached_kernel(t_new, n_keys, q_ref, k_ref, v_ref, sink_ref, o_ref):
    m = q_ref.shape[1]
    nkp = k_ref.shape[0]
    r = lax.broadcasted_iota(jnp.int32, (m, nkp), 0)
    i = lax.broadcasted_iota(jnp.int32, (m, nkp), 1)
    first_key_pos = PAST_LEN - (n_keys - t_new)
    dist = (n_keys - t_new) + r % t_new - i
    visible = (dist >= 0) & (dist <= WINDOW) & (i < n_keys) & (first_key_pos + i >= 0)
    rg = lax.broadcasted_iota(jnp.int32, (m, 1), 0) // t_new
    sinks = sink_ref[...]
    for kv in range(SWA_KV_HEADS):
        ks = slice(kv * SWA_DH, (kv + 1) * SWA_DH)
        h0 = kv * SWA_GROUP
        slope = jnp.where(rg == 0, ALIBI[h0], ALIBI[h0 + 1])
        sink = jnp.where(rg == 0, sinks[:, h0:h0 + 1], sinks[:, h0 + 1:h0 + 2])
        o_ref[kv] = _attend(q_ref[kv], k_ref[:, ks], v_ref[:, ks], dist, visible, slope, sink)


def swa_cached(q, k_all, v_all, sinks):
    assert SWA_GROUP == 2
    bsz, t_new, _ = q.shape
    n_keys = k_all.shape[1]
    nkp = -(-n_keys // SUBLANES) * SUBLANES
    pad = ((0, 0), (0, nkp - n_keys), (0, 0))
    k_all, v_all = jnp.pad(k_all, pad), jnp.pad(v_all, pad)
    m = SWA_GROUP * t_new
    qs = q.reshape(bsz, t_new, SWA_KV_HEADS, SWA_GROUP, SWA_DH).transpose(0, 2, 3, 1, 4)
    qs = qs.reshape(bsz, SWA_KV_HEADS, m, SWA_DH)
    sink_row = jnp.zeros((1, LANES), jnp.float32).at[0, :SWA_HEADS].set(sinks)
    o = pl.pallas_call(
        functools.partial(_swa_cached_kernel, t_new, n_keys),
        grid=(bsz,),
        in_specs=[pl.BlockSpec((None, SWA_KV_HEADS, m, SWA_DH), lambda b: (b, 0, 0, 0)),
                  pl.BlockSpec((None, nkp, SWA_KVW), lambda b: (b, 0, 0)),
                  pl.BlockSpec((None, nkp, SWA_KVW), lambda b: (b, 0, 0)),
                  pl.BlockSpec((1, LANES), lambda b: (0, 0))],
        out_specs=pl.BlockSpec((None, SWA_KV_HEADS, m, SWA_DH), lambda b: (b, 0, 0, 0)),
        out_shape=jax.ShapeDtypeStruct((bsz, SWA_KV_HEADS, m, SWA_DH), jnp.float32),
        name="swa_cached",
    )(qs, k_all, v_all, sink_row)
    o = o.reshape(bsz, SWA_KV_HEADS, SWA_GROUP, t_new, SWA_DH).transpose(0, 3, 1, 2, 4)
    return o.reshape(bsz, t_new, SWA_W)


def _mix_out_kernel(seq_len, x_ref, gate_ref, hm_ref, oa_ref, cb_ref, cc_ref, ch_ref, hcc_ref, hch_ref,
                    pa_ref, pb_ref, cw_ref, w_ref, o_ref, u_ref):
    i = pl.program_id(0)
    tm = x_ref.shape[0]
    u = cc_ref[...] * ch_ref[...]
    u_ref[...] = u
    hu = hcc_ref[...] * hch_ref[...]
    r = lax.broadcasted_iota(jnp.int32, u.shape, 0)
    p = (i * tm + r) % seq_len
    pa, pb = pa_ref[...], pb_ref[...]
    last, last2 = hu[SUBLANES - 1:SUBLANES], hu[SUBLANES - 2:SUBLANES - 1]
    u1 = jnp.where(r >= 1, pltpu.roll(u, 1, 0), last)
    u2 = jnp.where(r >= 2, pltpu.roll(u, 2, 0), jnp.where(r == 1, last, last2))
    s1 = jnp.where(p >= 1, u1, pa)
    s2 = jnp.where(p >= 2, u2, jnp.where(p == 1, pa, pb))
    cw = cw_ref[...]
    yc = cb_ref[...] * (cw[0:1] * s2 + cw[1:2] * s1 + cw[2:3] * u)
    cat = jnp.concatenate([hm_ref[...], oa_ref[...], yc], axis=-1)
    mix = jnp.dot(cat.astype(jnp.bfloat16), w_ref[...], preferred_element_type=jnp.float32)
    o_ref[...] = x_ref[...] + gate_ref[...] * mix


def mix_out(x, gate, hm, oa, z, conv_prev, conv_w, w_out_bf, seq_len):
    n, d = x.shape
    c = CONV_W
    tm = min(ROW_TILE, n)
    gate, gate_spec = _row_spec(gate, n, tm, d)
    if seq_len % tm:
        pa = jnp.repeat(conv_prev[:, 1], seq_len, axis=0)
        pb = jnp.repeat(conv_prev[:, 0], seq_len, axis=0)
    else:
        pa, pb = conv_prev[:, 1], conv_prev[:, 0]
    pa, pa_spec = _row_spec(pa, n, tm, c)
    pb, pb_spec = _row_spec(pb, n, tm, c)
    cw = jnp.zeros((SUBLANES, c), jnp.float32).at[:CONV_WIDTH].set(conv_w)

    def zcol(off, width):
        return pl.BlockSpec((tm, width), lambda i: (i, off // width))

    def zhalo(off):
        return pl.BlockSpec((SUBLANES, c), lambda i: (jnp.maximum(i * (tm // SUBLANES) - 1, 0), off // c))

    return pl.pallas_call(
        functools.partial(_mix_out_kernel, seq_len),
        grid=(n // tm,),
        in_specs=[pl.BlockSpec((tm, d), lambda i: (i, 0)), gate_spec,
                  pl.BlockSpec((tm, MLSTM_W), lambda i: (i, 0)),
                  pl.BlockSpec((tm, SWA_W), lambda i: (i, 0)),
                  zcol(Z_CB, c), zcol(Z_CC, c), zcol(Z_CH, c), zhalo(Z_CC), zhalo(Z_CH),
                  pa_spec, pb_spec,
                  pl.BlockSpec((SUBLANES, c), lambda i: (0, 0)),
                  pl.BlockSpec((d, d), lambda i: (0, 0))],
        out_specs=[pl.BlockSpec((tm, d), lambda i: (i, 0)), pl.BlockSpec((tm, c), lambda i: (i, 0))],
        out_shape=[jax.ShapeDtypeStruct((n, d), jnp.float32), jax.ShapeDtypeStruct((n, c), jnp.float32)],
        name="mix_out",
    )(x, gate, hm, oa, z, z, z, z, z, pa, pb, cw, w_out_bf)


def _pair_candidates(s1, i1, s2, i2):
    k, sub = PEER_TOPK, SUBLANES
    assert k == 2 * sub
    row = lax.broadcasted_iota(jnp.int32, (sub, s1.shape[1]), 0)
    sums, ids = [], []

    def emit(a, b0, nvalid):
        c = s1[a:a + 1] + s2[b0:b0 + sub]
        sums.append(c if nvalid >= sub else jnp.where(row < nvalid, c, -jnp.inf))
        ids.append(i1[a:a + 1] * PEER_NKEYS + i2[b0:b0 + sub])

    for a in range(sub):
        nb = k // (a + 1)
        for b0 in range(0, nb, sub):
            emit(a, b0, nb - b0)
    sums.append(s1[sub:] + s2[0:1])
    ids.append(i1[sub:] * PEER_NKEYS + i2[0:1])
    return jnp.concatenate(sums, axis=0), jnp.concatenate(ids, axis=0)


N_CAND = SUBLANES * (sum(-(-(PEER_TOPK // (a + 1)) // SUBLANES) for a in range(SUBLANES)) + 1)


def _extract_top(s, rows, payload=None):
    m = jnp.max(s, axis=0, keepdims=True)
    r = jnp.min(jnp.where(s == m, rows, jnp.int32(s.shape[0])), axis=0, keepdims=True)
    hit = rows == r
    ident = r if payload is None else jnp.max(jnp.where(hit, payload, -1), axis=0, keepdims=True)
    return m, ident, jnp.where(hit, -jnp.inf, s)


def _gelu_tanh(x):
    return 0.5 * x * (1.0 + jnp.tanh(math.sqrt(2.0 / math.pi) * (x + 0.044715 * (x * x * x))))


def _peer_kernel(nblk, xr_ref, gain_ref, sc_ref, sh_ref, wq_ref, sk_ref, xres_ref, gate_ref, tab_ref,
                 o_ref, q_scr, h_scr, gt_scr, idx_v, idx_s, s_scr, tv_scr, ti_scr, cv_scr, ci_scr, fv_scr, fi_scr,
                 buf, sem, idx_sem):
    s = pl.program_id(0)
    tb, d = xr_ref.shape
    nch = d // LANES
    assert tb == LANES
    slot_r = s % 2
    slot_e = 1 - slot_r
    per_token_gate = gate_ref.shape[0] == tb

    def row_copy(e, slot, r):
        src = tab_ref.at[pl.ds(pl.multiple_of(e * nch, nch), nch)]
        return pltpu.make_async_copy(src, buf.at[slot, :, r, :], sem.at[slot])

    def issue(t, slot):
        for r in range(PEER_SEL):
            row_copy(idx_s[r, t], slot, r).start(priority=r % 2)

    def wait(slot):
        pltpu.make_async_copy(tab_ref.at[pl.ds(0, PEER_SEL * nch)], buf.at[slot], sem.at[slot]).wait()

    lane = lax.broadcasted_iota(jnp.int32, (PEER_SEL, tb), 1)

    def ffn(t, slot):
        hrow = h_scr[slot_e, pl.ds(t, 1), :]
        part = None
        for c in range(nch):
            u = lax.bitcast_convert_type(buf[slot, c] & jnp.uint32(0xFFFF0000), jnp.float32)
            term = u * hrow[:, c * LANES:(c + 1) * LANES]
            part = term if part is None else part + term
        sdot = jnp.sum(part, axis=-1, keepdims=True)
        g = jnp.sum(jnp.where(lane == t, gt_scr[slot_e], 0.0), axis=-1, keepdims=True)
        w = g * _gelu_tanh(sdot)
        y = jnp.concatenate(
            [jnp.sum(lax.bitcast_convert_type(buf[slot, c] << 16, jnp.float32) * w, axis=0, keepdims=True)
             for c in range(nch)], axis=1)
        gate = gate_ref[pl.ds(t, 1), :] if per_token_gate else gate_ref[...]
        return xres_ref[pl.ds(t, 1), :] + gate * y

    ahead = PEER_NSLOT - 1

    key_rows = lax.broadcasted_iota(jnp.int32, (PEER_NKEYS, tb), 0)
    cand_rows = lax.broadcasted_iota(jnp.int32, (N_CAND, tb), 0)

    def sl_scores(head):
        qh = q_scr[head].astype(jnp.bfloat16)
        for p in range(2):
            s_scr[p] = lax.dot_general(sk_ref[p], qh[:, p * HALF_KEY:(p + 1) * HALF_KEY],
                                       (((1,), (1,)), ((), ())), preferred_element_type=jnp.float32)

    def sl_stage1(k0, cnt, head):
        for p in range(2):
            sc_ = s_scr[p]
            for k in range(k0, k0 + cnt):
                m, r, sc_ = _extract_top(sc_, key_rows)
                tv_scr[p, k:k + 1, :] = m
                ti_scr[p, k:k + 1, :] = r
            s_scr[p] = sc_

    def sl_cand(head):
        c, ci = _pair_candidates(tv_scr[0], ti_scr[0], tv_scr[1], ti_scr[1])
        cv_scr[...] = c
        ci_scr[...] = ci

    def sl_stage2(k0, cnt, head):
        c, ci = cv_scr[...], ci_scr[...]
        for k in range(k0, k0 + cnt):
            m, e, c = _extract_top(c, cand_rows, ci)
            fv_scr[k:k + 1, :] = m
            fi_scr[k:k + 1, :] = e
        cv_scr[...] = c

    def sl_out(head):
        top_s = fv_scr[...]
        e = jnp.exp(top_s - jnp.max(top_s, axis=0, keepdims=True))
        off = pl.multiple_of(head * PEER_TOPK, PEER_TOPK)
        idx_v[pl.ds(off, PEER_TOPK), :] = fi_scr[...]
        gt_scr[slot_r, pl.ds(off, PEER_TOPK), :] = e / jnp.sum(e, axis=0, keepdims=True)

    per_slice = PEER_TOPK // PEER_NSLOT
    slices = [sl_scores]
    slices += [functools.partial(sl_stage1, k0, per_slice) for k0 in range(0, PEER_TOPK, per_slice)]
    slices += [sl_cand]
    k0 = 0
    for cnt in (4, 3, 3, 3, 3):
        slices.append(functools.partial(sl_stage2, k0, cnt))
        k0 += cnt
    slices += [sl_out]
    assert len(slices) == 2 * PEER_NSLOT and k0 == PEER_TOPK

    def publish_routing():
        cp = pltpu.make_async_copy(idx_v, idx_s, idx_sem.at[0])
        cp.start()
        cp.wait()

    def group(t0, head, todo, last):
        for j in range(PEER_NSLOT):
            t = t0 + j
            wait(j)
            out = ffn(t, j)
            if not last or j == 0:
                issue(t + ahead, (j + ahead) % PEER_NSLOT)
            for i in todo[j]:
                slices[i](head)
            if last:
                @pl.when(s < nblk)
                def _():
                    if j == 0:
                        publish_routing()
                    else:
                        issue(j - 1, j - 1)
            o_ref[pl.ds(t, 1), :] = out

    @pl.when(s == 0)
    def _():
        assert tab_ref.shape[0] // nch >= PEER_SEL * tb
        idx_v[...] = (lax.broadcasted_iota(jnp.int32, idx_v.shape, 0) * tb
                      + lax.broadcasted_iota(jnp.int32, idx_v.shape, 1))
        h_scr[slot_e] = jnp.zeros(h_scr.shape[1:], h_scr.dtype)
        gt_scr[slot_e] = jnp.zeros(gt_scr.shape[1:], gt_scr.dtype)
        publish_routing()
        for t0 in range(ahead):
            issue(t0, t0)

    h = _norm_mod(xr_ref[...], gain_ref[...], sc_ref[...], sh_ref[...])
    h_scr[slot_r] = h
    q = jnp.dot(h.astype(jnp.bfloat16), wq_ref[...], preferred_element_type=jnp.float32)
    for head in range(PEER_HEADS):
        q_scr[head] = q[:, head * PEER_DKEY:(head + 1) * PEER_DKEY]

    n_pairs = tb // (2 * PEER_NSLOT)
    assert n_pairs == PEER_HEADS

    one_each = [[j] for j in range(2 * PEER_NSLOT)]
    two_each = [[2 * j, 2 * j + 1] for j in range(PEER_NSLOT)]

    def body(gi, carry):
        t0 = pl.multiple_of(gi * (2 * PEER_NSLOT), 2 * PEER_NSLOT)
        group(t0, gi, one_each[:PEER_NSLOT], False)
        group(t0 + PEER_NSLOT, gi, one_each[PEER_NSLOT:], False)
        return carry

    lax.fori_loop(0, n_pairs - 1, body, 0)
    group(tb - 2 * PEER_NSLOT, n_pairs - 1, two_each, False)
    group(tb - PEER_NSLOT, n_pairs - 1, [[]] * PEER_NSLOT, True)


def pack_expert_table(u, v):
    ub = lax.bitcast_convert_type(u.astype(jnp.bfloat16), jnp.uint16).astype(jnp.uint32)
    vb = lax.bitcast_convert_type(v.astype(jnp.bfloat16), jnp.uint16).astype(jnp.uint32)
    return (ub << 16) | vb


def peer_layer(x, gain, sc, sh, gate, wq_bf, sk_bf, tab):
    n, d = x.shape
    tb = PEER_TB
    nblk = n // tb
    nch = d // LANES

    def cur(s):
        return jnp.minimum(s, nblk - 1)

    def prev(s):
        return jnp.maximum(s - 1, 0)

    def rows(arr, blk):
        g = arr.shape[0]
        if g == n:
            return arr, pl.BlockSpec((tb, d), lambda s: (blk(s), 0))
        per = n // g
        return arr[:, None, :], pl.BlockSpec((None, 1, d), lambda s: (blk(s) * tb // per, 0, 0))

    sc, sc_spec = rows(sc, cur)
    sh, sh_spec = rows(sh, cur)
    gate, gate_spec = rows(gate, prev)
    return pl.pallas_call(
        functools.partial(_peer_kernel, nblk),
        grid=(nblk + 1,),
        in_specs=[pl.BlockSpec((tb, d), lambda s: (cur(s), 0)),
                  pl.BlockSpec((1, d), lambda s: (0, 0)),
                  sc_spec, sh_spec,
                  pl.BlockSpec(wq_bf.shape, lambda s: (0, 0)),
                  pl.BlockSpec(sk_bf.shape, lambda s: (0, 0, 0)),
                  pl.BlockSpec((tb, d), lambda s: (prev(s), 0)),
                  gate_spec,
                  pl.BlockSpec(memory_space=pl.ANY)],
        out_specs=pl.BlockSpec((tb, d), lambda s: (prev(s), 0)),
        out_shape=jax.ShapeDtypeStruct((n, d), jnp.float32),
        scratch_shapes=[pltpu.VMEM((PEER_HEADS, tb, PEER_DKEY), jnp.float32),
                        pltpu.VMEM((2, tb, d), jnp.float32),
                        pltpu.VMEM((2, PEER_SEL, tb), jnp.float32),
                        pltpu.VMEM((PEER_SEL, tb), jnp.int32),
                        pltpu.SMEM((PEER_SEL, tb), jnp.int32),
                        pltpu.VMEM((2, PEER_NKEYS, tb), jnp.float32),
                        pltpu.VMEM((2, PEER_TOPK, tb), jnp.float32),
                        pltpu.VMEM((2, PEER_TOPK, tb), jnp.int32),
                        pltpu.VMEM((N_CAND, tb), jnp.float32),
                        pltpu.VMEM((N_CAND, tb), jnp.int32),
                        pltpu.VMEM((PEER_TOPK, tb), jnp.float32),
                        pltpu.VMEM((PEER_TOPK, tb), jnp.int32),
                        pltpu.VMEM((PEER_NSLOT, nch, PEER_SEL, LANES), jnp.uint32),
                        pltpu.SemaphoreType.DMA((PEER_NSLOT,)),
                        pltpu.SemaphoreType.DMA((1,))],
        compiler_params=pltpu.CompilerParams(dimension_semantics=("arbitrary",)),
        name="peer",
    )(x, gain, sc, sh, wq_bf, sk_bf, x, gate, tab.reshape(-1, LANES))


def _final_norm_kernel(x_ref, g_ref, o_ref):
    x = x_ref[...]
    o_ref[...] = x * lax.rsqrt(jnp.mean(x * x, axis=-1, keepdims=True) + EPS) * g_ref[...]


def final_norm(x, g):
    n, d = x.shape
    tm = min(ROW_TILE, n)
    return pl.pallas_call(
        _final_norm_kernel,
        grid=(n // tm,),
        in_specs=[pl.BlockSpec((tm, d), lambda i: (i, 0)), pl.BlockSpec((1, d), lambda i: (0, 0))],
        out_specs=pl.BlockSpec((tm, d), lambda i: (i, 0)),
        out_shape=jax.ShapeDtypeStruct((n, d), x.dtype),
        name="final_norm",
    )(x, g.reshape(1, d))


def _per_tile_rows(m, seq_len, tile):
    return m if seq_len % tile == 0 else jnp.repeat(m, seq_len, axis=0)


def trunk_layer(x, seq_len, mod, p, state):
    n, d = x.shape
    bsz = n // seq_len
    sh1, sc1, gt1, sh2, sc2, gt2 = (_per_tile_rows(m, seq_len, ROW_TILE) for m in jnp.split(mod, 6, axis=-1))
    z = mix_in(x, p['g_mix'], sc1, sh1, p['w_in'])
    k_new = z[:, Z_SK:Z_SK + SWA_KVW].reshape(bsz, seq_len, SWA_KVW)
    v_new = z[:, Z_SV:Z_SV + SWA_KVW].reshape(bsz, seq_len, SWA_KVW)
    if state is None:
        win_rows = min(WINDOW, PAST_LEN)
        c0 = jnp.zeros((bsz, MLSTM_HEADS, MLSTM_DH, MLSTM_DH), jnp.float32)
        n0 = jnp.zeros((bsz, MLSTM_HEADS, MLSTM_DH), jnp.float32)
        m0 = jnp.zeros((bsz, MLSTM_HEADS), jnp.float32)
        cbuf = jnp.zeros((bsz, CONV_WIDTH - 1, CONV_W), jnp.float32)
        oa = swa_banded(z, p['sinks'], seq_len)
        k_all, v_all = k_new, v_new
    else:
        kbuf, vbuf, cbuf, c0, n0, m0 = state
        win_rows = kbuf.shape[1]
        k_all = jnp.concatenate([kbuf.reshape(bsz, win_rows, SWA_KVW), k_new], axis=1)
        v_all = jnp.concatenate([vbuf.reshape(bsz, win_rows, SWA_KVW), v_new], axis=1)
        q = z[:, Z_SQ:Z_SQ + SWA_W].reshape(bsz, seq_len, SWA_W)
        oa = swa_cached(q, k_all, v_all, p['sinks']).reshape(n, SWA_W)
    hm, c1, n1, m1 = mlstm_layer(z, p['gate_b'], p['mh_g'], c0, n0, m0, seq_len)
    x, u = mix_out(x, gt1, hm, oa, z, cbuf, p['conv_w'], p['w_out'], seq_len)
    k_keep = k_all[:, -win_rows:].reshape(bsz, win_rows, SWA_KV_HEADS, SWA_DH)
    v_keep = v_all[:, -win_rows:].reshape(bsz, win_rows, SWA_KV_HEADS, SWA_DH)
    cbuf_new = jnp.concatenate([cbuf, u.reshape(bsz, seq_len, CONV_W)], axis=1)[:, -(CONV_WIDTH - 1):]
    x = peer_layer(x, p['g_ffn'], sc2, sh2, gt2, p['wq'], p['subkeys'], p['tab'])
    return x, (k_keep, v_keep, cbuf_new, c1, n1, m1)


def kernel(x_prompt, x_sample, cache_swa_k, cache_swa_v, state_conv, state_mlstm_C, state_mlstm_n, state_mlstm_m, c_prompt, c_sample, ada_w, ada_b, norm_mix_g, norm_ffn_g, w_in, w_out, mlstm_gate_b, mlstm_norm_g, swa_sinks, conv_w, peer_wq, peer_subkeys, peer_u, peer_v, final_g):
    bp, tp, d = x_prompt.shape
    bs, ts, _ = x_sample.shape
    xp, xs = x_prompt.reshape(bp * tp, d), x_sample.reshape(bs * ts, d)
    c_all = jnp.concatenate([c_prompt, c_sample], axis=0)
    bf = jnp.bfloat16
    n_gate = 2 * MLSTM_HEADS
    new_p, new_s = [], []
    for l in range(DEPTH):
        wl = w_in[l]
        w_perm = jnp.concatenate([wl[:, :Z_SQ], wl[:, Z_SQ + n_gate:], wl[:, Z_SQ:Z_SQ + n_gate],
                                  jnp.zeros((d, LANES - n_gate), wl.dtype)], axis=1)
        p = dict(g_mix=norm_mix_g[l].reshape(1, d), g_ffn=norm_ffn_g[l].reshape(1, d),
                 w_in=w_perm.astype(bf), w_out=w_out[l].astype(bf), gate_b=mlstm_gate_b[l],
                 mh_g=mlstm_norm_g[l], sinks=swa_sinks[l], conv_w=conv_w[l], wq=peer_wq[l].astype(bf),
                 subkeys=peer_subkeys[l].astype(bf), tab=pack_expert_table(peer_u[l], peer_v[l]))
        mod = adaln_mod(c_all, ada_w[l], ada_b[l])
        xp, sp = trunk_layer(xp, tp, mod[:bp], p, None)
        st = (cache_swa_k[l], cache_swa_v[l], state_conv[l], state_mlstm_C[l], state_mlstm_n[l], state_mlstm_m[l])
        xs, ss = trunk_layer(xs, ts, mod[bp:], p, st)
        new_p.append(sp)
        new_s.append(ss)
    y_prompt = final_norm(xp, final_g).reshape(bp, tp, d)
    y_sample = final_norm(xs, final_g).reshape(bs, ts, d)
    pk, pv, pc, pC, pn, pm = [jnp.stack(t) for t in zip(*new_p)]
    sk, sv, sc, sC, sn, sm = [jnp.stack(t) for t in zip(*new_s)]
    return (y_prompt, y_sample, pk, pv, pc, pC, pn, pm, sk, sv, sc, sC, sn, sm)
```

```python
import functools
import math

import jax
import jax.numpy as jnp
from jax import lax
from jax.experimental import pallas as pl
from jax.experimental.pallas import tpu as pltpu

D_MODEL = 1024
DEPTH = 2
PAST_LEN = 16384

MLSTM_W = D_MODEL // 2
MLSTM_HEADS = 4
MLSTM_DH = MLSTM_W // MLSTM_HEADS
MLSTM_CHUNK = 64
SWA_W = D_MODEL // 4
SWA_DH = 64
SWA_HEADS = SWA_W // SWA_DH
SWA_KV_HEADS = SWA_HEADS // 2
SWA_GROUP = SWA_HEADS // SWA_KV_HEADS
SWA_KVW = SWA_KV_HEADS * SWA_DH
WINDOW = 128
CONV_W = D_MODEL - MLSTM_W - SWA_W
CONV_WIDTH = 3
PEER_HEADS = 8
PEER_NKEYS = 128
PEER_DKEY = 256
HALF_KEY = PEER_DKEY // 2
PEER_TOPK = 16
PEER_SEL = PEER_HEADS * PEER_TOPK
EPS = 1e-6

LANES = 128
SUBLANES = 8
ROW_TILE = 512
PEER_TB = LANES
PEER_NSLOT = 8
MLSTM_SEQS = 2

Z_Q, Z_K, Z_V, Z_O = 0, MLSTM_W, 2 * MLSTM_W, 3 * MLSTM_W
Z_SQ = 4 * MLSTM_W
Z_SK = Z_SQ + SWA_W
Z_SV = Z_SK + SWA_KVW
Z_CB = Z_SV + SWA_KVW
Z_CC = Z_CB + CONV_W
Z_CH = Z_CC + CONV_W
Z_GATE = Z_CH + CONV_W
Z_W = Z_GATE + LANES
ALIBI = tuple(2.0 ** (-8.0 * h / SWA_HEADS) for h in range(1, SWA_HEADS + 1))


def _bdot(a, b, dims):
    return lax.dot_general(a.astype(jnp.bfloat16), b.astype(jnp.bfloat16), (dims, ((), ())),
                           preferred_element_type=jnp.float32)


def _row_spec(arr, n, tm, d):
    g = arr.shape[0]
    if g == n:
        return arr, pl.BlockSpec((tm, d), lambda i, *_: (i, 0))
    per = n // g
    return arr[:, None, :], pl.BlockSpec((None, 1, d), lambda i, *_: (i * tm // per, 0, 0))


def _norm_mod(x, gain, sc, sh):
    y = x * lax.rsqrt(jnp.mean(x * x, axis=-1, keepdims=True) + EPS) * gain
    return y * (1.0 + sc) + sh


def _adaln_kernel(c_ref, w_ref, b_ref, o_ref):
    c = c_ref[...]
    o_ref[...] = _bdot(c * jax.nn.sigmoid(c), w_ref[...], ((1,), (0,))) + b_ref[...]


def adaln_mod(c, w, b):
    bsz, d = c.shape
    m = w.shape[1]
    tn = 512
    return pl.pallas_call(
        _adaln_kernel,
        grid=(m // tn,),
        in_specs=[pl.BlockSpec((bsz, d), lambda j: (0, 0)),
                  pl.BlockSpec((d, tn), lambda j: (0, j)),
                  pl.BlockSpec((1, tn), lambda j: (0, j))],
        out_specs=pl.BlockSpec((bsz, tn), lambda j: (0, j)),
        out_shape=jax.ShapeDtypeStruct((bsz, m), jnp.float32),
        name="adaln_mod",
    )(c, w, b.reshape(1, m))


def _mix_in_kernel(x_ref, gain_ref, sc_ref, sh_ref, w_ref, z_ref):
    h = _norm_mod(x_ref[...], gain_ref[...], sc_ref[...], sh_ref[...])
    z_ref[...] = jnp.dot(h.astype(jnp.bfloat16), w_ref[...], preferred_element_type=jnp.float32)


def mix_in(x, gain, sc, sh, w_bf):
    n, d = x.shape
    zw = w_bf.shape[1]
    tm, tn = min(ROW_TILE, n), zw
    sc, sc_spec = _row_spec(sc, n, tm, d)
    sh, sh_spec = _row_spec(sh, n, tm, d)
    return pl.pallas_call(
        _mix_in_kernel,
        grid=(n // tm, zw // tn),
        in_specs=[pl.BlockSpec((tm, d), lambda i, j: (i, 0)),
                  pl.BlockSpec((1, d), lambda i, j: (0, 0)),
                  sc_spec, sh_spec,
                  pl.BlockSpec((d, tn), lambda i, j: (0, j))],
        out_specs=pl.BlockSpec((tm, tn), lambda i, j: (i, j)),
        out_shape=jax.ShapeDtypeStruct((n, zw), jnp.float32),
        name="mix_in",
    )(x, gain, sc, sh, w_bf)


def _log_sigmoid(x):
    return jnp.minimum(x, 0.0) - jnp.log(1.0 + jnp.exp(-jnp.abs(x)))


def _mlstm_kernel(chunk, valid, q_ref, k_ref, v_ref, o_ref, gate_ref, gb_ref, g_ref, c0_ref, n0_ref, m0_ref,
                  hm_ref, c_ref, n_ref, m_ref):
    nb, tc = q_ref.shape[:2]
    nh, dh, L = MLSTM_HEADS, MLSTM_DH, chunk

    @pl.when(pl.program_id(1) == 0)
    def _():
        c_ref[...] = c0_ref[...]
        n_ref[...] = n0_ref[...]
        m_ref[...] = m0_ref[...]

    row = lax.broadcasted_iota(jnp.int32, (L, L), 0)
    col = lax.broadcasted_iota(jnp.int32, (L, L), 1)
    eye = row == col
    visible = (col <= row) & (col < valid)
    rcol = lax.broadcasted_iota(jnp.int32, (L, 1), 0)
    gb = gb_ref[...]
    gain = g_ref[...]

    def to_row(x_col):
        return jnp.sum(jnp.where(eye, x_col, 0.0), axis=0, keepdims=True)

    def one_chunk(ci, carry):
        r0 = pl.multiple_of(ci * L, L)
        for sq, hd in [(a, b) for a in range(nb) for b in range(nh)]:
            pre = gate_ref[sq, pl.ds(r0, L), :] + gb
            sl = slice(hd * dh, (hd + 1) * dh)
            q = q_ref[sq, pl.ds(r0, L), sl]
            k = k_ref[sq, pl.ds(r0, L), sl] * (dh ** -0.5)
            v = v_ref[sq, pl.ds(r0, L), sl]
            ig = pre[:, hd:hd + 1]
            lf = jnp.where(rcol < valid, _log_sigmoid(pre[:, nh + hd:nh + hd + 1]), 0.0)
            b = lf
            s = 1
            while s < L:
                b = b + jnp.where(rcol >= s, pltpu.roll(b, s, 0), 0.0)
                s *= 2
            cmat, nrow, m_prev = c_ref[sq, hd], n_ref[sq, hd], m_ref[sq, hd][:, 0:1]
            a = b + m_prev
            d = jnp.where(visible, b - to_row(b) + to_row(ig), -jnp.inf)
            m_t = jnp.maximum(a, jnp.max(d, axis=-1, keepdims=True))
            w_inter = jnp.exp(a - m_t)
            w_intra = jnp.exp(d - m_t)
            qk = _bdot(q, k, ((1,), (1,))) * w_intra
            num = w_inter * _bdot(q, cmat, ((1,), (1,))) + _bdot(qk, v, ((1,), (0,)))
            den = w_inter * jnp.sum(q * nrow, axis=-1, keepdims=True) + jnp.sum(qk, axis=-1, keepdims=True)
            h = num / jnp.maximum(jnp.abs(den), jnp.exp(-m_t))
            hn = h * lax.rsqrt(jnp.mean(h * h, axis=-1, keepdims=True) + EPS) * gain[:, sl]
            hm_ref[sq, pl.ds(r0, L), sl] = hn * jax.nn.sigmoid(o_ref[sq, pl.ds(r0, L), sl])
            m_last = m_t[L - 1:L]
            wl_inter = w_inter[L - 1:L]
            wl = jnp.where(rcol < valid, jnp.exp(b[L - 1:L] - b + ig - m_last), 0.0)
            c_ref[sq, hd] = wl_inter * cmat + _bdot(wl * v, k, ((0,), (0,)))
            n_ref[sq, hd] = wl_inter * nrow + jnp.sum(wl * k, axis=0, keepdims=True)
            m_ref[sq, hd] = jnp.broadcast_to(m_last, (1, dh))
        return carry

    lax.fori_loop(0, tc // L, one_chunk, 0)


def mlstm_layer(z, gate_b, mh_g, c0, n0, m0, seq_len):
    bsz = z.shape[0] // seq_len
    nh, dh = MLSTM_HEADS, MLSTM_DH
    chunk = math.gcd(seq_len, MLSTM_CHUNK)
    valid, padded = chunk, seq_len
    if chunk % SUBLANES:
        assert seq_len < SUBLANES
        chunk = padded = SUBLANES
        z = jnp.pad(z.reshape(bsz, seq_len, -1), ((0, 0), (0, padded - seq_len), (0, 0)))
    z = z.reshape(bsz, padded, -1)
    nb = MLSTM_SEQS
    assert bsz % nb == 0
    tc = min(padded, ROW_TILE)
    w = nh * dh
    gb = jnp.zeros((1, LANES), jnp.float32).at[0, :2 * nh].set(gate_b)

    def zcol(off, width):
        return pl.BlockSpec((nb, tc, width), lambda b, c: (b, c, off // width))

    def state(shape):
        return pl.BlockSpec((nb,) + shape, lambda b, c: (b,) + (0,) * len(shape))

    hm, c1, n1, m1 = pl.pallas_call(
        functools.partial(_mlstm_kernel, chunk, valid),
        grid=(bsz // nb, padded // tc),
        in_specs=[zcol(Z_Q, w), zcol(Z_K, w), zcol(Z_V, w), zcol(Z_O, w), zcol(Z_GATE, LANES),
                  pl.BlockSpec((1, LANES), lambda b, c: (0, 0)),
                  pl.BlockSpec((1, w), lambda b, c: (0, 0)),
                  state((nh, dh, dh)), state((nh, 1, dh)), state((nh, 1, dh))],
        out_specs=[pl.BlockSpec((nb, tc, w), lambda b, c: (b, c, 0)),
                   state((nh, dh, dh)), state((nh, 1, dh)), state((nh, 1, dh))],
        out_shape=[jax.ShapeDtypeStruct((bsz, padded, w), jnp.float32),
                   jax.ShapeDtypeStruct((bsz, nh, dh, dh), jnp.float32),
                   jax.ShapeDtypeStruct((bsz, nh, 1, dh), jnp.float32),
                   jax.ShapeDtypeStruct((bsz, nh, 1, dh), jnp.float32)],
        compiler_params=pltpu.CompilerParams(dimension_semantics=("arbitrary", "arbitrary")),
        name="mlstm",
    )(z, z, z, z, z, gb, mh_g.reshape(1, w), c0, n0[:, :, None, :],
      jnp.broadcast_to(m0[:, :, None, None], (bsz, nh, 1, dh)))
    return hm[:, :seq_len].reshape(bsz * seq_len, w), c1, n1[:, :, 0, :], m1[:, :, 0, 0]


def _attend(q, kb, vb, dist, visible, slope, sink):
    s = _bdot(q, kb, ((1,), (1,))) * (SWA_DH ** -0.5) - slope * dist.astype(jnp.float32)
    s = jnp.where(visible, s, -jnp.inf)
    mx = jnp.maximum(jnp.max(s, axis=-1, keepdims=True), sink)
    e = jnp.exp(s - mx)
    p = e / (jnp.sum(e, axis=-1, keepdims=True) + jnp.exp(sink - mx))
    return _bdot(p, vb, ((1,), (0,)))


def _swa_banded_kernel(q_ref, kp_ref, kc_ref, vp_ref, vc_ref, sink_ref, o_ref):
    j = pl.program_id(1)
    w = q_ref.shape[0]
    t = lax.broadcasted_iota(jnp.int32, (w, 2 * w), 0)
    i = lax.broadcasted_iota(jnp.int32, (w, 2 * w), 1)
    dist = t + w - i
    visible = (dist >= 0) & (dist <= WINDOW) & ((j > 0) | (i >= w))
    sinks = sink_ref[...]
    for kv in range(SWA_KV_HEADS):
        ks = slice(kv * SWA_DH, (kv + 1) * SWA_DH)
        kb = jnp.concatenate([kp_ref[:, ks], kc_ref[:, ks]], axis=0)
        vb = jnp.concatenate([vp_ref[:, ks], vc_ref[:, ks]], axis=0)
        for g in range(SWA_GROUP):
            h = kv * SWA_GROUP + g
            hs = slice(h * SWA_DH, (h + 1) * SWA_DH)
            o_ref[:, hs] = _attend(q_ref[:, hs], kb, vb, dist, visible, ALIBI[h], sinks[:, h:h + 1])


def swa_banded(z, sinks, seq_len):
    n = z.shape[0]
    w = WINDOW
    nb = seq_len // w
    sink_row = jnp.zeros((1, LANES), jnp.float32).at[0, :SWA_HEADS].set(sinks)

    def cur(off, width):
        return pl.BlockSpec((w, width), lambda b, j: (b * nb + j, off // width))

    def prev(off, width):
        return pl.BlockSpec((w, width), lambda b, j: (b * nb + jnp.maximum(j - 1, 0), off // width))

    return pl.pallas_call(
        _swa_banded_kernel,
        grid=(n // seq_len, nb),
        in_specs=[cur(Z_SQ, SWA_W), prev(Z_SK, SWA_KVW), cur(Z_SK, SWA_KVW), prev(Z_SV, SWA_KVW),
                  cur(Z_SV, SWA_KVW), pl.BlockSpec((1, LANES), lambda b, j: (0, 0))],
        out_specs=pl.BlockSpec((w, SWA_W), lambda b, j: (b * nb + j, 0)),
        out_shape=jax.ShapeDtypeStruct((n, SWA_W), jnp.float32),
        name="swa_banded",
    )(z, z, z, z, z, sink_row)


def _swa_cached_kernel(t_new, n_keys, q_ref, k_ref, v_ref, sink_ref, o_ref):
    m = q_ref.shape[1]
    nkp = k_ref.shape[0]
    r = lax.broadcasted_iota(jnp.int32, (m, nkp), 0)
    i = lax.broadcasted_iota(jnp.int32, (m, nkp), 1)
    first_key_pos = PAST_LEN - (n_keys - t_new)
    dist = (n_keys - t_new) + r % t_new - i
    visible = (dist >= 0) & (dist <= WINDOW) & (i < n_keys) & (first_key_pos + i >= 0)
    rg = lax.broadcasted_iota(jnp.int32, (m, 1), 0) // t_new
    sinks = sink_ref[...]
    for kv in range(SWA_KV_HEADS):
        ks = slice(kv * SWA_DH, (kv + 1) * SWA_DH)
        h0 = kv * SWA_GROUP
        slope = jnp.where(rg == 0, ALIBI[h0], ALIBI[h0 + 1])
        sink = jnp.where(rg == 0, sinks[:, h0:h0 + 1], sinks[:, h0 + 1:h0 + 2])
        o_ref[kv] = _attend(q_ref[kv], k_ref[:, ks], v_ref[:, ks], dist, visible, slope, sink)


def swa_cached(q, k_all, v_all, sinks):
    assert SWA_GROUP == 2
    bsz, t_new, _ = q.shape
    n_keys = k_all.shape[1]
    nkp = -(-n_keys // SUBLANES) * SUBLANES
    pad = ((0, 0), (0, nkp - n_keys), (0, 0))
    k_all, v_all = jnp.pad(k_all, pad), jnp.pad(v_all, pad)
    m = SWA_GROUP * t_new
    qs = q.reshape(bsz, t_new, SWA_KV_HEADS, SWA_GROUP, SWA_DH).transpose(0, 2, 3, 1, 4)
    qs = qs.reshape(bsz, SWA_KV_HEADS, m, SWA_DH)
    sink_row = jnp.zeros((1, LANES), jnp.float32).at[0, :SWA_HEADS].set(sinks)
    o = pl.pallas_call(
        functools.partial(_swa_cached_kernel, t_new, n_keys),
        grid=(bsz,),
        in_specs=[pl.BlockSpec((None, SWA_KV_HEADS, m, SWA_DH), lambda b: (b, 0, 0, 0)),
                  pl.BlockSpec((None, nkp, SWA_KVW), lambda b: (b, 0, 0)),
                  pl.BlockSpec((None, nkp, SWA_KVW), lambda b: (b, 0, 0)),
                  pl.BlockSpec((1, LANES), lambda b: (0, 0))],
        out_specs=pl.BlockSpec((None, SWA_KV_HEADS, m, SWA_DH), lambda b: (b, 0, 0, 0)),
        out_shape=jax.ShapeDtypeStruct((bsz, SWA_KV_HEADS, m, SWA_DH), jnp.float32),
        name="swa_cached",
    )(qs, k_all, v_all, sink_row)
    o = o.reshape(bsz, SWA_KV_HEADS, SWA_GROUP, t_new, SWA_DH).transpose(0, 3, 1, 2, 4)
    return o.reshape(bsz, t_new, SWA_W)


def _mix_out_kernel(seq_len, x_ref, gate_ref, hm_ref, oa_ref, cb_ref, cc_ref, ch_ref, hcc_ref, hch_ref,
                    pa_ref, pb_ref, cw_ref, w_ref, o_ref, u_ref):
    i = pl.program_id(0)
    tm = x_ref.shape[0]
    u = cc_ref[...] * ch_ref[...]
    u_ref[...] = u
    hu = hcc_ref[...] * hch_ref[...]
    r = lax.broadcasted_iota(jnp.int32, u.shape, 0)
    p = (i * tm + r) % seq_len
    pa, pb = pa_ref[...], pb_ref[...]
    last, last2 = hu[SUBLANES - 1:SUBLANES], hu[SUBLANES - 2:SUBLANES - 1]
    u1 = jnp.where(r >= 1, pltpu.roll(u, 1, 0), last)
    u2 = jnp.where(r >= 2, pltpu.roll(u, 2, 0), jnp.where(r == 1, last, last2))
    s1 = jnp.where(p >= 1, u1, pa)
    s2 = jnp.where(p >= 2, u2, jnp.where(p == 1, pa, pb))
    cw = cw_ref[...]
    yc = cb_ref[...] * (cw[0:1] * s2 + cw[1:2] * s1 + cw[2:3] * u)
    cat = jnp.concatenate([hm_ref[...], oa_ref[...], yc], axis=-1)
    mix = jnp.dot(cat.astype(jnp.bfloat16), w_ref[...], preferred_element_type=jnp.float32)
    o_ref[...] = x_ref[...] + gate_ref[...] * mix


def mix_out(x, gate, hm, oa, z, conv_prev, conv_w, w_out_bf, seq_len):
    n, d = x.shape
    c = CONV_W
    tm = min(ROW_TILE, n)
    gate, gate_spec = _row_spec(gate, n, tm, d)
    if seq_len % tm:
        pa = jnp.repeat(conv_prev[:, 1], seq_len, axis=0)
        pb = jnp.repeat(conv_prev[:, 0], seq_len, axis=0)
    else:
        pa, pb = conv_prev[:, 1], conv_prev[:, 0]
    pa, pa_spec = _row_spec(pa, n, tm, c)
    pb, pb_spec = _row_spec(pb, n, tm, c)
    cw = jnp.zeros((SUBLANES, c), jnp.float32).at[:CONV_WIDTH].set(conv_w)

    def zcol(off, width):
        return pl.BlockSpec((tm, width), lambda i: (i, off // width))

    def zhalo(off):
        return pl.BlockSpec((SUBLANES, c), lambda i: (jnp.maximum(i * (tm // SUBLANES) - 1, 0), off // c))

    return pl.pallas_call(
        functools.partial(_mix_out_kernel, seq_len),
        grid=(n // tm,),
        in_specs=[pl.BlockSpec((tm, d), lambda i: (i, 0)), gate_spec,
                  pl.BlockSpec((tm, MLSTM_W), lambda i: (i, 0)),
                  pl.BlockSpec((tm, SWA_W), lambda i: (i, 0)),
                  zcol(Z_CB, c), zcol(Z_CC, c), zcol(Z_CH, c), zhalo(Z_CC), zhalo(Z_CH),
                  pa_spec, pb_spec,
                  pl.BlockSpec((SUBLANES, c), lambda i: (0, 0)),
                  pl.BlockSpec((d, d), lambda i: (0, 0))],
        out_specs=[pl.BlockSpec((tm, d), lambda i: (i, 0)), pl.BlockSpec((tm, c), lambda i: (i, 0))],
        out_shape=[jax.ShapeDtypeStruct((n, d), jnp.float32), jax.ShapeDtypeStruct((n, c), jnp.float32)],
        name="mix_out",
    )(x, gate, hm, oa, z, z, z, z, z, pa, pb, cw, w_out_bf)


def _pair_candidates(s1, i1, s2, i2):
    k, sub = PEER_TOPK, SUBLANES
    assert k == 2 * sub
    row = lax.broadcasted_iota(jnp.int32, (sub, s1.shape[1]), 0)
    sums, ids = [], []

    def emit(a, b0, nvalid):
        c = s1[a:a + 1] + s2[b0:b0 + sub]
        sums.append(c if nvalid >= sub else jnp.where(row < nvalid, c, -jnp.inf))
        ids.append(i1[a:a + 1] * PEER_NKEYS + i2[b0:b0 + sub])

    for a in range(sub):
        nb = k // (a + 1)
        for b0 in range(0, nb, sub):
            emit(a, b0, nb - b0)
    sums.append(s1[sub:] + s2[0:1])
    ids.append(i1[sub:] * PEER_NKEYS + i2[0:1])
    return jnp.concatenate(sums, axis=0), jnp.concatenate(ids, axis=0)


N_CAND = SUBLANES * (sum(-(-(PEER_TOPK // (a + 1)) // SUBLANES) for a in range(SUBLANES)) + 1)


def _extract_top(s, rows, payload=None):
    m = jnp.max(s, axis=0, keepdims=True)
    r = jnp.min(jnp.where(s == m, rows, jnp.int32(s.shape[0])), axis=0, keepdims=True)
    hit = rows == r
    ident = r if payload is None else jnp.max(jnp.where(hit, payload, -1), axis=0, keepdims=True)
    return m, ident, jnp.where(hit, -jnp.inf, s)


def _gelu_tanh(x):
    return 0.5 * x * (1.0 + jnp.tanh(math.sqrt(2.0 / math.pi) * (x + 0.044715 * (x * x * x))))


def _peer_kernel(nblk, xr_ref, gain_ref, sc_ref, sh_ref, wq_ref, sk_ref, xres_ref, gate_ref, tab_ref,
                 o_ref, q_scr, h_scr, gt_scr, idx_v, idx_s, s_scr, tv_scr, ti_scr, cv_scr, ci_scr, fv_scr, fi_scr,
                 buf, sem, idx_sem):
    s = pl.program_id(0)
    tb, d = xr_ref.shape
    nch = d // LANES
    assert tb == LANES
    slot_r = s % 2
    slot_e = 1 - slot_r
    per_token_gate = gate_ref.shape[0] == tb

    def row_copy(e, slot, r):
        src = tab_ref.at[pl.ds(pl.multiple_of(e * nch, nch), nch)]
        return pltpu.make_async_copy(src, buf.at[slot, :, r, :], sem.at[slot])

    def issue(t, slot):
        for r in range(PEER_SEL):
            row_copy(idx_s[r, t], slot, r).start(priority=r % 2)

    def wait(slot):
        pltpu.make_async_copy(tab_ref.at[pl.ds(0, PEER_SEL * nch)], buf.at[slot], sem.at[slot]).wait()

    lane = lax.broadcasted_iota(jnp.int32, (PEER_SEL, tb), 1)

    def ffn(t, slot):
        hrow = h_scr[slot_e, pl.ds(t, 1), :]
        part = None
        for c in range(nch):
            u = lax.bitcast_convert_type(buf[slot, c] & jnp.uint32(0xFFFF0000), jnp.float32)
            term = u * hrow[:, c * LANES:(c + 1) * LANES]
            part = term if part is None else part + term
        sdot = jnp.sum(part, axis=-1, keepdims=True)
        g = jnp.sum(jnp.where(lane == t, gt_scr[slot_e], 0.0), axis=-1, keepdims=True)
        w = g * _gelu_tanh(sdot)
        y = jnp.concatenate(
            [jnp.sum(lax.bitcast_convert_type(buf[slot, c] << 16, jnp.float32) * w, axis=0, keepdims=True)
             for c in range(nch)], axis=1)
        gate = gate_ref[pl.ds(t, 1), :] if per_token_gate else gate_ref[...]
        return xres_ref[pl.ds(t, 1), :] + gate * y

    ahead = PEER_NSLOT - 1

    key_rows = lax.broadcasted_iota(jnp.int32, (PEER_NKEYS, tb), 0)
    cand_rows = lax.broadcasted_iota(jnp.int32, (N_CAND, tb), 0)

    def sl_scores(head):
        qh = q_scr[head].astype(jnp.bfloat16)
        for p in range(2):
            s_scr[p] = lax.dot_general(sk_ref[p], qh[:, p * HALF_KEY:(p + 1) * HALF_KEY],
                                       (((1,), (1,)), ((), ())), preferred_element_type=jnp.float32)

    def sl_stage1(k0, cnt, head):
        for p in range(2):
            sc_ = s_scr[p]
            for k in range(k0, k0 + cnt):
                m, r, sc_ = _extract_top(sc_, key_rows)
                tv_scr[p, k:k + 1, :] = m
                ti_scr[p, k:k + 1, :] = r
            s_scr[p] = sc_

    def sl_cand(head):
        c, ci = _pair_candidates(tv_scr[0], ti_scr[0], tv_scr[1], ti_scr[1])
        cv_scr[...] = c
        ci_scr[...] = ci

    def sl_stage2(k0, cnt, head):
        c, ci = cv_scr[...], ci_scr[...]
        for k in range(k0, k0 + cnt):
            m, e, c = _extract_top(c, cand_rows, ci)
            fv_scr[k:k + 1, :] = m
            fi_scr[k:k + 1, :] = e
        cv_scr[...] = c

    def sl_out(head):
        top_s = fv_scr[...]
        e = jnp.exp(top_s - jnp.max(top_s, axis=0, keepdims=True))
        off = pl.multiple_of(head * PEER_TOPK, PEER_TOPK)
        idx_v[pl.ds(off, PEER_TOPK), :] = fi_scr[...]
        gt_scr[slot_r, pl.ds(off, PEER_TOPK), :] = e / jnp.sum(e, axis=0, keepdims=True)

    per_slice = PEER_TOPK // PEER_NSLOT
    slices = [sl_scores]
    slices += [functools.partial(sl_stage1, k0, per_slice) for k0 in range(0, PEER_TOPK, per_slice)]
    slices += [sl_cand]
    k0 = 0
    for cnt in (4, 3, 3, 3, 3):
        slices.append(functools.partial(sl_stage2, k0, cnt))
        k0 += cnt
    slices += [sl_out]
    assert len(slices) == 2 * PEER_NSLOT and k0 == PEER_TOPK

    def publish_routing():
        cp = pltpu.make_async_copy(idx_v, idx_s, idx_sem.at[0])
        cp.start()
        cp.wait()

    def group(t0, head, todo, last):
        for j in range(PEER_NSLOT):
            t = t0 + j
            wait(j)
            out = ffn(t, j)
            if not last or j == 0:
                issue(t + ahead, (j + ahead) % PEER_NSLOT)
            for i in todo[j]:
                slices[i](head)
            if last:
                @pl.when(s < nblk)
                def _():
                    if j == 0:
                        publish_routing()
                    else:
                        issue(j - 1, j - 1)
            o_ref[pl.ds(t, 1), :] = out

    @pl.when(s == 0)
    def _():
        assert tab_ref.shape[0] // nch >= PEER_SEL * tb
        idx_v[...] = (lax.broadcasted_iota(jnp.int32, idx_v.shape, 0) * tb
                      + lax.broadcasted_iota(jnp.int32, idx_v.shape, 1))
        h_scr[slot_e] = jnp.zeros(h_scr.shape[1:], h_scr.dtype)
        gt_scr[slot_e] = jnp.zeros(gt_scr.shape[1:], gt_scr.dtype)
        publish_routing()
        for t0 in range(ahead):
            issue(t0, t0)

    h = _norm_mod(xr_ref[...], gain_ref[...], sc_ref[...], sh_ref[...])
    h_scr[slot_r] = h
    q = jnp.dot(h.astype(jnp.bfloat16), wq_ref[...], preferred_element_type=jnp.float32)
    for head in range(PEER_HEADS):
        q_scr[head] = q[:, head * PEER_DKEY:(head + 1) * PEER_DKEY]

    n_pairs = tb // (2 * PEER_NSLOT)
    assert n_pairs == PEER_HEADS

    one_each = [[j] for j in range(2 * PEER_NSLOT)]
    two_each = [[2 * j, 2 * j + 1] for j in range(PEER_NSLOT)]

    def body(gi, carry):
        t0 = pl.multiple_of(gi * (2 * PEER_NSLOT), 2 * PEER_NSLOT)
        group(t0, gi, one_each[:PEER_NSLOT], False)
        group(t0 + PEER_NSLOT, gi, one_each[PEER_NSLOT:], False)
        return carry

    lax.fori_loop(0, n_pairs - 1, body, 0)
    group(tb - 2 * PEER_NSLOT, n_pairs - 1, two_each, False)
    group(tb - PEER_NSLOT, n_pairs - 1, [[]] * PEER_NSLOT, True)


def _pack_kernel(u_ref, v_ref, o_ref):
    def bf16_bits(x):
        return lax.bitcast_convert_type(x.astype(jnp.bfloat16).astype(jnp.float32), jnp.uint32)

    for c in range(o_ref.shape[1]):
        cols = slice(c * LANES, (c + 1) * LANES)
        o_ref[:, c, :] = bf16_bits(u_ref[:, cols]) | (bf16_bits(v_ref[:, cols]) >> 16)


def pack_expert_table(u, v):
    e, d = u.shape
    te = ROW_TILE
    nch = d // LANES
    out = pl.pallas_call(
        _pack_kernel,
        grid=(e // te,),
        in_specs=[pl.BlockSpec((te, d), lambda i: (i, 0)), pl.BlockSpec((te, d), lambda i: (i, 0))],
        out_specs=pl.BlockSpec((te, nch, LANES), lambda i: (i, 0, 0)),
        out_shape=jax.ShapeDtypeStruct((e, nch, LANES), jnp.uint32),
        name="pack_experts",
    )(u, v)
    return out.reshape(e * nch, LANES)


def peer_layer(x, gain, sc, sh, gate, wq_bf, sk_bf, tab):
    n, d = x.shape
    tb = PEER_TB
    nblk = n // tb
    nch = d // LANES

    def cur(s):
        return jnp.minimum(s, nblk - 1)

    def prev(s):
        return jnp.maximum(s - 1, 0)

    def rows(arr, blk):
        g = arr.shape[0]
        if g == n:
            return arr, pl.BlockSpec((tb, d), lambda s: (blk(s), 0))
        per = n // g
        return arr[:, None, :], pl.BlockSpec((None, 1, d), lambda s: (blk(s) * tb // per, 0, 0))

    sc, sc_spec = rows(sc, cur)
    sh, sh_spec = rows(sh, cur)
    gate, gate_spec = rows(gate, prev)
    return pl.pallas_call(
        functools.partial(_peer_kernel, nblk),
        grid=(nblk + 1,),
        in_specs=[pl.BlockSpec((tb, d), lambda s: (cur(s), 0)),
                  pl.BlockSpec((1, d), lambda s: (0, 0)),
                  sc_spec, sh_spec,
                  pl.BlockSpec(wq_bf.shape, lambda s: (0, 0)),
                  pl.BlockSpec(sk_bf.shape, lambda s: (0, 0, 0)),
                  pl.BlockSpec((tb, d), lambda s: (prev(s), 0)),
                  gate_spec,
                  pl.BlockSpec(memory_space=pl.ANY)],
        out_specs=pl.BlockSpec((tb, d), lambda s: (prev(s), 0)),
        out_shape=jax.ShapeDtypeStruct((n, d), jnp.float32),
        scratch_shapes=[pltpu.VMEM((PEER_HEADS, tb, PEER_DKEY), jnp.float32),
                        pltpu.VMEM((2, tb, d), jnp.float32),
                        pltpu.VMEM((2, PEER_SEL, tb), jnp.float32),
                        pltpu.VMEM((PEER_SEL, tb), jnp.int32),
                        pltpu.SMEM((PEER_SEL, tb), jnp.int32),
                        pltpu.VMEM((2, PEER_NKEYS, tb), jnp.float32),
                        pltpu.VMEM((2, PEER_TOPK, tb), jnp.float32),
                        pltpu.VMEM((2, PEER_TOPK, tb), jnp.int32),
                        pltpu.VMEM((N_CAND, tb), jnp.float32),
                        pltpu.VMEM((N_CAND, tb), jnp.int32),
                        pltpu.VMEM((PEER_TOPK, tb), jnp.float32),
                        pltpu.VMEM((PEER_TOPK, tb), jnp.int32),
                        pltpu.VMEM((PEER_NSLOT, nch, PEER_SEL, LANES), jnp.uint32),
                        pltpu.SemaphoreType.DMA((PEER_NSLOT,)),
                        pltpu.SemaphoreType.DMA((1,))],
        compiler_params=pltpu.CompilerParams(dimension_semantics=("arbitrary",)),
        name="peer",
    )(x, gain, sc, sh, wq_bf, sk_bf, x, gate, tab)


def _final_norm_kernel(x_ref, g_ref, o_ref):
    x = x_ref[...]
    o_ref[...] = x * lax.rsqrt(jnp.mean(x * x, axis=-1, keepdims=True) + EPS) * g_ref[...]


def final_norm(x, g):
    n, d = x.shape
    tm = min(ROW_TILE, n)
    return pl.pallas_call(
        _final_norm_kernel,
        grid=(n // tm,),
        in_specs=[pl.BlockSpec((tm, d), lambda i: (i, 0)), pl.BlockSpec((1, d), lambda i: (0, 0))],
        out_specs=pl.BlockSpec((tm, d), lambda i: (i, 0)),
        out_shape=jax.ShapeDtypeStruct((n, d), x.dtype),
        name="final_norm",
    )(x, g.reshape(1, d))


def _per_tile_rows(m, seq_len, tile):
    return m if seq_len % tile == 0 else jnp.repeat(m, seq_len, axis=0)


def trunk_layer(x, seq_len, mod, p, state):
    n, d = x.shape
    bsz = n // seq_len
    sh1, sc1, gt1, sh2, sc2, gt2 = (_per_tile_rows(m, seq_len, ROW_TILE) for m in jnp.split(mod, 6, axis=-1))
    z = mix_in(x, p['g_mix'], sc1, sh1, p['w_in'])
    k_new = z[:, Z_SK:Z_SK + SWA_KVW].reshape(bsz, seq_len, SWA_KVW)
    v_new = z[:, Z_SV:Z_SV + SWA_KVW].reshape(bsz, seq_len, SWA_KVW)
    if state is None:
        win_rows = min(WINDOW, PAST_LEN)
        c0 = jnp.zeros((bsz, MLSTM_HEADS, MLSTM_DH, MLSTM_DH), jnp.float32)
        n0 = jnp.zeros((bsz, MLSTM_HEADS, MLSTM_DH), jnp.float32)
        m0 = jnp.zeros((bsz, MLSTM_HEADS), jnp.float32)
        cbuf = jnp.zeros((bsz, CONV_WIDTH - 1, CONV_W), jnp.float32)
        oa = swa_banded(z, p['sinks'], seq_len)
        k_all, v_all = k_new, v_new
    else:
        kbuf, vbuf, cbuf, c0, n0, m0 = state
        win_rows = kbuf.shape[1]
        k_all = jnp.concatenate([kbuf.reshape(bsz, win_rows, SWA_KVW), k_new], axis=1)
        v_all = jnp.concatenate([vbuf.reshape(bsz, win_rows, SWA_KVW), v_new], axis=1)
        q = z[:, Z_SQ:Z_SQ + SWA_W].reshape(bsz, seq_len, SWA_W)
        oa = swa_cached(q, k_all, v_all, p['sinks']).reshape(n, SWA_W)
    hm, c1, n1, m1 = mlstm_layer(z, p['gate_b'], p['mh_g'], c0, n0, m0, seq_len)
    x, u = mix_out(x, gt1, hm, oa, z, cbuf, p['conv_w'], p['w_out'], seq_len)
    k_keep = k_all[:, -win_rows:].reshape(bsz, win_rows, SWA_KV_HEADS, SWA_DH)
    v_keep = v_all[:, -win_rows:].reshape(bsz, win_rows, SWA_KV_HEADS, SWA_DH)
    cbuf_new = jnp.concatenate([cbuf, u.reshape(bsz, seq_len, CONV_W)], axis=1)[:, -(CONV_WIDTH - 1):]
    x = peer_layer(x, p['g_ffn'], sc2, sh2, gt2, p['wq'], p['subkeys'], p['tab'])
    return x, (k_keep, v_keep, cbuf_new, c1, n1, m1)


def kernel(x_prompt, x_sample, cache_swa_k, cache_swa_v, state_conv, state_mlstm_C, state_mlstm_n, state_mlstm_m, c_prompt, c_sample, ada_w, ada_b, norm_mix_g, norm_ffn_g, w_in, w_out, mlstm_gate_b, mlstm_norm_g, swa_sinks, conv_w, peer_wq, peer_subkeys, peer_u, peer_v, final_g):
    bp, tp, d = x_prompt.shape
    bs, ts, _ = x_sample.shape
    xp, xs = x_prompt.reshape(bp * tp, d), x_sample.reshape(bs * ts, d)
    c_all = jnp.concatenate([c_prompt, c_sample], axis=0)
    bf = jnp.bfloat16
    n_gate = 2 * MLSTM_HEADS
    new_p, new_s = [], []
    for l in range(DEPTH):
        wl = w_in[l]
        w_perm = jnp.concatenate([wl[:, :Z_SQ], wl[:, Z_SQ + n_gate:], wl[:, Z_SQ:Z_SQ + n_gate],
                                  jnp.zeros((d, LANES - n_gate), wl.dtype)], axis=1)
        p = dict(g_mix=norm_mix_g[l].reshape(1, d), g_ffn=norm_ffn_g[l].reshape(1, d),
                 w_in=w_perm.astype(bf), w_out=w_out[l].astype(bf), gate_b=mlstm_gate_b[l],
                 mh_g=mlstm_norm_g[l], sinks=swa_sinks[l], conv_w=conv_w[l], wq=peer_wq[l].astype(bf),
                 subkeys=peer_subkeys[l].astype(bf), tab=pack_expert_table(peer_u[l], peer_v[l]))
        mod = adaln_mod(c_all, ada_w[l], ada_b[l])
        xp, sp = trunk_layer(xp, tp, mod[:bp], p, None)
        st = (cache_swa_k[l], cache_swa_v[l], state_conv[l], state_mlstm_C[l], state_mlstm_n[l], state_mlstm_m[l])
        xs, ss = trunk_layer(xs, ts, mod[bp:], p, st)
        new_p.append(sp)
        new_s.append(ss)
    y_prompt = final_norm(xp, final_g).reshape(bp, tp, d)
    y_sample = final_norm(xs, final_g).reshape(bs, ts, d)
    pk, pv, pc, pC, pn, pm = [jnp.stack(t) for t in zip(*new_p)]
    sk, sv, sc, sC, sn, sm = [jnp.stack(t) for t in zip(*new_s)]
    return (y_prompt, y_sample, pk, pv, pc, pC, pn, pm, sk, sv, sc, sC, sn, sm)
```

```python
import functools
import math

import jax
import jax.numpy as jnp
from jax import lax
from jax.experimental import pallas as pl
from jax.experimental.pallas import tpu as pltpu

D_MODEL = 1024
DEPTH = 2
PAST_LEN = 16384

MLSTM_W = D_MODEL // 2
MLSTM_HEADS = 4
MLSTM_DH = MLSTM_W // MLSTM_HEADS
MLSTM_CHUNK = 64
SWA_W = D_MODEL // 4
SWA_DH = 64
SWA_HEADS = SWA_W // SWA_DH
SWA_KV_HEADS = SWA_HEADS // 2
SWA_GROUP = SWA_HEADS // SWA_KV_HEADS
SWA_KVW = SWA_KV_HEADS * SWA_DH
WINDOW = 128
CONV_W = D_MODEL - MLSTM_W - SWA_W
CONV_WIDTH = 3
PEER_HEADS = 8
PEER_NKEYS = 128
PEER_DKEY = 256
HALF_KEY = PEER_DKEY // 2
PEER_TOPK = 16
PEER_SEL = PEER_HEADS * PEER_TOPK
EPS = 1e-6

LANES = 128
SUBLANES = 8
ROW_TILE = 512
PEER_TB = LANES
PEER_NSLOT = 8
MLSTM_SEQS = 2

Z_Q, Z_K, Z_V, Z_O = 0, MLSTM_W, 2 * MLSTM_W, 3 * MLSTM_W
Z_SQ = 4 * MLSTM_W
Z_SK = Z_SQ + SWA_W
Z_SV = Z_SK + SWA_KVW
Z_CB = Z_SV + SWA_KVW
Z_CC = Z_CB + CONV_W
Z_CH = Z_CC + CONV_W
Z_GATE = Z_CH + CONV_W
Z_W = Z_GATE + LANES
ALIBI = tuple(2.0 ** (-8.0 * h / SWA_HEADS) for h in range(1, SWA_HEADS + 1))


def _bdot(a, b, dims):
    return lax.dot_general(a.astype(jnp.bfloat16), b.astype(jnp.bfloat16), (dims, ((), ())),
                           preferred_element_type=jnp.float32)


def _row_spec(arr, n, tm, d):
    g = arr.shape[0]
    if g == n:
        return arr, pl.BlockSpec((tm, d), lambda i, *_: (i, 0))
    per = n // g
    return arr[:, None, :], pl.BlockSpec((None, 1, d), lambda i, *_: (i * tm // per, 0, 0))


def _norm_mod(x, gain, sc, sh):
    y = x * lax.rsqrt(jnp.mean(x * x, axis=-1, keepdims=True) + EPS) * gain
    return y * (1.0 + sc) + sh


def _adaln_kernel(c_ref, w_ref, b_ref, o_ref):
    c = c_ref[...]
    o_ref[...] = _bdot(c * jax.nn.sigmoid(c), w_ref[...], ((1,), (0,))) + b_ref[...]


def adaln_mod(c, w, b):
    bsz, d = c.shape
    m = w.shape[1]
    tn = 512
    return pl.pallas_call(
        _adaln_kernel,
        grid=(m // tn,),
        in_specs=[pl.BlockSpec((bsz, d), lambda j: (0, 0)),
                  pl.BlockSpec((d, tn), lambda j: (0, j)),
                  pl.BlockSpec((1, tn), lambda j: (0, j))],
        out_specs=pl.BlockSpec((bsz, tn), lambda j: (0, j)),
        out_shape=jax.ShapeDtypeStruct((bsz, m), jnp.float32),
        name="adaln_mod",
    )(c, w, b.reshape(1, m))


def _mix_in_kernel(x_ref, gain_ref, sc_ref, sh_ref, w_ref, z_ref):
    h = _norm_mod(x_ref[...], gain_ref[...], sc_ref[...], sh_ref[...])
    z_ref[...] = jnp.dot(h.astype(jnp.bfloat16), w_ref[...], preferred_element_type=jnp.float32)


def mix_in(x, gain, sc, sh, w_bf):
    n, d = x.shape
    zw = w_bf.shape[1]
    tm, tn = min(ROW_TILE, n), zw
    sc, sc_spec = _row_spec(sc, n, tm, d)
    sh, sh_spec = _row_spec(sh, n, tm, d)
    return pl.pallas_call(
        _mix_in_kernel,
        grid=(n // tm, zw // tn),
        in_specs=[pl.BlockSpec((tm, d), lambda i, j: (i, 0)),
                  pl.BlockSpec((1, d), lambda i, j: (0, 0)),
                  sc_spec, sh_spec,
                  pl.BlockSpec((d, tn), lambda i, j: (0, j))],
        out_specs=pl.BlockSpec((tm, tn), lambda i, j: (i, j)),
        out_shape=jax.ShapeDtypeStruct((n, zw), jnp.float32),
        name="mix_in",
    )(x, gain, sc, sh, w_bf)


def _log_sigmoid(x):
    return jnp.minimum(x, 0.0) - jnp.log(1.0 + jnp.exp(-jnp.abs(x)))


def _mlstm_kernel(chunk, valid, q_ref, k_ref, v_ref, o_ref, gate_ref, gb_ref, g_ref, c0_ref, n0_ref, m0_ref,
                  hm_ref, c_ref, n_ref, m_ref):
    nb, tc = q_ref.shape[:2]
    nh, dh, L = MLSTM_HEADS, MLSTM_DH, chunk

    @pl.when(pl.program_id(1) == 0)
    def _():
        c_ref[...] = c0_ref[...]
        n_ref[...] = n0_ref[...]
        m_ref[...] = m0_ref[...]

    row = lax.broadcasted_iota(jnp.int32, (L, L), 0)
    col = lax.broadcasted_iota(jnp.int32, (L, L), 1)
    eye = row == col
    visible = (col <= row) & (col < valid)
    rcol = lax.broadcasted_iota(jnp.int32, (L, 1), 0)
    gb = gb_ref[...]
    gain = g_ref[...]

    def to_row(x_col):
        return jnp.sum(jnp.where(eye, x_col, 0.0), axis=0, keepdims=True)

    def one_chunk(ci, carry):
        r0 = pl.multiple_of(ci * L, L)
        for sq, hd in [(a, b) for a in range(nb) for b in range(nh)]:
            pre = gate_ref[sq, pl.ds(r0, L), :] + gb
            sl = slice(hd * dh, (hd + 1) * dh)
            q = q_ref[sq, pl.ds(r0, L), sl]
            k = k_ref[sq, pl.ds(r0, L), sl] * (dh ** -0.5)
            v = v_ref[sq, pl.ds(r0, L), sl]
            ig = pre[:, hd:hd + 1]
            lf = jnp.where(rcol < valid, _log_sigmoid(pre[:, nh + hd:nh + hd + 1]), 0.0)
            b = lf
            s = 1
            while s < L:
                b = b + jnp.where(rcol >= s, pltpu.roll(b, s, 0), 0.0)
                s *= 2
            cmat, nrow, m_prev = c_ref[sq, hd], n_ref[sq, hd], m_ref[sq, hd][:, 0:1]
            a = b + m_prev
            d = jnp.where(visible, b - to_row(b) + to_row(ig), -jnp.inf)
            m_t = jnp.maximum(a, jnp.max(d, axis=-1, keepdims=True))
            w_inter = jnp.exp(a - m_t)
            w_intra = jnp.exp(d - m_t)
            qk = _bdot(q, k, ((1,), (1,))) * w_intra
            num = w_inter * _bdot(q, cmat, ((1,), (1,))) + _bdot(qk, v, ((1,), (0,)))
            den = w_inter * jnp.sum(q * nrow, axis=-1, keepdims=True) + jnp.sum(qk, axis=-1, keepdims=True)
            h = num / jnp.maximum(jnp.abs(den), jnp.exp(-m_t))
            hn = h * lax.rsqrt(jnp.mean(h * h, axis=-1, keepdims=True) + EPS) * gain[:, sl]
            hm_ref[sq, pl.ds(r0, L), sl] = hn * jax.nn.sigmoid(o_ref[sq, pl.ds(r0, L), sl])
            m_last = m_t[L - 1:L]
            wl_inter = w_inter[L - 1:L]
            wl = jnp.where(rcol < valid, jnp.exp(b[L - 1:L] - b + ig - m_last), 0.0)
            c_ref[sq, hd] = wl_inter * cmat + _bdot(wl * v, k, ((0,), (0,)))
            n_ref[sq, hd] = wl_inter * nrow + jnp.sum(wl * k, axis=0, keepdims=True)
            m_ref[sq, hd] = jnp.broadcast_to(m_last, (1, dh))
        return carry

    lax.fori_loop(0, tc // L, one_chunk, 0)


def mlstm_layer(z, gate_b, mh_g, c0, n0, m0, seq_len):
    bsz = z.shape[0] // seq_len
    nh, dh = MLSTM_HEADS, MLSTM_DH
    chunk = math.gcd(seq_len, MLSTM_CHUNK)
    valid, padded = chunk, seq_len
    if chunk % SUBLANES:
        assert seq_len < SUBLANES
        chunk = padded = SUBLANES
        z = jnp.pad(z.reshape(bsz, seq_len, -1), ((0, 0), (0, padded - seq_len), (0, 0)))
    z = z.reshape(bsz, padded, -1)
    nb = MLSTM_SEQS
    assert bsz % nb == 0
    tc = min(padded, ROW_TILE)
    w = nh * dh
    gb = jnp.zeros((1, LANES), jnp.float32).at[0, :2 * nh].set(gate_b)

    def zcol(off, width):
        return pl.BlockSpec((nb, tc, width), lambda b, c: (b, c, off // width))

    def state(shape):
        return pl.BlockSpec((nb,) + shape, lambda b, c: (b,) + (0,) * len(shape))

    hm, c1, n1, m1 = pl.pallas_call(
        functools.partial(_mlstm_kernel, chunk, valid),
        grid=(bsz // nb, padded // tc),
        in_specs=[zcol(Z_Q, w), zcol(Z_K, w), zcol(Z_V, w), zcol(Z_O, w), zcol(Z_GATE, LANES),
                  pl.BlockSpec((1, LANES), lambda b, c: (0, 0)),
                  pl.BlockSpec((1, w), lambda b, c: (0, 0)),
                  state((nh, dh, dh)), state((nh, 1, dh)), state((nh, 1, dh))],
        out_specs=[pl.BlockSpec((nb, tc, w), lambda b, c: (b, c, 0)),
                   state((nh, dh, dh)), state((nh, 1, dh)), state((nh, 1, dh))],
        out_shape=[jax.ShapeDtypeStruct((bsz, padded, w), jnp.float32),
                   jax.ShapeDtypeStruct((bsz, nh, dh, dh), jnp.float32),
                   jax.ShapeDtypeStruct((bsz, nh, 1, dh), jnp.float32),
                   jax.ShapeDtypeStruct((bsz, nh, 1, dh), jnp.float32)],
        compiler_params=pltpu.CompilerParams(dimension_semantics=("arbitrary", "arbitrary")),
        name="mlstm",
    )(z, z, z, z, z, gb, mh_g.reshape(1, w), c0, n0[:, :, None, :],
      jnp.broadcast_to(m0[:, :, None, None], (bsz, nh, 1, dh)))
    return hm[:, :seq_len].reshape(bsz * seq_len, w), c1, n1[:, :, 0, :], m1[:, :, 0, 0]


def _attend(q, kb, vb, dist, visible, slope, sink):
    s = _bdot(q, kb, ((1,), (1,))) * (SWA_DH ** -0.5) - slope * dist.astype(jnp.float32)
    s = jnp.where(visible, s, -jnp.inf)
    mx = jnp.maximum(jnp.max(s, axis=-1, keepdims=True), sink)
    e = jnp.exp(s - mx)
    p = e / (jnp.sum(e, axis=-1, keepdims=True) + jnp.exp(sink - mx))
    return _bdot(p, vb, ((1,), (0,)))


def _swa_banded_kernel(q_ref, kp_ref, kc_ref, vp_ref, vc_ref, sink_ref, o_ref):
    j = pl.program_id(1)
    w = q_ref.shape[0]
    t = lax.broadcasted_iota(jnp.int32, (w, 2 * w), 0)
    i = lax.broadcasted_iota(jnp.int32, (w, 2 * w), 1)
    dist = t + w - i
    visible = (dist >= 0) & (dist <= WINDOW) & ((j > 0) | (i >= w))
    sinks = sink_ref[...]
    for kv in range(SWA_KV_HEADS):
        ks = slice(kv * SWA_DH, (kv + 1) * SWA_DH)
        kb = jnp.concatenate([kp_ref[:, ks], kc_ref[:, ks]], axis=0)
        vb = jnp.concatenate([vp_ref[:, ks], vc_ref[:, ks]], axis=0)
        for g in range(SWA_GROUP):
            h = kv * SWA_GROUP + g
            hs = slice(h * SWA_DH, (h + 1) * SWA_DH)
            o_ref[:, hs] = _attend(q_ref[:, hs], kb, vb, dist, visible, ALIBI[h], sinks[:, h:h + 1])


def swa_banded(z, sinks, seq_len):
    n = z.shape[0]
    w = WINDOW
    nb = seq_len // w
    sink_row = jnp.zeros((1, LANES), jnp.float32).at[0, :SWA_HEADS].set(sinks)

    def cur(off, width):
        return pl.BlockSpec((w, width), lambda b, j: (b * nb + j, off // width))

    def prev(off, width):
        return pl.BlockSpec((w, width), lambda b, j: (b * nb + jnp.maximum(j - 1, 0), off // width))

    return pl.pallas_call(
        _swa_banded_kernel,
        grid=(n // seq_len, nb),
        in_specs=[cur(Z_SQ, SWA_W), prev(Z_SK, SWA_KVW), cur(Z_SK, SWA_KVW), prev(Z_SV, SWA_KVW),
                  cur(Z_SV, SWA_KVW), pl.BlockSpec((1, LANES), lambda b, j: (0, 0))],
        out_specs=pl.BlockSpec((w, SWA_W), lambda b, j: (b * nb + j, 0)),
        out_shape=jax.ShapeDtypeStruct((n, SWA_W), jnp.float32),
        name="swa_banded",
    )(z, z, z, z, z, sink_row)


def _swa_cached_kernel(t_new, n_keys, q_ref, k_ref, v_ref, sink_ref, o_ref):
    m = q_ref.shape[1]
    nkp = k_ref.shape[0]
    r = lax.broadcasted_iota(jnp.int32, (m, nkp), 0)
    i = lax.broadcasted_iota(jnp.int32, (m, nkp), 1)
    first_key_pos = PAST_LEN - (n_keys - t_new)
    dist = (n_keys - t_new) + r % t_new - i
    visible = (dist >= 0) & (dist <= WINDOW) & (i < n_keys) & (first_key_pos + i >= 0)
    rg = lax.broadcasted_iota(jnp.int32, (m, 1), 0) // t_new
    sinks = sink_ref[...]
    for kv in range(SWA_KV_HEADS):
        ks = slice(kv * SWA_DH, (kv + 1) * SWA_DH)
        h0 = kv * SWA_GROUP
        slope = jnp.where(rg == 0, ALIBI[h0], ALIBI[h0 + 1])
        sink = jnp.where(rg == 0, sinks[:, h0:h0 + 1], sinks[:, h0 + 1:h0 + 2])
        o_ref[kv] = _attend(q_ref[kv], k_ref[:, ks], v_ref[:, ks], dist, visible, slope, sink)


def swa_cached(q, k_all, v_all, sinks):
    assert SWA_GROUP == 2
    bsz, t_new, _ = q.shape
    n_keys = k_all.shape[1]
    nkp = -(-n_keys // SUBLANES) * SUBLANES
    pad = ((0, 0), (0, nkp - n_keys), (0, 0))
    k_all, v_all = jnp.pad(k_all, pad), jnp.pad(v_all, pad)
    m = SWA_GROUP * t_new
    qs = q.reshape(bsz, t_new, SWA_KV_HEADS, SWA_GROUP, SWA_DH).transpose(0, 2, 3, 1, 4)
    qs = qs.reshape(bsz, SWA_KV_HEADS, m, SWA_DH)
    sink_row = jnp.zeros((1, LANES), jnp.float32).at[0, :SWA_HEADS].set(sinks)
    o = pl.pallas_call(
        functools.partial(_swa_cached_kernel, t_new, n_keys),
        grid=(bsz,),
        in_specs=[pl.BlockSpec((None, SWA_KV_HEADS, m, SWA_DH), lambda b: (b, 0, 0, 0)),
                  pl.BlockSpec((None, nkp, SWA_KVW), lambda b: (b, 0, 0)),
                  pl.BlockSpec((None, nkp, SWA_KVW), lambda b: (b, 0, 0)),
                  pl.BlockSpec((1, LANES), lambda b: (0, 0))],
        out_specs=pl.BlockSpec((None, SWA_KV_HEADS, m, SWA_DH), lambda b: (b, 0, 0, 0)),
        out_shape=jax.ShapeDtypeStruct((bsz, SWA_KV_HEADS, m, SWA_DH), jnp.float32),
        name="swa_cached",
    )(qs, k_all, v_all, sink_row)
    o = o.reshape(bsz, SWA_KV_HEADS, SWA_GROUP, t_new, SWA_DH).transpose(0, 3, 1, 2, 4)
    return o.reshape(bsz, t_new, SWA_W)


def _mix_out_kernel(seq_len, x_ref, gate_ref, hm_ref, oa_ref, cb_ref, cc_ref, ch_ref, hcc_ref, hch_ref,
                    pa_ref, pb_ref, cw_ref, w_ref, o_ref, u_ref):
    i = pl.program_id(0)
    tm = x_ref.shape[0]
    u = cc_ref[...] * ch_ref[...]
    u_ref[...] = u
    hu = hcc_ref[...] * hch_ref[...]
    r = lax.broadcasted_iota(jnp.int32, u.shape, 0)
    p = (i * tm + r) % seq_len
    pa, pb = pa_ref[...], pb_ref[...]
    last, last2 = hu[SUBLANES - 1:SUBLANES], hu[SUBLANES - 2:SUBLANES - 1]
    u1 = jnp.where(r >= 1, pltpu.roll(u, 1, 0), last)
    u2 = jnp.where(r >= 2, pltpu.roll(u, 2, 0), jnp.where(r == 1, last, last2))
    s1 = jnp.where(p >= 1, u1, pa)
    s2 = jnp.where(p >= 2, u2, jnp.where(p == 1, pa, pb))
    cw = cw_ref[...]
    yc = cb_ref[...] * (cw[0:1] * s2 + cw[1:2] * s1 + cw[2:3] * u)
    cat = jnp.concatenate([hm_ref[...], oa_ref[...], yc], axis=-1)
    mix = jnp.dot(cat.astype(jnp.bfloat16), w_ref[...], preferred_element_type=jnp.float32)
    o_ref[...] = x_ref[...] + gate_ref[...] * mix


def mix_out(x, gate, hm, oa, z, conv_prev, conv_w, w_out_bf, seq_len):
    n, d = x.shape
    c = CONV_W
    tm = min(ROW_TILE, n)
    gate, gate_spec = _row_spec(gate, n, tm, d)
    if seq_len % tm:
        pa = jnp.repeat(conv_prev[:, 1], seq_len, axis=0)
        pb = jnp.repeat(conv_prev[:, 0], seq_len, axis=0)
    else:
        pa, pb = conv_prev[:, 1], conv_prev[:, 0]
    pa, pa_spec = _row_spec(pa, n, tm, c)
    pb, pb_spec = _row_spec(pb, n, tm, c)
    cw = jnp.zeros((SUBLANES, c), jnp.float32).at[:CONV_WIDTH].set(conv_w)

    def zcol(off, width):
        return pl.BlockSpec((tm, width), lambda i: (i, off // width))

    def zhalo(off):
        return pl.BlockSpec((SUBLANES, c), lambda i: (jnp.maximum(i * (tm // SUBLANES) - 1, 0), off // c))

    return pl.pallas_call(
        functools.partial(_mix_out_kernel, seq_len),
        grid=(n // tm,),
        in_specs=[pl.BlockSpec((tm, d), lambda i: (i, 0)), gate_spec,
                  pl.BlockSpec((tm, MLSTM_W), lambda i: (i, 0)),
                  pl.BlockSpec((tm, SWA_W), lambda i: (i, 0)),
                  zcol(Z_CB, c), zcol(Z_CC, c), zcol(Z_CH, c), zhalo(Z_CC), zhalo(Z_CH),
                  pa_spec, pb_spec,
                  pl.BlockSpec((SUBLANES, c), lambda i: (0, 0)),
                  pl.BlockSpec((d, d), lambda i: (0, 0))],
        out_specs=[pl.BlockSpec((tm, d), lambda i: (i, 0)), pl.BlockSpec((tm, c), lambda i: (i, 0))],
        out_shape=[jax.ShapeDtypeStruct((n, d), jnp.float32), jax.ShapeDtypeStruct((n, c), jnp.float32)],
        name="mix_out",
    )(x, gate, hm, oa, z, z, z, z, z, pa, pb, cw, w_out_bf)


def _pair_candidates(s1, i1, s2, i2):
    k, sub = PEER_TOPK, SUBLANES
    assert k == 2 * sub
    row = lax.broadcasted_iota(jnp.int32, (sub, s1.shape[1]), 0)
    sums, ids = [], []

    def emit(a, b0, nvalid):
        c = s1[a:a + 1] + s2[b0:b0 + sub]
        sums.append(c if nvalid >= sub else jnp.where(row < nvalid, c, -jnp.inf))
        ids.append(i1[a:a + 1] * PEER_NKEYS + i2[b0:b0 + sub])

    for a in range(sub):
        nb = k // (a + 1)
        for b0 in range(0, nb, sub):
            emit(a, b0, nb - b0)
    sums.append(s1[sub:] + s2[0:1])
    ids.append(i1[sub:] * PEER_NKEYS + i2[0:1])
    return jnp.concatenate(sums, axis=0), jnp.concatenate(ids, axis=0)


N_CAND = SUBLANES * (sum(-(-(PEER_TOPK // (a + 1)) // SUBLANES) for a in range(SUBLANES)) + 1)


def _extract_top(s, rows, payload=None):
    m = jnp.max(s, axis=0, keepdims=True)
    r = jnp.min(jnp.where(s == m, rows, jnp.int32(s.shape[0])), axis=0, keepdims=True)
    hit = rows == r
    ident = r if payload is None else jnp.max(jnp.where(hit, payload, -1), axis=0, keepdims=True)
    return m, ident, jnp.where(hit, -jnp.inf, s)


def _gelu_tanh(x):
    return 0.5 * x * (1.0 + jnp.tanh(math.sqrt(2.0 / math.pi) * (x + 0.044715 * (x * x * x))))


def _peer_kernel(nblk, xr_ref, gain_ref, sc_ref, sh_ref, wq_ref, sk_ref, xres_ref, gate_ref, tab_ref,
                 o_ref, q_scr, h_scr, gt_scr, idx_v, idx_s, s_scr, tv_scr, ti_scr, cv_scr, ci_scr, fv_scr, fi_scr,
                 buf, sem, idx_sem):
    s = pl.program_id(0)
    tb, d = xr_ref.shape
    nch = d // LANES
    assert tb == LANES
    slot_r = s % 2
    slot_e = 1 - slot_r
    per_token_gate = gate_ref.shape[0] == tb

    def row_copy(e, slot, r):
        src = tab_ref.at[pl.ds(pl.multiple_of(e * nch, nch), nch)]
        return pltpu.make_async_copy(src, buf.at[slot, :, r, :], sem.at[slot])

    def issue(t, slot):
        for r in range(PEER_SEL):
            row_copy(idx_s[r, t], slot, r).start(priority=r % 2)

    def wait(slot):
        pltpu.make_async_copy(tab_ref.at[pl.ds(0, PEER_SEL * nch)], buf.at[slot], sem.at[slot]).wait()

    lane = lax.broadcasted_iota(jnp.int32, (PEER_SEL, tb), 1)

    def ffn(t, slot):
        hrow = h_scr[slot_e, pl.ds(t, 1), :]
        part = None
        for c in range(nch):
            u = lax.bitcast_convert_type(buf[slot, c] & jnp.uint32(0xFFFF0000), jnp.float32)
            term = u * hrow[:, c * LANES:(c + 1) * LANES]
            part = term if part is None else part + term
        sdot = jnp.sum(part, axis=-1, keepdims=True)
        g = jnp.sum(jnp.where(lane == t, gt_scr[slot_e], 0.0), axis=-1, keepdims=True)
        w = g * _gelu_tanh(sdot)
        y = jnp.concatenate(
            [jnp.sum(lax.bitcast_convert_type(buf[slot, c] << 16, jnp.float32) * w, axis=0, keepdims=True)
             for c in range(nch)], axis=1)
        gate = gate_ref[pl.ds(t, 1), :] if per_token_gate else gate_ref[...]
        return xres_ref[pl.ds(t, 1), :] + gate * y

    ahead = PEER_NSLOT - 1

    key_rows = lax.broadcasted_iota(jnp.int32, (PEER_NKEYS, tb), 0)
    cand_rows = lax.broadcasted_iota(jnp.int32, (N_CAND, tb), 0)

    def sl_scores(head):
        qh = q_scr[head].astype(jnp.bfloat16)
        for p in range(2):
            s_scr[p] = lax.dot_general(sk_ref[p], qh[:, p * HALF_KEY:(p + 1) * HALF_KEY],
                                       (((1,), (1,)), ((), ())), preferred_element_type=jnp.float32)

    def sl_stage1(k0, cnt, head):
        for p in range(2):
            sc_ = s_scr[p]
            for k in range(k0, k0 + cnt):
                m, r, sc_ = _extract_top(sc_, key_rows)
                tv_scr[p, k:k + 1, :] = m
                ti_scr[p, k:k + 1, :] = r
            s_scr[p] = sc_

    def sl_cand(head):
        c, ci = _pair_candidates(tv_scr[0], ti_scr[0], tv_scr[1], ti_scr[1])
        cv_scr[...] = c
        ci_scr[...] = ci

    def sl_stage2(k0, cnt, head):
        c, ci = cv_scr[...], ci_scr[...]
        for k in range(k0, k0 + cnt):
            m, e, c = _extract_top(c, cand_rows, ci)
            fv_scr[k:k + 1, :] = m
            fi_scr[k:k + 1, :] = e
        cv_scr[...] = c

    def sl_out(head):
        top_s = fv_scr[...]
        e = jnp.exp(top_s - jnp.max(top_s, axis=0, keepdims=True))
        off = pl.multiple_of(head * PEER_TOPK, PEER_TOPK)
        idx_v[pl.ds(off, PEER_TOPK), :] = fi_scr[...]
        gt_scr[slot_r, pl.ds(off, PEER_TOPK), :] = e / jnp.sum(e, axis=0, keepdims=True)

    per_slice = PEER_TOPK // PEER_NSLOT
    slices = [sl_scores]
    slices += [functools.partial(sl_stage1, k0, per_slice) for k0 in range(0, PEER_TOPK, per_slice)]
    slices += [sl_cand]
    k0 = 0
    for cnt in (4, 3, 3, 3, 3):
        slices.append(functools.partial(sl_stage2, k0, cnt))
        k0 += cnt
    slices += [sl_out]
    assert len(slices) == 2 * PEER_NSLOT and k0 == PEER_TOPK

    def publish_routing():
        cp = pltpu.make_async_copy(idx_v, idx_s, idx_sem.at[0])
        cp.start()
        cp.wait()

    def group(t0, head, todo, last):
        for j in range(PEER_NSLOT):
            t = t0 + j
            wait(j)
            out = ffn(t, j)
            if not last or j == 0:
                issue(t + ahead, (j + ahead) % PEER_NSLOT)
            for i in todo[j]:
                slices[i](head)
            if last:
                @pl.when(s < nblk)
                def _():
                    if j == 0:
                        publish_routing()
                    else:
                        issue(j - 1, j - 1)
            o_ref[pl.ds(t, 1), :] = out

    @pl.when(s == 0)
    def _():
        assert tab_ref.shape[0] // nch >= PEER_SEL * tb
        idx_v[...] = (lax.broadcasted_iota(jnp.int32, idx_v.shape, 0) * tb
                      + lax.broadcasted_iota(jnp.int32, idx_v.shape, 1))
        h_scr[slot_e] = jnp.zeros(h_scr.shape[1:], h_scr.dtype)
        gt_scr[slot_e] = jnp.zeros(gt_scr.shape[1:], gt_scr.dtype)
        publish_routing()
        for t0 in range(ahead):
            issue(t0, t0)

    h = _norm_mod(xr_ref[...], gain_ref[...], sc_ref[...], sh_ref[...])
    h_scr[slot_r] = h
    q = jnp.dot(h.astype(jnp.bfloat16), wq_ref[...], preferred_element_type=jnp.float32)
    for head in range(PEER_HEADS):
        q_scr[head] = q[:, head * PEER_DKEY:(head + 1) * PEER_DKEY]

    n_pairs = tb // (2 * PEER_NSLOT)
    assert n_pairs == PEER_HEADS

    one_each = [[j] for j in range(2 * PEER_NSLOT)]
    two_each = [[2 * j, 2 * j + 1] for j in range(PEER_NSLOT)]

    def body(gi, carry):
        t0 = pl.multiple_of(gi * (2 * PEER_NSLOT), 2 * PEER_NSLOT)
        group(t0, gi, one_each[:PEER_NSLOT], False)
        group(t0 + PEER_NSLOT, gi, one_each[PEER_NSLOT:], False)
        return carry

    lax.fori_loop(0, n_pairs - 1, body, 0)
    group(tb - 2 * PEER_NSLOT, n_pairs - 1, two_each, False)
    group(tb - PEER_NSLOT, n_pairs - 1, [[]] * PEER_NSLOT, True)


def _pack_kernel(u_ref, v_ref, o_ref):
    def bf16_bits(x):
        return lax.bitcast_convert_type(x.astype(jnp.bfloat16).astype(jnp.float32), jnp.uint32)

    for c in range(o_ref.shape[1]):
        cols = slice(c * LANES, (c + 1) * LANES)
        o_ref[:, c, :] = bf16_bits(u_ref[:, cols]) | (bf16_bits(v_ref[:, cols]) >> 16)


def pack_expert_table(u, v, layer):
    _, e, d = u.shape
    te = ROW_TILE
    nch = d // LANES
    src = pl.BlockSpec((None, te, d), lambda i: (layer, i, 0))
    out = pl.pallas_call(
        _pack_kernel,
        grid=(e // te,),
        in_specs=[src, src],
        out_specs=pl.BlockSpec((te, nch, LANES), lambda i: (i, 0, 0)),
        out_shape=jax.ShapeDtypeStruct((e, nch, LANES), jnp.uint32),
        name="pack_experts",
    )(u, v)
    return out.reshape(e * nch, LANES)


def peer_layer(x, gain, sc, sh, gate, wq_bf, sk_bf, tab):
    n, d = x.shape
    tb = PEER_TB
    nblk = n // tb
    nch = d // LANES

    def cur(s):
        return jnp.minimum(s, nblk - 1)

    def prev(s):
        return jnp.maximum(s - 1, 0)

    def rows(arr, blk):
        g = arr.shape[0]
        if g == n:
            return arr, pl.BlockSpec((tb, d), lambda s: (blk(s), 0))
        per = n // g
        return arr[:, None, :], pl.BlockSpec((None, 1, d), lambda s: (blk(s) * tb // per, 0, 0))

    sc, sc_spec = rows(sc, cur)
    sh, sh_spec = rows(sh, cur)
    gate, gate_spec = rows(gate, prev)
    return pl.pallas_call(
        functools.partial(_peer_kernel, nblk),
        grid=(nblk + 1,),
        in_specs=[pl.BlockSpec((tb, d), lambda s: (cur(s), 0)),
                  pl.BlockSpec((1, d), lambda s: (0, 0)),
                  sc_spec, sh_spec,
                  pl.BlockSpec(wq_bf.shape, lambda s: (0, 0)),
                  pl.BlockSpec(sk_bf.shape, lambda s: (0, 0, 0)),
                  pl.BlockSpec((tb, d), lambda s: (prev(s), 0)),
                  gate_spec,
                  pl.BlockSpec(memory_space=pl.ANY)],
        out_specs=pl.BlockSpec((tb, d), lambda s: (prev(s), 0)),
        out_shape=jax.ShapeDtypeStruct((n, d), jnp.float32),
        scratch_shapes=[pltpu.VMEM((PEER_HEADS, tb, PEER_DKEY), jnp.float32),
                        pltpu.VMEM((2, tb, d), jnp.float32),
                        pltpu.VMEM((2, PEER_SEL, tb), jnp.float32),
                        pltpu.VMEM((PEER_SEL, tb), jnp.int32),
                        pltpu.SMEM((PEER_SEL, tb), jnp.int32),
                        pltpu.VMEM((2, PEER_NKEYS, tb), jnp.float32),
                        pltpu.VMEM((2, PEER_TOPK, tb), jnp.float32),
                        pltpu.VMEM((2, PEER_TOPK, tb), jnp.int32),
                        pltpu.VMEM((N_CAND, tb), jnp.float32),
                        pltpu.VMEM((N_CAND, tb), jnp.int32),
                        pltpu.VMEM((PEER_TOPK, tb), jnp.float32),
                        pltpu.VMEM((PEER_TOPK, tb), jnp.int32),
                        pltpu.VMEM((PEER_NSLOT, nch, PEER_SEL, LANES), jnp.uint32),
                        pltpu.SemaphoreType.DMA((PEER_NSLOT,)),
                        pltpu.SemaphoreType.DMA((1,))],
        compiler_params=pltpu.CompilerParams(dimension_semantics=("arbitrary",)),
        name="peer",
    )(x, gain, sc, sh, wq_bf, sk_bf, x, gate, tab)


def _final_norm_kernel(x_ref, g_ref, o_ref):
    x = x_ref[...]
    o_ref[...] = x * lax.rsqrt(jnp.mean(x * x, axis=-1, keepdims=True) + EPS) * g_ref[...]


def final_norm(x, g):
    n, d = x.shape
    tm = min(ROW_TILE, n)
    return pl.pallas_call(
        _final_norm_kernel,
        grid=(n // tm,),
        in_specs=[pl.BlockSpec((tm, d), lambda i: (i, 0)), pl.BlockSpec((1, d), lambda i: (0, 0))],
        out_specs=pl.BlockSpec((tm, d), lambda i: (i, 0)),
        out_shape=jax.ShapeDtypeStruct((n, d), x.dtype),
        name="final_norm",
    )(x, g.reshape(1, d))


def _per_tile_rows(m, seq_len, tile):
    return m if seq_len % tile == 0 else jnp.repeat(m, seq_len, axis=0)


def trunk_layer(x, seq_len, mod, p, state):
    n, d = x.shape
    bsz = n // seq_len
    sh1, sc1, gt1, sh2, sc2, gt2 = (_per_tile_rows(m, seq_len, ROW_TILE) for m in jnp.split(mod, 6, axis=-1))
    z = mix_in(x, p['g_mix'], sc1, sh1, p['w_in'])
    k_new = z[:, Z_SK:Z_SK + SWA_KVW].reshape(bsz, seq_len, SWA_KVW)
    v_new = z[:, Z_SV:Z_SV + SWA_KVW].reshape(bsz, seq_len, SWA_KVW)
    if state is None:
        win_rows = min(WINDOW, PAST_LEN)
        c0 = jnp.zeros((bsz, MLSTM_HEADS, MLSTM_DH, MLSTM_DH), jnp.float32)
        n0 = jnp.zeros((bsz, MLSTM_HEADS, MLSTM_DH), jnp.float32)
        m0 = jnp.zeros((bsz, MLSTM_HEADS), jnp.float32)
        cbuf = jnp.zeros((bsz, CONV_WIDTH - 1, CONV_W), jnp.float32)
        oa = swa_banded(z, p['sinks'], seq_len)
        k_all, v_all = k_new, v_new
    else:
        kbuf, vbuf, cbuf, c0, n0, m0 = state
        win_rows = kbuf.shape[1]
        k_all = jnp.concatenate([kbuf.reshape(bsz, win_rows, SWA_KVW), k_new], axis=1)
        v_all = jnp.concatenate([vbuf.reshape(bsz, win_rows, SWA_KVW), v_new], axis=1)
        q = z[:, Z_SQ:Z_SQ + SWA_W].reshape(bsz, seq_len, SWA_W)
        oa = swa_cached(q, k_all, v_all, p['sinks']).reshape(n, SWA_W)
    hm, c1, n1, m1 = mlstm_layer(z, p['gate_b'], p['mh_g'], c0, n0, m0, seq_len)
    x, u = mix_out(x, gt1, hm, oa, z, cbuf, p['conv_w'], p['w_out'], seq_len)
    k_keep = k_all[:, -win_rows:].reshape(bsz, win_rows, SWA_KV_HEADS, SWA_DH)
    v_keep = v_all[:, -win_rows:].reshape(bsz, win_rows, SWA_KV_HEADS, SWA_DH)
    cbuf_new = jnp.concatenate([cbuf, u.reshape(bsz, seq_len, CONV_W)], axis=1)[:, -(CONV_WIDTH - 1):]
    x = peer_layer(x, p['g_ffn'], sc2, sh2, gt2, p['wq'], p['subkeys'], p['tab'])
    return x, (k_keep, v_keep, cbuf_new, c1, n1, m1)


def kernel(x_prompt, x_sample, cache_swa_k, cache_swa_v, state_conv, state_mlstm_C, state_mlstm_n, state_mlstm_m, c_prompt, c_sample, ada_w, ada_b, norm_mix_g, norm_ffn_g, w_in, w_out, mlstm_gate_b, mlstm_norm_g, swa_sinks, conv_w, peer_wq, peer_subkeys, peer_u, peer_v, final_g):
    bp, tp, d = x_prompt.shape
    bs, ts, _ = x_sample.shape
    xp, xs = x_prompt.reshape(bp * tp, d), x_sample.reshape(bs * ts, d)
    c_all = jnp.concatenate([c_prompt, c_sample], axis=0)
    bf = jnp.bfloat16
    n_gate = 2 * MLSTM_HEADS
    new_p, new_s = [], []
    for l in range(DEPTH):
        wl = w_in[l]
        w_perm = jnp.concatenate([wl[:, :Z_SQ], wl[:, Z_SQ + n_gate:], wl[:, Z_SQ:Z_SQ + n_gate],
                                  jnp.zeros((d, LANES - n_gate), wl.dtype)], axis=1)
        p = dict(g_mix=norm_mix_g[l].reshape(1, d), g_ffn=norm_ffn_g[l].reshape(1, d),
                 w_in=w_perm.astype(bf), w_out=w_out[l].astype(bf), gate_b=mlstm_gate_b[l],
                 mh_g=mlstm_norm_g[l], sinks=swa_sinks[l], conv_w=conv_w[l], wq=peer_wq[l].astype(bf),
                 subkeys=peer_subkeys[l].astype(bf), tab=pack_expert_table(peer_u, peer_v, l))
        mod = adaln_mod(c_all, ada_w[l], ada_b[l])
        xp, sp = trunk_layer(xp, tp, mod[:bp], p, None)
        st = (cache_swa_k[l], cache_swa_v[l], state_conv[l], state_mlstm_C[l], state_mlstm_n[l], state_mlstm_m[l])
        xs, ss = trunk_layer(xs, ts, mod[bp:], p, st)
        new_p.append(sp)
        new_s.append(ss)
    y_prompt = final_norm(xp, final_g).reshape(bp, tp, d)
    y_sample = final_norm(xs, final_g).reshape(bs, ts, d)
    pk, pv, pc, pC, pn, pm = [jnp.stack(t) for t in zip(*new_p)]
    sk, sv, sc, sC, sn, sm = [jnp.stack(t) for t in zip(*new_s)]
    return (y_prompt, y_sample, pk, pv, pc, pC, pn, pm, sk, sv, sc, sC, sn, sm)
```

```python
import functools
import math

import jax
import jax.numpy as jnp
from jax import lax
from jax.experimental import pallas as pl
from jax.experimental.pallas import tpu as pltpu

D_MODEL = 1024
DEPTH = 2
PAST_LEN = 16384

MLSTM_W = D_MODEL // 2
MLSTM_HEADS = 4
MLSTM_DH = MLSTM_W // MLSTM_HEADS
MLSTM_CHUNK = 64
SWA_W = D_MODEL // 4
SWA_DH = 64
SWA_HEADS = SWA_W // SWA_DH
SWA_KV_HEADS = SWA_HEADS // 2
SWA_GROUP = SWA_HEADS // SWA_KV_HEADS
SWA_KVW = SWA_KV_HEADS * SWA_DH
WINDOW = 128
CONV_W = D_MODEL - MLSTM_W - SWA_W
CONV_WIDTH = 3
PEER_HEADS = 8
PEER_NKEYS = 128
PEER_DKEY = 256
HALF_KEY = PEER_DKEY // 2
PEER_TOPK = 16
PEER_SEL = PEER_HEADS * PEER_TOPK
EPS = 1e-6

LANES = 128
SUBLANES = 8
ROW_TILE = 512
PEER_TB = LANES
PEER_NSLOT = 8
MLSTM_SEQS = 2

Z_Q, Z_K, Z_V, Z_O = 0, MLSTM_W, 2 * MLSTM_W, 3 * MLSTM_W
Z_SQ = 4 * MLSTM_W
Z_SK = Z_SQ + SWA_W
Z_SV = Z_SK + SWA_KVW
Z_CB = Z_SV + SWA_KVW
Z_CC = Z_CB + CONV_W
Z_CH = Z_CC + CONV_W
Z_GATE = Z_CH + CONV_W
Z_W = Z_GATE + LANES
ALIBI = tuple(2.0 ** (-8.0 * h / SWA_HEADS) for h in range(1, SWA_HEADS + 1))


def _bdot(a, b, dims):
    return lax.dot_general(a.astype(jnp.bfloat16), b.astype(jnp.bfloat16), (dims, ((), ())),
                           preferred_element_type=jnp.float32)


def _row_spec(arr, n, tm, d):
    g = arr.shape[0]
    if g == n:
        return arr, pl.BlockSpec((tm, d), lambda i, *_: (i, 0))
    per = n // g
    return arr[:, None, :], pl.BlockSpec((None, 1, d), lambda i, *_: (i * tm // per, 0, 0))


def _norm_mod(x, gain, sc, sh):
    y = x * lax.rsqrt(jnp.mean(x * x, axis=-1, keepdims=True) + EPS) * gain
    return y * (1.0 + sc) + sh


def _adaln_kernel(c_ref, w_ref, b_ref, o_ref):
    c = c_ref[...]
    o_ref[...] = _bdot(c * jax.nn.sigmoid(c), w_ref[...], ((1,), (0,))) + b_ref[...]


def adaln_mod(c, w, b):
    bsz, d = c.shape
    m = w.shape[1]
    tn = 512
    return pl.pallas_call(
        _adaln_kernel,
        grid=(m // tn,),
        in_specs=[pl.BlockSpec((bsz, d), lambda j: (0, 0)),
                  pl.BlockSpec((d, tn), lambda j: (0, j)),
                  pl.BlockSpec((1, tn), lambda j: (0, j))],
        out_specs=pl.BlockSpec((bsz, tn), lambda j: (0, j)),
        out_shape=jax.ShapeDtypeStruct((bsz, m), jnp.float32),
        name="adaln_mod",
    )(c, w, b.reshape(1, m))


def _mix_in_kernel(x_ref, gain_ref, sc_ref, sh_ref, w_ref, z_ref):
    h = _norm_mod(x_ref[...], gain_ref[...], sc_ref[...], sh_ref[...])
    z_ref[...] = jnp.dot(h.astype(jnp.bfloat16), w_ref[...], preferred_element_type=jnp.float32)


def mix_in(x, gain, sc, sh, w_bf):
    n, d = x.shape
    zw = w_bf.shape[1]
    tm, tn = min(ROW_TILE, n), zw
    sc, sc_spec = _row_spec(sc, n, tm, d)
    sh, sh_spec = _row_spec(sh, n, tm, d)
    return pl.pallas_call(
        _mix_in_kernel,
        grid=(n // tm, zw // tn),
        in_specs=[pl.BlockSpec((tm, d), lambda i, j: (i, 0)),
                  pl.BlockSpec((1, d), lambda i, j: (0, 0)),
                  sc_spec, sh_spec,
                  pl.BlockSpec((d, tn), lambda i, j: (0, j))],
        out_specs=pl.BlockSpec((tm, tn), lambda i, j: (i, j)),
        out_shape=jax.ShapeDtypeStruct((n, zw), jnp.float32),
        name="mix_in",
    )(x, gain, sc, sh, w_bf)


def _log_sigmoid(x):
    return jnp.minimum(x, 0.0) - jnp.log(1.0 + jnp.exp(-jnp.abs(x)))


def _mlstm_kernel(chunk, valid, q_ref, k_ref, v_ref, o_ref, gate_ref, gb_ref, g_ref, c0_ref, n0_ref, m0_ref,
                  hm_ref, c_ref, n_ref, m_ref):
    nb, tc = q_ref.shape[:2]
    nh, dh, L = MLSTM_HEADS, MLSTM_DH, chunk

    @pl.when(pl.program_id(1) == 0)
    def _():
        c_ref[...] = c0_ref[...]
        n_ref[...] = n0_ref[...]
        m_ref[...] = m0_ref[...]

    row = lax.broadcasted_iota(jnp.int32, (L, L), 0)
    col = lax.broadcasted_iota(jnp.int32, (L, L), 1)
    eye = row == col
    visible = (col <= row) & (col < valid)
    rcol = lax.broadcasted_iota(jnp.int32, (L, 1), 0)
    gb = gb_ref[...]
    gain = g_ref[...]

    def to_row(x_col):
        return jnp.sum(jnp.where(eye, x_col, 0.0), axis=0, keepdims=True)

    def one_chunk(ci, carry):
        r0 = pl.multiple_of(ci * L, L)
        for sq, hd in [(a, b) for a in range(nb) for b in range(nh)]:
            pre = gate_ref[sq, pl.ds(r0, L), :] + gb
            sl = slice(hd * dh, (hd + 1) * dh)
            q = q_ref[sq, pl.ds(r0, L), sl]
            k = k_ref[sq, pl.ds(r0, L), sl] * (dh ** -0.5)
            v = v_ref[sq, pl.ds(r0, L), sl]
            ig = pre[:, hd:hd + 1]
            lf = jnp.where(rcol < valid, _log_sigmoid(pre[:, nh + hd:nh + hd + 1]), 0.0)
            b = lf
            s = 1
            while s < L:
                b = b + jnp.where(rcol >= s, pltpu.roll(b, s, 0), 0.0)
                s *= 2
            cmat, nrow, m_prev = c_ref[sq, hd], n_ref[sq, hd], m_ref[sq, hd][:, 0:1]
            a = b + m_prev
            d = jnp.where(visible, b - to_row(b) + to_row(ig), -jnp.inf)
            m_t = jnp.maximum(a, jnp.max(d, axis=-1, keepdims=True))
            w_inter = jnp.exp(a - m_t)
            w_intra = jnp.exp(d - m_t)
            qk = _bdot(q, k, ((1,), (1,))) * w_intra
            num = w_inter * _bdot(q, cmat, ((1,), (1,))) + _bdot(qk, v, ((1,), (0,)))
            den = w_inter * jnp.sum(q * nrow, axis=-1, keepdims=True) + jnp.sum(qk, axis=-1, keepdims=True)
            h = num / jnp.maximum(jnp.abs(den), jnp.exp(-m_t))
            hn = h * lax.rsqrt(jnp.mean(h * h, axis=-1, keepdims=True) + EPS) * gain[:, sl]
            hm_ref[sq, pl.ds(r0, L), sl] = hn * jax.nn.sigmoid(o_ref[sq, pl.ds(r0, L), sl])
            m_last = m_t[L - 1:L]
            wl_inter = w_inter[L - 1:L]
            wl = jnp.where(rcol < valid, jnp.exp(b[L - 1:L] - b + ig - m_last), 0.0)
            c_ref[sq, hd] = wl_inter * cmat + _bdot(wl * v, k, ((0,), (0,)))
            n_ref[sq, hd] = wl_inter * nrow + jnp.sum(wl * k, axis=0, keepdims=True)
            m_ref[sq, hd] = jnp.broadcast_to(m_last, (1, dh))
        return carry

    lax.fori_loop(0, tc // L, one_chunk, 0)


def mlstm_layer(z, gate_b, mh_g, c0, n0, m0, seq_len):
    bsz = z.shape[0] // seq_len
    nh, dh = MLSTM_HEADS, MLSTM_DH
    chunk = math.gcd(seq_len, MLSTM_CHUNK)
    valid, padded = chunk, seq_len
    if chunk % SUBLANES:
        assert seq_len < SUBLANES
        chunk = padded = SUBLANES
        z = jnp.pad(z.reshape(bsz, seq_len, -1), ((0, 0), (0, padded - seq_len), (0, 0)))
    z = z.reshape(bsz, padded, -1)
    nb = MLSTM_SEQS
    assert bsz % nb == 0
    tc = min(padded, ROW_TILE)
    w = nh * dh
    gb = jnp.zeros((1, LANES), jnp.float32).at[0, :2 * nh].set(gate_b)

    def zcol(off, width):
        return pl.BlockSpec((nb, tc, width), lambda b, c: (b, c, off // width))

    def state(shape):
        return pl.BlockSpec((nb,) + shape, lambda b, c: (b,) + (0,) * len(shape))

    hm, c1, n1, m1 = pl.pallas_call(
        functools.partial(_mlstm_kernel, chunk, valid),
        grid=(bsz // nb, padded // tc),
        in_specs=[zcol(Z_Q, w), zcol(Z_K, w), zcol(Z_V, w), zcol(Z_O, w), zcol(Z_GATE, LANES),
                  pl.BlockSpec((1, LANES), lambda b, c: (0, 0)),
                  pl.BlockSpec((1, w), lambda b, c: (0, 0)),
                  state((nh, dh, dh)), state((nh, 1, dh)), state((nh, 1, dh))],
        out_specs=[pl.BlockSpec((nb, tc, w), lambda b, c: (b, c, 0)),
                   state((nh, dh, dh)), state((nh, 1, dh)), state((nh, 1, dh))],
        out_shape=[jax.ShapeDtypeStruct((bsz, padded, w), jnp.float32),
                   jax.ShapeDtypeStruct((bsz, nh, dh, dh), jnp.float32),
                   jax.ShapeDtypeStruct((bsz, nh, 1, dh), jnp.float32),
                   jax.ShapeDtypeStruct((bsz, nh, 1, dh), jnp.float32)],
        compiler_params=pltpu.CompilerParams(dimension_semantics=("arbitrary", "arbitrary")),
        name="mlstm",
    )(z, z, z, z, z, gb, mh_g.reshape(1, w), c0, n0[:, :, None, :],
      jnp.broadcast_to(m0[:, :, None, None], (bsz, nh, 1, dh)))
    return hm[:, :seq_len].reshape(bsz * seq_len, w), c1, n1[:, :, 0, :], m1[:, :, 0, 0]


def _attend(q, kb, vb, dist, visible, slope, sink):
    s = _bdot(q, kb, ((1,), (1,))) * (SWA_DH ** -0.5) - slope * dist.astype(jnp.float32)
    s = jnp.where(visible, s, -jnp.inf)
    mx = jnp.maximum(jnp.max(s, axis=-1, keepdims=True), sink)
    e = jnp.exp(s - mx)
    p = e / (jnp.sum(e, axis=-1, keepdims=True) + jnp.exp(sink - mx))
    return _bdot(p, vb, ((1,), (0,)))


def _swa_banded_kernel(q_ref, kp_ref, kc_ref, vp_ref, vc_ref, sink_ref, o_ref):
    j = pl.program_id(1)
    w = q_ref.shape[0]
    t = lax.broadcasted_iota(jnp.int32, (w, 2 * w), 0)
    i = lax.broadcasted_iota(jnp.int32, (w, 2 * w), 1)
    dist = t + w - i
    visible = (dist >= 0) & (dist <= WINDOW) & ((j > 0) | (i >= w))
    sinks = sink_ref[...]
    for kv in range(SWA_KV_HEADS):
        ks = slice(kv * SWA_DH, (kv + 1) * SWA_DH)
        kb = jnp.concatenate([kp_ref[:, ks], kc_ref[:, ks]], axis=0)
        vb = jnp.concatenate([vp_ref[:, ks], vc_ref[:, ks]], axis=0)
        for g in range(SWA_GROUP):
            h = kv * SWA_GROUP + g
            hs = slice(h * SWA_DH, (h + 1) * SWA_DH)
            o_ref[:, hs] = _attend(q_ref[:, hs], kb, vb, dist, visible, ALIBI[h], sinks[:, h:h + 1])


def swa_banded(z, sinks, seq_len):
    n = z.shape[0]
    w = WINDOW
    nb = seq_len // w
    sink_row = jnp.zeros((1, LANES), jnp.float32).at[0, :SWA_HEADS].set(sinks)

    def cur(off, width):
        return pl.BlockSpec((w, width), lambda b, j: (b * nb + j, off // width))

    def prev(off, width):
        return pl.BlockSpec((w, width), lambda b, j: (b * nb + jnp.maximum(j - 1, 0), off // width))

    return pl.pallas_call(
        _swa_banded_kernel,
        grid=(n // seq_len, nb),
        in_specs=[cur(Z_SQ, SWA_W), prev(Z_SK, SWA_KVW), cur(Z_SK, SWA_KVW), prev(Z_SV, SWA_KVW),
                  cur(Z_SV, SWA_KVW), pl.BlockSpec((1, LANES), lambda b, j: (0, 0))],
        out_specs=pl.BlockSpec((w, SWA_W), lambda b, j: (b * nb + j, 0)),
        out_shape=jax.ShapeDtypeStruct((n, SWA_W), jnp.float32),
        name="swa_banded",
    )(z, z, z, z, z, sink_row)


def _swa_cached_kernel(t_new, n_keys, q_ref, k_ref, v_ref, sink_ref, o_ref):
    m = q_ref.shape[1]
    nkp = k_ref.shape[0]
    r = lax.broadcasted_iota(jnp.int32, (m, nkp), 0)
    i = lax.broadcasted_iota(jnp.int32, (m, nkp), 1)
    first_key_pos = PAST_LEN - (n_keys - t_new)
    dist = (n_keys - t_new) + r % t_new - i
    visible = (dist >= 0) & (dist <= WINDOW) & (i < n_keys) & (first_key_pos + i >= 0)
    rg = lax.broadcasted_iota(jnp.int32, (m, 1), 0) // t_new
    sinks = sink_ref[...]
    for kv in range(SWA_KV_HEADS):
        ks = slice(kv * SWA_DH, (kv + 1) * SWA_DH)
        h0 = kv * SWA_GROUP
        slope = jnp.where(rg == 0, ALIBI[h0], ALIBI[h0 + 1])
        sink = jnp.where(rg == 0, sinks[:, h0:h0 + 1], sinks[:, h0 + 1:h0 + 2])
        o_ref[kv] = _attend(q_ref[kv], k_ref[:, ks], v_ref[:, ks], dist, visible, slope, sink)


def swa_cached(q, k_all, v_all, sinks):
    assert SWA_GROUP == 2
    bsz, t_new, _ = q.shape
    n_keys = k_all.shape[1]
    nkp = -(-n_keys // SUBLANES) * SUBLANES
    pad = ((0, 0), (0, nkp - n_keys), (0, 0))
    k_all, v_all = jnp.pad(k_all, pad), jnp.pad(v_all, pad)
    m = SWA_GROUP * t_new
    qs = q.reshape(bsz, t_new, SWA_KV_HEADS, SWA_GROUP, SWA_DH).transpose(0, 2, 3, 1, 4)
    qs = qs.reshape(bsz, SWA_KV_HEADS, m, SWA_DH)
    sink_row = jnp.zeros((1, LANES), jnp.float32).at[0, :SWA_HEADS].set(sinks)
    o = pl.pallas_call(
        functools.partial(_swa_cached_kernel, t_new, n_keys),
        grid=(bsz,),
        in_specs=[pl.BlockSpec((None, SWA_KV_HEADS, m, SWA_DH), lambda b: (b, 0, 0, 0)),
                  pl.BlockSpec((None, nkp, SWA_KVW), lambda b: (b, 0, 0)),
                  pl.BlockSpec((None, nkp, SWA_KVW), lambda b: (b, 0, 0)),
                  pl.BlockSpec((1, LANES), lambda b: (0, 0))],
        out_specs=pl.BlockSpec((None, SWA_KV_HEADS, m, SWA_DH), lambda b: (b, 0, 0, 0)),
        out_shape=jax.ShapeDtypeStruct((bsz, SWA_KV_HEADS, m, SWA_DH), jnp.float32),
        name="swa_cached",
    )(qs, k_all, v_all, sink_row)
    o = o.reshape(bsz, SWA_KV_HEADS, SWA_GROUP, t_new, SWA_DH).transpose(0, 3, 1, 2, 4)
    return o.reshape(bsz, t_new, SWA_W)


def _mix_out_kernel(seq_len, x_ref, gate_ref, hm_ref, oa_ref, cb_ref, cc_ref, ch_ref, hcc_ref, hch_ref,
                    pa_ref, pb_ref, cw_ref, w_ref, o_ref, u_ref):
    i = pl.program_id(0)
    tm = x_ref.shape[0]
    u = cc_ref[...] * ch_ref[...]
    u_ref[...] = u
    hu = hcc_ref[...] * hch_ref[...]
    r = lax.broadcasted_iota(jnp.int32, u.shape, 0)
    p = (i * tm + r) % seq_len
    pa, pb = pa_ref[...], pb_ref[...]
    last, last2 = hu[SUBLANES - 1:SUBLANES], hu[SUBLANES - 2:SUBLANES - 1]
    u1 = jnp.where(r >= 1, pltpu.roll(u, 1, 0), last)
    u2 = jnp.where(r >= 2, pltpu.roll(u, 2, 0), jnp.where(r == 1, last, last2))
    s1 = jnp.where(p >= 1, u1, pa)
    s2 = jnp.where(p >= 2, u2, jnp.where(p == 1, pa, pb))
    cw = cw_ref[...]
    yc = cb_ref[...] * (cw[0:1] * s2 + cw[1:2] * s1 + cw[2:3] * u)
    cat = jnp.concatenate([hm_ref[...], oa_ref[...], yc], axis=-1)
    mix = jnp.dot(cat.astype(jnp.bfloat16), w_ref[...], preferred_element_type=jnp.float32)
    o_ref[...] = x_ref[...] + gate_ref[...] * mix


def mix_out(x, gate, hm, oa, z, conv_prev, conv_w, w_out_bf, seq_len):
    n, d = x.shape
    c = CONV_W
    tm = min(ROW_TILE, n)
    gate, gate_spec = _row_spec(gate, n, tm, d)
    if seq_len % tm:
        pa = jnp.repeat(conv_prev[:, 1], seq_len, axis=0)
        pb = jnp.repeat(conv_prev[:, 0], seq_len, axis=0)
    else:
        pa, pb = conv_prev[:, 1], conv_prev[:, 0]
    pa, pa_spec = _row_spec(pa, n, tm, c)
    pb, pb_spec = _row_spec(pb, n, tm, c)
    cw = jnp.zeros((SUBLANES, c), jnp.float32).at[:CONV_WIDTH].set(conv_w)

    def zcol(off, width):
        return pl.BlockSpec((tm, width), lambda i: (i, off // width))

    def zhalo(off):
        return pl.BlockSpec((SUBLANES, c), lambda i: (jnp.maximum(i * (tm // SUBLANES) - 1, 0), off // c))

    return pl.pallas_call(
        functools.partial(_mix_out_kernel, seq_len),
        grid=(n // tm,),
        in_specs=[pl.BlockSpec((tm, d), lambda i: (i, 0)), gate_spec,
                  pl.BlockSpec((tm, MLSTM_W), lambda i: (i, 0)),
                  pl.BlockSpec((tm, SWA_W), lambda i: (i, 0)),
                  zcol(Z_CB, c), zcol(Z_CC, c), zcol(Z_CH, c), zhalo(Z_CC), zhalo(Z_CH),
                  pa_spec, pb_spec,
                  pl.BlockSpec((SUBLANES, c), lambda i: (0, 0)),
                  pl.BlockSpec((d, d), lambda i: (0, 0))],
        out_specs=[pl.BlockSpec((tm, d), lambda i: (i, 0)), pl.BlockSpec((tm, c), lambda i: (i, 0))],
        out_shape=[jax.ShapeDtypeStruct((n, d), jnp.float32), jax.ShapeDtypeStruct((n, c), jnp.float32)],
        name="mix_out",
    )(x, gate, hm, oa, z, z, z, z, z, pa, pb, cw, w_out_bf)


def _pair_candidates(s1, i1, s2, i2):
    k, sub = PEER_TOPK, SUBLANES
    assert k == 2 * sub
    row = lax.broadcasted_iota(jnp.int32, (sub, s1.shape[1]), 0)
    sums, ids = [], []

    def emit(a, b0, nvalid):
        c = s1[a:a + 1] + s2[b0:b0 + sub]
        sums.append(c if nvalid >= sub else jnp.where(row < nvalid, c, -jnp.inf))
        ids.append(i1[a:a + 1] * PEER_NKEYS + i2[b0:b0 + sub])

    for a in range(sub):
        nb = k // (a + 1)
        for b0 in range(0, nb, sub):
            emit(a, b0, nb - b0)
    sums.append(s1[sub:] + s2[0:1])
    ids.append(i1[sub:] * PEER_NKEYS + i2[0:1])
    return jnp.concatenate(sums, axis=0), jnp.concatenate(ids, axis=0)


N_CAND = SUBLANES * (sum(-(-(PEER_TOPK // (a + 1)) // SUBLANES) for a in range(SUBLANES)) + 1)


def _extract_top(s, rows, payload=None):
    m = jnp.max(s, axis=0, keepdims=True)
    r = jnp.min(jnp.where(s == m, rows, jnp.int32(s.shape[0])), axis=0, keepdims=True)
    hit = rows == r
    ident = r if payload is None else jnp.max(jnp.where(hit, payload, -1), axis=0, keepdims=True)
    return m, ident, jnp.where(hit, -jnp.inf, s)


def _gelu_tanh(x):
    return 0.5 * x * (1.0 + jnp.tanh(math.sqrt(2.0 / math.pi) * (x + 0.044715 * (x * x * x))))


def _peer_kernel(nblk, xr_ref, gain_ref, sc_ref, sh_ref, wq_ref, sk_ref, xres_ref, gate_ref, tab_ref,
                 o_ref, q_scr, h_scr, gt_scr, idx_v, idx_s, s_scr, tv_scr, ti_scr, cv_scr, ci_scr, fv_scr, fi_scr,
                 buf, sem, idx_sem):
    s = pl.program_id(0)
    tb, d = xr_ref.shape
    nch = d // LANES
    assert tb == LANES
    slot_r = s % 2
    slot_e = 1 - slot_r
    per_token_gate = gate_ref.shape[0] == tb

    def row_copy(e, slot, r):
        src = tab_ref.at[pl.ds(pl.multiple_of(e * nch, nch), nch)]
        return pltpu.make_async_copy(src, buf.at[slot, :, r, :], sem.at[slot])

    def issue(t, slot):
        for r in range(PEER_SEL):
            row_copy(idx_s[r, t], slot, r).start(priority=slot % 2)

    def wait(slot):
        pltpu.make_async_copy(tab_ref.at[pl.ds(0, PEER_SEL * nch)], buf.at[slot], sem.at[slot]).wait()

    lane = lax.broadcasted_iota(jnp.int32, (PEER_SEL, tb), 1)

    def ffn(t, slot):
        hrow = h_scr[slot_e, pl.ds(t, 1), :]
        part = None
        for c in range(nch):
            u = lax.bitcast_convert_type(buf[slot, c] & jnp.uint32(0xFFFF0000), jnp.float32)
            term = u * hrow[:, c * LANES:(c + 1) * LANES]
            part = term if part is None else part + term
        sdot = jnp.sum(part, axis=-1, keepdims=True)
        g = jnp.sum(jnp.where(lane == t, gt_scr[slot_e], 0.0), axis=-1, keepdims=True)
        w = g * _gelu_tanh(sdot)
        y = jnp.concatenate(
            [jnp.sum(lax.bitcast_convert_type(buf[slot, c] << 16, jnp.float32) * w, axis=0, keepdims=True)
             for c in range(nch)], axis=1)
        gate = gate_ref[pl.ds(t, 1), :] if per_token_gate else gate_ref[...]
        return xres_ref[pl.ds(t, 1), :] + gate * y

    ahead = PEER_NSLOT - 1

    key_rows = lax.broadcasted_iota(jnp.int32, (PEER_NKEYS, tb), 0)
    cand_rows = lax.broadcasted_iota(jnp.int32, (N_CAND, tb), 0)

    def sl_scores(head):
        qh = q_scr[head].astype(jnp.bfloat16)
        for p in range(2):
            s_scr[p] = lax.dot_general(sk_ref[p], qh[:, p * HALF_KEY:(p + 1) * HALF_KEY],
                                       (((1,), (1,)), ((), ())), preferred_element_type=jnp.float32)

    def sl_stage1(k0, cnt, head):
        for p in range(2):
            sc_ = s_scr[p]
            for k in range(k0, k0 + cnt):
                m, r, sc_ = _extract_top(sc_, key_rows)
                tv_scr[p, k:k + 1, :] = m
                ti_scr[p, k:k + 1, :] = r
            s_scr[p] = sc_

    def sl_cand(head):
        c, ci = _pair_candidates(tv_scr[0], ti_scr[0], tv_scr[1], ti_scr[1])
        cv_scr[...] = c
        ci_scr[...] = ci

    def sl_stage2(k0, cnt, head):
        c, ci = cv_scr[...], ci_scr[...]
        for k in range(k0, k0 + cnt):
            m, e, c = _extract_top(c, cand_rows, ci)
            fv_scr[k:k + 1, :] = m
            fi_scr[k:k + 1, :] = e
        cv_scr[...] = c

    def sl_out(head):
        top_s = fv_scr[...]
        e = jnp.exp(top_s - jnp.max(top_s, axis=0, keepdims=True))
        off = pl.multiple_of(head * PEER_TOPK, PEER_TOPK)
        idx_v[pl.ds(off, PEER_TOPK), :] = fi_scr[...]
        gt_scr[slot_r, pl.ds(off, PEER_TOPK), :] = e / jnp.sum(e, axis=0, keepdims=True)

    per_slice = PEER_TOPK // PEER_NSLOT
    slices = [sl_scores]
    slices += [functools.partial(sl_stage1, k0, per_slice) for k0 in range(0, PEER_TOPK, per_slice)]
    slices += [sl_cand]
    k0 = 0
    for cnt in (4, 3, 3, 3, 3):
        slices.append(functools.partial(sl_stage2, k0, cnt))
        k0 += cnt
    slices += [sl_out]
    assert len(slices) == 2 * PEER_NSLOT and k0 == PEER_TOPK

    def publish_routing():
        cp = pltpu.make_async_copy(idx_v, idx_s, idx_sem.at[0])
        cp.start()
        cp.wait()

    def group(t0, head, todo, last):
        for j in range(PEER_NSLOT):
            t = t0 + j
            wait(j)
            out = ffn(t, j)
            if not last or j == 0:
                issue(t + ahead, (j + ahead) % PEER_NSLOT)
            for i in todo[j]:
                slices[i](head)
            if last:
                @pl.when(s < nblk)
                def _():
                    if j == 0:
                        publish_routing()
                    else:
                        issue(j - 1, j - 1)
            o_ref[pl.ds(t, 1), :] = out

    @pl.when(s == 0)
    def _():
        assert tab_ref.shape[0] // nch >= PEER_SEL * tb
        idx_v[...] = (lax.broadcasted_iota(jnp.int32, idx_v.shape, 0) * tb
                      + lax.broadcasted_iota(jnp.int32, idx_v.shape, 1))
        h_scr[slot_e] = jnp.zeros(h_scr.shape[1:], h_scr.dtype)
        gt_scr[slot_e] = jnp.zeros(gt_scr.shape[1:], gt_scr.dtype)
        publish_routing()
        for t0 in range(ahead):
            issue(t0, t0)

    h = _norm_mod(xr_ref[...], gain_ref[...], sc_ref[...], sh_ref[...])
    h_scr[slot_r] = h
    q = jnp.dot(h.astype(jnp.bfloat16), wq_ref[...], preferred_element_type=jnp.float32)
    for head in range(PEER_HEADS):
        q_scr[head] = q[:, head * PEER_DKEY:(head + 1) * PEER_DKEY]

    n_pairs = tb // (2 * PEER_NSLOT)
    assert n_pairs == PEER_HEADS

    one_each = [[j] for j in range(2 * PEER_NSLOT)]
    two_each = [[2 * j, 2 * j + 1] for j in range(PEER_NSLOT)]

    def body(gi, carry):
        t0 = pl.multiple_of(gi * (2 * PEER_NSLOT), 2 * PEER_NSLOT)
        group(t0, gi, one_each[:PEER_NSLOT], False)
        group(t0 + PEER_NSLOT, gi, one_each[PEER_NSLOT:], False)
        return carry

    lax.fori_loop(0, n_pairs - 1, body, 0)
    group(tb - 2 * PEER_NSLOT, n_pairs - 1, two_each, False)
    group(tb - PEER_NSLOT, n_pairs - 1, [[]] * PEER_NSLOT, True)


def _pack_kernel(u_ref, v_ref, o_ref):
    def bf16_bits(x):
        return lax.bitcast_convert_type(x.astype(jnp.bfloat16).astype(jnp.float32), jnp.uint32)

    for c in range(o_ref.shape[1]):
        cols = slice(c * LANES, (c + 1) * LANES)
        o_ref[:, c, :] = bf16_bits(u_ref[:, cols]) | (bf16_bits(v_ref[:, cols]) >> 16)


def pack_expert_table(u, v, layer):
    _, e, d = u.shape
    te = ROW_TILE
    nch = d // LANES
    src = pl.BlockSpec((None, te, d), lambda i: (layer, i, 0))
    out = pl.pallas_call(
        _pack_kernel,
        grid=(e // te,),
        in_specs=[src, src],
        out_specs=pl.BlockSpec((te, nch, LANES), lambda i: (i, 0, 0)),
        out_shape=jax.ShapeDtypeStruct((e, nch, LANES), jnp.uint32),
        name="pack_experts",
    )(u, v)
    return out.reshape(e * nch, LANES)


def peer_layer(x, gain, sc, sh, gate, wq_bf, sk_bf, tab):
    n, d = x.shape
    tb = PEER_TB
    nblk = n // tb
    nch = d // LANES

    def cur(s):
        return jnp.minimum(s, nblk - 1)

    def prev(s):
        return jnp.maximum(s - 1, 0)

    def rows(arr, blk):
        g = arr.shape[0]
        if g == n:
            return arr, pl.BlockSpec((tb, d), lambda s: (blk(s), 0))
        per = n // g
        return arr[:, None, :], pl.BlockSpec((None, 1, d), lambda s: (blk(s) * tb // per, 0, 0))

    sc, sc_spec = rows(sc, cur)
    sh, sh_spec = rows(sh, cur)
    gate, gate_spec = rows(gate, prev)
    return pl.pallas_call(
        functools.partial(_peer_kernel, nblk),
        grid=(nblk + 1,),
        in_specs=[pl.BlockSpec((tb, d), lambda s: (cur(s), 0)),
                  pl.BlockSpec((1, d), lambda s: (0, 0)),
                  sc_spec, sh_spec,
                  pl.BlockSpec(wq_bf.shape, lambda s: (0, 0)),
                  pl.BlockSpec(sk_bf.shape, lambda s: (0, 0, 0)),
                  pl.BlockSpec((tb, d), lambda s: (prev(s), 0)),
                  gate_spec,
                  pl.BlockSpec(memory_space=pl.ANY)],
        out_specs=pl.BlockSpec((tb, d), lambda s: (prev(s), 0)),
        out_shape=jax.ShapeDtypeStruct((n, d), jnp.float32),
        scratch_shapes=[pltpu.VMEM((PEER_HEADS, tb, PEER_DKEY), jnp.float32),
                        pltpu.VMEM((2, tb, d), jnp.float32),
                        pltpu.VMEM((2, PEER_SEL, tb), jnp.float32),
                        pltpu.VMEM((PEER_SEL, tb), jnp.int32),
                        pltpu.SMEM((PEER_SEL, tb), jnp.int32),
                        pltpu.VMEM((2, PEER_NKEYS, tb), jnp.float32),
                        pltpu.VMEM((2, PEER_TOPK, tb), jnp.float32),
                        pltpu.VMEM((2, PEER_TOPK, tb), jnp.int32),
                        pltpu.VMEM((N_CAND, tb), jnp.float32),
                        pltpu.VMEM((N_CAND, tb), jnp.int32),
                        pltpu.VMEM((PEER_TOPK, tb), jnp.float32),
                        pltpu.VMEM((PEER_TOPK, tb), jnp.int32),
                        pltpu.VMEM((PEER_NSLOT, nch, PEER_SEL, LANES), jnp.uint32),
                        pltpu.SemaphoreType.DMA((PEER_NSLOT,)),
                        pltpu.SemaphoreType.DMA((1,))],
        compiler_params=pltpu.CompilerParams(dimension_semantics=("arbitrary",)),
        name="peer",
    )(x, gain, sc, sh, wq_bf, sk_bf, x, gate, tab)


def _final_norm_kernel(x_ref, g_ref, o_ref):
    x = x_ref[...]
    o_ref[...] = x * lax.rsqrt(jnp.mean(x * x, axis=-1, keepdims=True) + EPS) * g_ref[...]


def final_norm(x, g):
    n, d = x.shape
    tm = min(ROW_TILE, n)
    return pl.pallas_call(
        _final_norm_kernel,
        grid=(n // tm,),
        in_specs=[pl.BlockSpec((tm, d), lambda i: (i, 0)), pl.BlockSpec((1, d), lambda i: (0, 0))],
        out_specs=pl.BlockSpec((tm, d), lambda i: (i, 0)),
        out_shape=jax.ShapeDtypeStruct((n, d), x.dtype),
        name="final_norm",
    )(x, g.reshape(1, d))


def _per_tile_rows(m, seq_len, tile):
    return m if seq_len % tile == 0 else jnp.repeat(m, seq_len, axis=0)


def trunk_layer(x, seq_len, mod, p, state):
    n, d = x.shape
    bsz = n // seq_len
    sh1, sc1, gt1, sh2, sc2, gt2 = (_per_tile_rows(m, seq_len, ROW_TILE) for m in jnp.split(mod, 6, axis=-1))
    z = mix_in(x, p['g_mix'], sc1, sh1, p['w_in'])
    k_new = z[:, Z_SK:Z_SK + SWA_KVW].reshape(bsz, seq_len, SWA_KVW)
    v_new = z[:, Z_SV:Z_SV + SWA_KVW].reshape(bsz, seq_len, SWA_KVW)
    if state is None:
        win_rows = min(WINDOW, PAST_LEN)
        c0 = jnp.zeros((bsz, MLSTM_HEADS, MLSTM_DH, MLSTM_DH), jnp.float32)
        n0 = jnp.zeros((bsz, MLSTM_HEADS, MLSTM_DH), jnp.float32)
        m0 = jnp.zeros((bsz, MLSTM_HEADS), jnp.float32)
        cbuf = jnp.zeros((bsz, CONV_WIDTH - 1, CONV_W), jnp.float32)
        oa = swa_banded(z, p['sinks'], seq_len)
        k_all, v_all = k_new, v_new
    else:
        kbuf, vbuf, cbuf, c0, n0, m0 = state
        win_rows = kbuf.shape[1]
        k_all = jnp.concatenate([kbuf.reshape(bsz, win_rows, SWA_KVW), k_new], axis=1)
        v_all = jnp.concatenate([vbuf.reshape(bsz, win_rows, SWA_KVW), v_new], axis=1)
        q = z[:, Z_SQ:Z_SQ + SWA_W].reshape(bsz, seq_len, SWA_W)
        oa = swa_cached(q, k_all, v_all, p['sinks']).reshape(n, SWA_W)
    hm, c1, n1, m1 = mlstm_layer(z, p['gate_b'], p['mh_g'], c0, n0, m0, seq_len)
    x, u = mix_out(x, gt1, hm, oa, z, cbuf, p['conv_w'], p['w_out'], seq_len)
    k_keep = k_all[:, -win_rows:].reshape(bsz, win_rows, SWA_KV_HEADS, SWA_DH)
    v_keep = v_all[:, -win_rows:].reshape(bsz, win_rows, SWA_KV_HEADS, SWA_DH)
    cbuf_new = jnp.concatenate([cbuf, u.reshape(bsz, seq_len, CONV_W)], axis=1)[:, -(CONV_WIDTH - 1):]
    x = peer_layer(x, p['g_ffn'], sc2, sh2, gt2, p['wq'], p['subkeys'], p['tab'])
    return x, (k_keep, v_keep, cbuf_new, c1, n1, m1)


def kernel(x_prompt, x_sample, cache_swa_k, cache_swa_v, state_conv, state_mlstm_C, state_mlstm_n, state_mlstm_m, c_prompt, c_sample, ada_w, ada_b, norm_mix_g, norm_ffn_g, w_in, w_out, mlstm_gate_b, mlstm_norm_g, swa_sinks, conv_w, peer_wq, peer_subkeys, peer_u, peer_v, final_g):
    bp, tp, d = x_prompt.shape
    bs, ts, _ = x_sample.shape
    xp, xs = x_prompt.reshape(bp * tp, d), x_sample.reshape(bs * ts, d)
    c_all = jnp.concatenate([c_prompt, c_sample], axis=0)
    bf = jnp.bfloat16
    n_gate = 2 * MLSTM_HEADS
    new_p, new_s = [], []
    for l in range(DEPTH):
        wl = w_in[l]
        w_perm = jnp.concatenate([wl[:, :Z_SQ], wl[:, Z_SQ + n_gate:], wl[:, Z_SQ:Z_SQ + n_gate],
                                  jnp.zeros((d, LANES - n_gate), wl.dtype)], axis=1)
        p = dict(g_mix=norm_mix_g[l].reshape(1, d), g_ffn=norm_ffn_g[l].reshape(1, d),
                 w_in=w_perm.astype(bf), w_out=w_out[l].astype(bf), gate_b=mlstm_gate_b[l],
                 mh_g=mlstm_norm_g[l], sinks=swa_sinks[l], conv_w=conv_w[l], wq=peer_wq[l].astype(bf),
                 subkeys=peer_subkeys[l].astype(bf), tab=pack_expert_table(peer_u, peer_v, l))
        mod = adaln_mod(c_all, ada_w[l], ada_b[l])
        xp, sp = trunk_layer(xp, tp, mod[:bp], p, None)
        st = (cache_swa_k[l], cache_swa_v[l], state_conv[l], state_mlstm_C[l], state_mlstm_n[l], state_mlstm_m[l])
        xs, ss = trunk_layer(xs, ts, mod[bp:], p, st)
        new_p.append(sp)
        new_s.append(ss)
    y_prompt = final_norm(xp, final_g).reshape(bp, tp, d)
    y_sample = final_norm(xs, final_g).reshape(bs, ts, d)
    pk, pv, pc, pC, pn, pm = [jnp.stack(t) for t in zip(*new_p)]
    sk, sv, sc, sC, sn, sm = [jnp.stack(t) for t in zip(*new_s)]
    return (y_prompt, y_sample, pk, pv, pc, pC, pn, pm, sk, sv, sc, sC, sn, sm)
```
